```python
import math
import jax, jax.numpy as jnp
from jax import lax
import numpy as np

D_MODEL = 4096
BATCH = 2
SEQ = 8192
DEPTH = 2

N_MIXERS = 2
RMS_EPS = 1e-6

DILATED_GROUPS = ((128, 1), (512, 4), (2048, 16))
N_GROUPS = 3
ATTN_HEAD_DIM = 128
ATTN_HEADS_PER_GROUP = 16
ATTN_HEADS = N_GROUPS * ATTN_HEADS_PER_GROUP
ATTN_BLOCK = 128
NEG_INF = -1e30

NUM_BUCKETS = 32
MAX_EXACT = 16
REL_MAX_DISTANCE = 2048

HG_HEAD_DIM = 128
HG_HEADS = D_MODEL // HG_HEAD_DIM
HG_CHUNK = 64

D_FF = -(-8 * D_MODEL // (3 * 256)) * 256

kernel_name = 'hybrid_dilated_attn_hgrn2_swiglu'


def rms_norm(x, gain):
    xf = x.astype(jnp.float32)
    y = xf * lax.rsqrt(jnp.mean(xf * xf, axis=-1, keepdims=True) + RMS_EPS)
    return (y * gain.astype(jnp.float32)).astype(x.dtype)


def t5_causal_bucket(dist):
    dist = jnp.maximum(dist, 0)
    log_ratio = jnp.log(jnp.maximum(dist, 1).astype(jnp.float32) / MAX_EXACT) / math.log(REL_MAX_DISTANCE / MAX_EXACT)
    large = MAX_EXACT + (log_ratio * (NUM_BUCKETS - MAX_EXACT)).astype(jnp.int32)
    large = jnp.minimum(large, NUM_BUCKETS - 1)
    return jnp.where(dist < MAX_EXACT, dist, large)


def dilated_window_group(q, k, v, rel_bias_g, window, dilation):
    B, S, H, Dh = q.shape
    span = window // dilation
    n = S // dilation
    nb = -(-n // ATTN_BLOCK)
    n_pad = nb * ATTN_BLOCK

    def to_blocks(a):
        a = a.reshape(B, n, dilation, H, Dh).transpose(0, 3, 2, 1, 4)
        a = jnp.pad(a, ((0, 0), (0, 0), (0, 0), (0, n_pad - n), (0, 0)))
        return a.reshape(B, H, dilation, nb, ATTN_BLOCK, Dh)

    def with_prev(a):
        prev = jnp.pad(a, ((0, 0), (0, 0), (0, 0), (1, 0), (0, 0), (0, 0)))[:, :, :, :-1]
        return jnp.concatenate([prev, a], axis=-2)

    qb = to_blocks(q)
    kc = with_prev(to_blocks(k))
    vc = with_prev(to_blocks(v))

    qi = jnp.arange(ATTN_BLOCK)[:, None]
    ki = jnp.arange(2 * ATTN_BLOCK)[None, :]
    rel = qi - ki + ATTN_BLOCK
    band = (rel >= 0) & (rel <= span)
    valid = band[None] & ((jnp.arange(nb)[:, None, None] > 0) | (ki[None] >= ATTN_BLOCK))
    bias = rel_bias_g[t5_causal_bucket(rel * dilation)].astype(jnp.float32)
    bias = bias.transpose(2, 0, 1)[None, :, None, None]

    logits = jnp.einsum('bhrnqd,bhrnkd->bhrnqk', qb, kc, preferred_element_type=jnp.float32) * (Dh ** -0.5) + bias
    logits = jnp.where(valid, logits, NEG_INF)
    lse = jax.nn.logsumexp(logits, axis=-1)
    probs = jnp.exp(logits - lse[..., None])
    out = jnp.einsum('bhrnqk,bhrnkd->bhrnqd', probs.astype(v.dtype), vc, preferred_element_type=jnp.float32)
    out = out.reshape(B, H, dilation, n_pad, Dh)[:, :, :, :n].transpose(0, 3, 2, 1, 4).reshape(B, S, H, Dh)
    lse = lse.reshape(B, H, dilation, n_pad)[..., :n].transpose(0, 3, 2, 1).reshape(B, S, H)
    return out, lse


def dilated_attention_mixer(y, w_in, w_out, rel_bias):
    B, S, _ = y.shape
    proj = (y @ w_in).reshape(B, S, N_GROUPS, 3, ATTN_HEADS_PER_GROUP, ATTN_HEAD_DIM)
    outs, lses = [], []
    for g, (window, dilation) in enumerate(DILATED_GROUPS):
        cols = rel_bias[:, g * ATTN_HEADS_PER_GROUP:(g + 1) * ATTN_HEADS_PER_GROUP]
        o, l = dilated_window_group(proj[:, :, g, 0], proj[:, :, g, 1], proj[:, :, g, 2], cols, window, dilation)
        outs.append(o)
        lses.append(l)
    alpha = jax.nn.softmax(jnp.stack(lses), axis=0)
    merged = jnp.sum(alpha[..., None] * jnp.stack(outs), axis=0)
    return merged.reshape(B, S, ATTN_HEADS_PER_GROUP * ATTN_HEAD_DIM).astype(y.dtype) @ w_out


def hgrn2_chunk_recurrence(q, k, v, log_f):
    B, H, S, Dk = q.shape
    Dv = v.shape[-1]
    nc = S // HG_CHUNK
    mid = HG_CHUNK // 2

    def chunks(a):
        return a.reshape(B, H, nc, HG_CHUNK, a.shape[-1]).transpose(2, 0, 1, 3, 4)

    causal = jnp.tril(jnp.ones((HG_CHUNK, HG_CHUNK), dtype=bool))

    def step(state, inp):
        qc, kc, vc, gc = inp
        b = jnp.cumsum(gc, axis=-2)
        b_ref = b[..., mid - 1:mid, :]
        b_last = b[..., -1:, :]
        inter = jnp.einsum('bhtd,bhde->bhte', qc * jnp.exp(b), state)
        scores = jnp.einsum('bhtd,bhsd->bhts', qc * jnp.exp(b - b_ref), kc * jnp.exp(b_ref - b))
        intra = jnp.einsum('bhts,bhse->bhte', jnp.where(causal, scores, 0.0), vc)
        new_state = state * jnp.exp(b_last[..., 0, :])[..., None] + jnp.einsum('bhsd,bhse->bhde', kc * jnp.exp(b_last - b), vc)
        return new_state, inter + intra

    state0 = jnp.zeros((B, H, Dk, Dv), jnp.float32)
    _, out = lax.scan(step, state0, (chunks(q), chunks(k), chunks(v), chunks(log_f)))
    return out.transpose(1, 2, 0, 3, 4).reshape(B, H, S, Dv)


def hgrn2_mixer(y, w_in, lower_bound, out_gain, w_out):
    B, S, _ = y.shape
    proj = (y @ w_in).reshape(B, S, 4, HG_HEADS, HG_HEAD_DIM).astype(jnp.float32)
    q_raw, f_raw, i_raw, g_raw = proj[:, :, 0], proj[:, :, 1], proj[:, :, 2], proj[:, :, 3]
    lb = lower_bound.astype(jnp.float32).reshape(HG_HEADS, HG_HEAD_DIM)
    forget = lb + (1.0 - lb) * jax.nn.sigmoid(f_raw)
    log_f = jnp.log(forget)
    key = 1.0 - forget
    q = jax.nn.silu(q_raw) * (HG_HEAD_DIM ** -0.5)
    to_bhsd = lambda a: a.transpose(0, 2, 1, 3)
    o = hgrn2_chunk_recurrence(to_bhsd(q), to_bhsd(key), to_bhsd(i_raw), to_bhsd(log_f))
    o = rms_norm(o, out_gain).transpose(0, 2, 1, 3) * jax.nn.silu(g_raw)
    return o.reshape(B, S, HG_HEADS * HG_HEAD_DIM).astype(y.dtype) @ w_out


def swiglu_ffn(y, w_in, w_out):
    h = y @ w_in
    gate, up = h[..., :D_FF], h[..., D_FF:]
    return (jax.nn.silu(gate) * up) @ w_out


def setup_inputs(seed: int = 0) -> dict:
    key = jax.random.key(seed)
    ks = jax.random.split(key, 11)
    n_attn = (DEPTH + 1) // 2
    n_hg = DEPTH // 2
    attn_in_cols = N_GROUPS * 3 * ATTN_HEADS_PER_GROUP * ATTN_HEAD_DIM
    attn_out_rows = ATTN_HEADS_PER_GROUP * ATTN_HEAD_DIM
    hg_width = HG_HEADS * HG_HEAD_DIM

    def dense(k, shape, fan_in):
        return jax.random.normal(k, shape, jnp.float32) * (fan_in ** -0.5)

    return {
        'x': jax.random.normal(ks[0], (BATCH, SEQ, D_MODEL), jnp.float32),
        'norm_gains': 1.0 + 0.05 * jax.random.normal(ks[1], (DEPTH, 4, D_MODEL), jnp.float32),
        'rel_bias': 0.1 * jax.random.normal(ks[2], (NUM_BUCKETS, ATTN_HEADS), jnp.float32),
        'attn_w_in': dense(ks[3], (n_attn, D_MODEL, attn_in_cols), D_MODEL),
        'attn_w_out': dense(ks[4], (n_attn, attn_out_rows, D_MODEL), attn_out_rows),
        'hgrn_w_in': dense(ks[5], (n_hg, D_MODEL, 4 * hg_width), D_MODEL),
        'hgrn_lb_logits': 1.0 + 0.1 * jax.random.normal(ks[6], (DEPTH, hg_width), jnp.float32),
        'hgrn_out_gain': 1.0 + 0.05 * jax.random.normal(ks[7], (n_hg, HG_HEAD_DIM), jnp.float32),
        'hgrn_w_out': dense(ks[8], (n_hg, hg_width, D_MODEL), hg_width),
        'ffn_w_in': dense(ks[9], (DEPTH, D_MODEL, 2 * D_FF), D_MODEL),
        'ffn_w_out': dense(ks[10], (DEPTH, D_FF, D_MODEL), D_FF),
    }


def reference(x, norm_gains, rel_bias, attn_w_in, attn_w_out, hgrn_w_in, hgrn_lb_logits,
              hgrn_out_gain, hgrn_w_out, ffn_w_in, ffn_w_out):
    lb_probs = jax.nn.softmax(hgrn_lb_logits.astype(jnp.float32), axis=0)
    lower_bounds = jnp.cumsum(lb_probs, axis=0) - lb_probs[0]
    h = x
    for i in range(DEPTH):
        gains = norm_gains[i]
        y = rms_norm(h, gains[0])
        if i % N_MIXERS == 0:
            y = dilated_attention_mixer(y, attn_w_in[i // N_MIXERS], attn_w_out[i // N_MIXERS], rel_bias)
        else:
            y = hgrn2_mixer(y, hgrn_w_in[i // N_MIXERS], lower_bounds[i], hgrn_out_gain[i // N_MIXERS], hgrn_w_out[i // N_MIXERS])
        h = h + rms_norm(y, gains[1])
        y = swiglu_ffn(rms_norm(h, gains[2]), ffn_w_in[i], ffn_w_out[i])
        h = h + rms_norm(y, gains[3])
    return h
```

```python
import functools
import math

import numpy as np
import jax
import jax.numpy as jnp
from jax import lax
from jax.experimental import pallas as pl
from jax.experimental.pallas import tpu as pltpu

RMS_EPS = 1e-6
NEG_INF = -1e30

DILATED_GROUPS = ((128, 1), (512, 4), (2048, 16))
HEAD_DIM = 128
ATTN_BLOCK = 128
NUM_BUCKETS = 32
MAX_EXACT = 16
REL_MAX_DISTANCE = 2048
HG_CHUNK = 64

V7X_VMEM_BYTES = 64 * 1024 * 1024
VMEM_LIMIT = V7X_VMEM_BYTES - 8 * 1024 * 1024

BF16 = jnp.bfloat16
F32 = jnp.float32


def _params(semantics, vmem=VMEM_LIMIT):
    return pltpu.CompilerParams(dimension_semantics=semantics, vmem_limit_bytes=vmem)


def _tile(n, pref):
    if n <= pref:
        return n
    t = pref - pref % 128
    while t >= 128:
        if n % t == 0:
            return t
        t -= 128
    raise ValueError(f"no 128-multiple tile of {n} below {pref}")


def _dot(a, b):
    return jnp.dot(a, b, preferred_element_type=F32)


def _dot_nt(a, b):
    return lax.dot_general(a, b, (((1,), (1,)), ((), ())), preferred_element_type=F32)


def _dot_tn(a, b):
    return lax.dot_general(a, b, (((0,), (0,)), ((), ())), preferred_element_type=F32)


def _rms(x, gain):
    return x * lax.rsqrt(jnp.mean(x * x, axis=-1, keepdims=True) + RMS_EPS) * gain


def _silu(x):
    return x * jax.nn.sigmoid(x)


def _prenorm_kernel(x_ref, g_ref, o_ref):
    o_ref[...] = _rms(x_ref[...], g_ref[...]).astype(o_ref.dtype)


def _prenorm(x, gain, tr=256):
    m, d = x.shape
    tr = _tile(m, tr)
    return pl.pallas_call(
        _prenorm_kernel,
        grid=(m // tr,),
        in_specs=[pl.BlockSpec((tr, d), lambda i: (i, 0)), pl.BlockSpec((1, d), lambda i: (0, 0))],
        out_specs=pl.BlockSpec((tr, d), lambda i: (i, 0)),
        out_shape=jax.ShapeDtypeStruct((m, d), BF16),
        compiler_params=_params(("parallel",)),
        name="prenorm",
    )(x, gain.reshape(1, d))


def _postnorm_kernel(h_ref, y_ref, ga_ref, gb_ref, h_out_ref, yn_ref):
    h = h_ref[...] + _rms(y_ref[...], ga_ref[...])
    h_out_ref[...] = h
    yn_ref[...] = _rms(h, gb_ref[...]).astype(yn_ref.dtype)


def _postnorm_last_kernel(h_ref, y_ref, ga_ref, h_out_ref):
    h_out_ref[...] = h_ref[...] + _rms(y_ref[...], ga_ref[...])


def _postnorm(h, y, gain_post, gain_next, tr=256):
    m, d = h.shape
    tr = _tile(m, tr)
    row = pl.BlockSpec((tr, d), lambda i: (i, 0))
    vec = pl.BlockSpec((1, d), lambda i: (0, 0))
    if gain_next is None:
        return pl.pallas_call(
            _postnorm_last_kernel,
            grid=(m // tr,),
            in_specs=[row, row, vec],
            out_specs=row,
            out_shape=jax.ShapeDtypeStruct((m, d), F32),
            compiler_params=_params(("parallel",)),
            name="postnorm_last",
        )(h, y, gain_post.reshape(1, d))
    return pl.pallas_call(
        _postnorm_kernel,
        grid=(m // tr,),
        in_specs=[row, row, vec, vec],
        out_specs=[row, row],
        out_shape=[jax.ShapeDtypeStruct((m, d), F32), jax.ShapeDtypeStruct((m, d), BF16)],
        compiler_params=_params(("parallel",)),
        name="postnorm",
    )(h, y, gain_post.reshape(1, d), gain_next.reshape(1, d))


def _matmul_kernel(x_ref, w_ref, o_ref):
    o_ref[...] = _dot(x_ref[...], w_ref[...]).astype(o_ref.dtype)


def _matmul(x, w, out_dtype, bm=1024, bn=1024):
    m, k = x.shape
    n = w.shape[1]
    bm, bn = _tile(m, bm), _tile(n, bn)
    return pl.pallas_call(
        _matmul_kernel,
        grid=(m // bm, n // bn),
        in_specs=[pl.BlockSpec((bm, k), lambda i, j: (i, 0)), pl.BlockSpec((k, bn), lambda i, j: (0, j))],
        out_specs=pl.BlockSpec((bm, bn), lambda i, j: (i, j)),
        out_shape=jax.ShapeDtypeStruct((m, n), out_dtype),
        compiler_params=_params(("parallel", "arbitrary")),
        name="matmul",
    )(x, w)


def _matmul_acc_kernel(x_ref, w_ref, o_ref, acc_ref):
    kk = pl.program_id(2)

    @pl.when(kk == 0)
    def _():
        acc_ref[...] = jnp.zeros_like(acc_ref)

    acc_ref[...] += _dot(x_ref[...], w_ref[...])

    @pl.when(kk == pl.num_programs(2) - 1)
    def _():
        o_ref[...] = acc_ref[...].astype(o_ref.dtype)


def _matmul_ktiled(x, w, out_dtype, bm=1024, bn=512, bk=5504):
    m, k = x.shape
    n = w.shape[1]
    bm, bn, bk = _tile(m, bm), _tile(n, bn), _tile(k, bk)
    return pl.pallas_call(
        _matmul_acc_kernel,
        grid=(m // bm, n // bn, k // bk),
        in_specs=[pl.BlockSpec((bm, bk), lambda i, j, kk: (i, kk)), pl.BlockSpec((bk, bn), lambda i, j, kk: (kk, j))],
        out_specs=pl.BlockSpec((bm, bn), lambda i, j, kk: (i, j)),
        out_shape=jax.ShapeDtypeStruct((m, n), out_dtype),
        scratch_shapes=[pltpu.VMEM((bm, bn), F32)],
        compiler_params=_params(("parallel", "arbitrary", "arbitrary")),
        name="matmul_ktiled",
    )(x, w)


def _ffn_in_kernel(x_ref, wg_ref, wu_ref, o_ref):
    x = x_ref[...]
    o_ref[...] = (_silu(_dot(x, wg_ref[...])) * _dot(x, wu_ref[...])).astype(o_ref.dtype)


def _ffn_in(x, w_in, bm=2048, bf=256):
    m, k = x.shape
    f = w_in.shape[1] // 2
    bm, bf = _tile(m, bm), _tile(f, bf)
    nf = f // bf
    return pl.pallas_call(
        _ffn_in_kernel,
        grid=(m // bm, nf),
        in_specs=[
            pl.BlockSpec((bm, k), lambda i, j: (i, 0)),
            pl.BlockSpec((k, bf), lambda i, j: (0, j)),
            pl.BlockSpec((k, bf), lambda i, j: (0, j + nf)),
        ],
        out_specs=pl.BlockSpec((bm, bf), lambda i, j: (i, j)),
        out_shape=jax.ShapeDtypeStruct((m, f), BF16),
        compiler_params=_params(("parallel", "arbitrary")),
        name="ffn_in",
    )(x, w_in, w_in)


def _t5_bucket_np(dist):
    dist = np.maximum(dist, 0)
    log_ratio = np.log(np.maximum(dist, 1).astype(np.float64) / MAX_EXACT) / math.log(REL_MAX_DISTANCE / MAX_EXACT)
    large = np.minimum(MAX_EXACT + (log_ratio * (NUM_BUCKETS - MAX_EXACT)).astype(np.int64), NUM_BUCKETS - 1)
    return np.where(dist < MAX_EXACT, dist, large)


def _bucket_tables():
    qi = np.arange(ATTN_BLOCK)[:, None]
    ki = np.arange(2 * ATTN_BLOCK)[None, :]
    rel = qi - ki + ATTN_BLOCK
    tables = []
    for window, dilation in DILATED_GROUPS:
        band = (rel >= 0) & (rel <= window // dilation)
        tables.append(np.where(band, _t5_bucket_np(rel * dilation), -1))
    return np.stack(tables).astype(np.int32)


def _bias_kernel(rb_ref, bucket_ref, o_ref, *, heads):
    g, h = pl.program_id(0), pl.program_id(1)
    bucket = bucket_ref[...]
    acc = jnp.full(bucket.shape, NEG_INF, F32)
    for b in range(NUM_BUCKETS):
        acc = jnp.where(bucket == b, rb_ref[b, g * heads + h], acc)
    o_ref[...] = acc


def _bias_tables(rel_bias, heads):
    ng = len(DILATED_GROUPS)
    buckets = jnp.asarray(_bucket_tables())
    return pl.pallas_call(
        functools.partial(_bias_kernel, heads=heads),
        grid=(ng, heads),
        in_specs=[
            pl.BlockSpec(memory_space=pltpu.SMEM),
            pl.BlockSpec((None, ATTN_BLOCK, 2 * ATTN_BLOCK), lambda g, h: (g, 0, 0)),
        ],
        out_specs=pl.BlockSpec((None, None, ATTN_BLOCK, 2 * ATTN_BLOCK), lambda g, h: (g, h, 0, 0)),
        out_shape=jax.ShapeDtypeStruct((ng, heads, ATTN_BLOCK, 2 * ATTN_BLOCK), F32),
        compiler_params=_params(("arbitrary", "arbitrary")),
        name="attn_bias",
    )(rel_bias, buckets)


def _attn_kernel(q_ref, kc_ref, kp_ref, vc_ref, vp_ref, bias_ref, o_ref, lse_ref, *, heads):
    has_prev = pl.program_id(2) > 0
    scale = HEAD_DIM ** -0.5
    lane = lax.broadcasted_iota(jnp.int32, (ATTN_BLOCK, ATTN_BLOCK), 1)
    lse_all = jnp.zeros((ATTN_BLOCK, ATTN_BLOCK), F32)
    for h in range(heads):
        cols = slice(h * HEAD_DIM, (h + 1) * HEAD_DIM)
        q = q_ref[:, cols]
        s_prev = _dot_nt(q, kp_ref[:, cols]) * scale + bias_ref[h, :, :ATTN_BLOCK]
        s_prev = jnp.where(has_prev, s_prev, NEG_INF)
        s_cur = _dot_nt(q, kc_ref[:, cols]) * scale + bias_ref[h, :, ATTN_BLOCK:]
        m = jnp.maximum(jnp.max(s_prev, axis=-1, keepdims=True), jnp.max(s_cur, axis=-1, keepdims=True))
        p_prev = jnp.exp(s_prev - m)
        p_cur = jnp.exp(s_cur - m)
        denom = jnp.sum(p_prev, axis=-1, keepdims=True) + jnp.sum(p_cur, axis=-1, keepdims=True)
        o = _dot(p_prev.astype(BF16), vp_ref[:, cols]) + _dot(p_cur.astype(BF16), vc_ref[:, cols])
        o_ref[:, cols] = (o / denom).astype(o_ref.dtype)
        lse_all = jnp.where(lane == h, m + jnp.log(denom), lse_all)
    lse_ref[...] = lse_all


def _attn_group(proj, bias_g, group, dilation, batch, seq, heads):
    width = heads * HEAD_DIM
    ncols = proj.shape[1] // width
    n = seq // dilation
    nb = n // ATTN_BLOCK
    pv = proj.reshape(batch, n, dilation * proj.shape[1])

    def col(which):
        return lambda b, r, i: (b, i, r * ncols + group * 3 + which)

    def col_prev(which):
        return lambda b, r, i: (b, jnp.maximum(i - 1, 0), r * ncols + group * 3 + which)

    blk = (None, ATTN_BLOCK, width)
    out, lse = pl.pallas_call(
        functools.partial(_attn_kernel, heads=heads),
        grid=(batch, dilation, nb),
        in_specs=[
            pl.BlockSpec(blk, col(0)),
            pl.BlockSpec(blk, col(1)),
            pl.BlockSpec(blk, col_prev(1)),
            pl.BlockSpec(blk, col(2)),
            pl.BlockSpec(blk, col_prev(2)),
            pl.BlockSpec((heads, ATTN_BLOCK, 2 * ATTN_BLOCK), lambda b, r, i: (0, 0, 0)),
        ],
        out_specs=[
            pl.BlockSpec(blk, lambda b, r, i: (b, i, r)),
            pl.BlockSpec((None, ATTN_BLOCK, ATTN_BLOCK), lambda b, r, i: (b, i, r)),
        ],
        out_shape=[
            jax.ShapeDtypeStruct((batch, n, dilation * width), BF16),
            jax.ShapeDtypeStruct((batch, n, dilation * ATTN_BLOCK), F32),
        ],
        compiler_params=_params(("parallel", "parallel", "arbitrary")),
        name=f"attn_d{dilation}",
    )(pv, pv, pv, pv, pv, bias_g)
    return out.reshape(batch * seq, width), lse.reshape(batch * seq, ATTN_BLOCK)


def _merge_outproj_kernel(o0_ref, o1_ref, o2_ref, l0_ref, l1_ref, l2_ref, w_ref, y_ref, merged_ref, *, heads):
    @pl.when(pl.program_id(1) == 0)
    def _():
        l0, l1, l2 = l0_ref[...], l1_ref[...], l2_ref[...]
        mx = jnp.maximum(jnp.maximum(l0, l1), l2)
        e0, e1, e2 = jnp.exp(l0 - mx), jnp.exp(l1 - mx), jnp.exp(l2 - mx)
        inv = 1.0 / (e0 + e1 + e2)
        a0, a1, a2 = e0 * inv, e1 * inv, e2 * inv
        for h in range(heads):
            cols = slice(h * HEAD_DIM, (h + 1) * HEAD_DIM)
            merged = (
                a0[:, h : h + 1] * o0_ref[:, cols].astype(F32)
                + a1[:, h : h + 1] * o1_ref[:, cols].astype(F32)
                + a2[:, h : h + 1] * o2_ref[:, cols].astype(F32)
            )
            merged_ref[:, cols] = merged.astype(merged_ref.dtype)

    y_ref[...] = _dot(merged_ref[...], w_ref[...])


def _merge_outproj(outs, lses, w_out, heads, bm=512, bn=1024):
    m, width = outs[0].shape
    n = w_out.shape[1]
    bm, bn = _tile(m, bm), _tile(n, bn)
    o_spec = pl.BlockSpec((bm, width), lambda i, j: (i, 0))
    l_spec = pl.BlockSpec((bm, ATTN_BLOCK), lambda i, j: (i, 0))
    return pl.pallas_call(
        functools.partial(_merge_outproj_kernel, heads=heads),
        grid=(m // bm, n // bn),
        in_specs=[o_spec, o_spec, o_spec, l_spec, l_spec, l_spec, pl.BlockSpec((width, bn), lambda i, j: (0, j))],
        out_specs=pl.BlockSpec((bm, bn), lambda i, j: (i, j)),
        out_shape=jax.ShapeDtypeStruct((m, n), F32),
        scratch_shapes=[pltpu.VMEM((bm, width), BF16)],
        compiler_params=_params(("parallel", "arbitrary")),
        name="attn_merge_outproj",
    )(*outs, *lses, w_out)


def _hgrn_in_kernel(x_ref, wq_ref, wf_ref, wi_ref, wg_ref, lbl_ref, q_ref, f_ref, v_ref, g_ref, *, layer):
    x = x_ref[...]
    logits = lbl_ref[...]
    p = jnp.exp(logits - jnp.max(logits, axis=0, keepdims=True))
    p = p / jnp.sum(p, axis=0, keepdims=True)
    lb = jnp.sum(p[: layer + 1], axis=0, keepdims=True) - p[0:1]
    q_ref[...] = _silu(_dot(x, wq_ref[...])) * (HEAD_DIM ** -0.5)
    f_ref[...] = lb + (1.0 - lb) * jax.nn.sigmoid(_dot(x, wf_ref[...]))
    v_ref[...] = _dot(x, wi_ref[...]).astype(v_ref.dtype)
    g_ref[...] = _silu(_dot(x, wg_ref[...]))


def _hgrn_in(x, w_in, lb_logits, layer, bm=1024, bn=256):
    m, k = x.shape
    width = w_in.shape[1] // 4
    depth = lb_logits.shape[0]
    bm, bn = _tile(m, bm), _tile(width, bn)
    nbn = width // bn

    def wspec(part):
        return pl.BlockSpec((k, bn), lambda i, j: (0, j + part * nbn))

    ospec = pl.BlockSpec((bm, bn), lambda i, j: (i, j))
    return pl.pallas_call(
        functools.partial(_hgrn_in_kernel, layer=layer),
        grid=(m // bm, nbn),
        in_specs=[pl.BlockSpec((bm, k), lambda i, j: (i, 0)), wspec(0), wspec(1), wspec(2), wspec(3),
                  pl.BlockSpec((depth, bn), lambda i, j: (0, j))],
        out_specs=[ospec, ospec, ospec, ospec],
        out_shape=[
            jax.ShapeDtypeStruct((m, width), F32),
            jax.ShapeDtypeStruct((m, width), F32),
            jax.ShapeDtypeStruct((m, width), BF16),
            jax.ShapeDtypeStruct((m, width), F32),
        ],
        compiler_params=_params(("parallel", "arbitrary")),
        name="hgrn_in",
    )(x, w_in, w_in, w_in, w_in, lb_logits)


def _hgrn_rec_kernel(q_ref, f_ref, v_ref, g_ref, gain_ref, o_ref, state_ref, *, heads_per_step, chunks):
    @pl.when(pl.program_id(2) == 0)
    def _():
        state_ref[...] = jnp.zeros_like(state_ref)

    row = lax.broadcasted_iota(jnp.int32, (HG_CHUNK, HG_CHUNK), 0)
    col = lax.broadcasted_iota(jnp.int32, (HG_CHUNK, HG_CHUNK), 1)
    causal = row >= col
    tri = causal.astype(BF16)
    mid = HG_CHUNK // 2
    gain = gain_ref[...]

    def chunk(c, carry):
        rows = pl.ds(pl.multiple_of(c * HG_CHUNK, HG_CHUNK), HG_CHUNK)
        for h in range(heads_per_step):
            cols = slice(h * HEAD_DIM, (h + 1) * HEAD_DIM)
            forget = f_ref[rows, cols]
            log_f = jnp.log(forget)
            key = 1.0 - forget
            hi = log_f.astype(BF16)
            lo = (log_f - hi.astype(F32)).astype(BF16)
            b = _dot(tri, hi) + _dot(tri, lo)
            b_mid = b[mid - 1 : mid, :]
            b_last = b[HG_CHUNK - 1 :, :]
            q = q_ref[rows, cols]
            v = v_ref[rows, cols]
            state = state_ref[h]
            inter = _dot_nt((q * jnp.exp(b)).astype(BF16), state.astype(BF16))
            scores = _dot_nt((q * jnp.exp(b - b_mid)).astype(BF16), (key * jnp.exp(b_mid - b)).astype(BF16))
            intra = _dot(jnp.where(causal, scores, 0.0).astype(BF16), v)
            state_ref[h] = state * jnp.exp(b_last) + _dot_tn(v, (key * jnp.exp(b_last - b)).astype(BF16))
            o = _rms(inter + intra, gain) * g_ref[rows, cols]
            o_ref[rows, cols] = o.astype(o_ref.dtype)
        return carry

    lax.fori_loop(0, chunks, chunk, 0)


def _hgrn_recurrence(q, f, v, g, out_gain, batch, seq, heads_per_step=4, ts=512):
    m, width = q.shape
    heads = width // HEAD_DIM
    heads_per_step = math.gcd(heads, heads_per_step)
    ts = _tile(seq, ts)
    bw = heads_per_step * HEAD_DIM
    nt = seq // ts
    spec = pl.BlockSpec((ts, bw), lambda b, hg, t: (b * nt + t, hg))
    return pl.pallas_call(
        functools.partial(_hgrn_rec_kernel, heads_per_step=heads_per_step, chunks=ts // HG_CHUNK),
        grid=(batch, heads // heads_per_step, nt),
        in_specs=[spec, spec, spec, spec, pl.BlockSpec((1, HEAD_DIM), lambda b, hg, t: (0, 0))],
        out_specs=spec,
        out_shape=jax.ShapeDtypeStruct((m, width), BF16),
        scratch_shapes=[pltpu.VMEM((heads_per_step, HEAD_DIM, HEAD_DIM), F32)],
        compiler_params=_params(("parallel", "parallel", "arbitrary")),
        name="hgrn_recurrence",
    )(q, f, v, g, out_gain.reshape(1, HEAD_DIM))


def kernel(x, norm_gains, rel_bias, attn_w_in, attn_w_out, hgrn_w_in, hgrn_lb_logits, hgrn_out_gain, hgrn_w_out,
           ffn_w_in, ffn_w_out):
    batch, seq, d = x.shape
    depth = norm_gains.shape[0]
    attn_heads = attn_w_out.shape[1] // HEAD_DIM
    h = x.reshape(batch * seq, d)
    yn = _prenorm(h, norm_gains[0, 0])
    bias = None
    for i in range(depth):
        gains = norm_gains[i]
        if i % 2 == 0:
            if bias is None:
                bias = _bias_tables(rel_bias, attn_heads)
            proj = _matmul(yn, attn_w_in[i // 2].astype(BF16), BF16)
            outs, lses = [], []
            for g, (_, dilation) in enumerate(DILATED_GROUPS):
                o, l = _attn_group(proj, bias[g], g, dilation, batch, seq, attn_heads)
                outs.append(o)
                lses.append(l)
            y = _merge_outproj(outs, lses, attn_w_out[i // 2].astype(BF16), attn_heads)
        else:
            q, f, v, g = _hgrn_in(yn, hgrn_w_in[i // 2].astype(BF16), hgrn_lb_logits, i)
            o = _hgrn_recurrence(q, f, v, g, hgrn_out_gain[i // 2], batch, seq)
            y = _matmul(o, hgrn_w_out[i // 2].astype(BF16), F32)
        h, yn = _postnorm(h, y, gains[1], gains[2])
        hidden = _ffn_in(yn, ffn_w_in[i].astype(BF16))
        y = _matmul_ktiled(hidden, ffn_w_out[i].astype(BF16), F32)
        if i + 1 < depth:
            h, yn = _postnorm(h, y, gains[3], norm_gains[i + 1, 0])
        else:
            h = _postnorm(h, y, gains[3], None)
    return h.reshape(batch, seq, d)
```

```python
import functools
import math

import numpy as np
import jax
import jax.numpy as jnp
from jax import lax
from jax.experimental import pallas as pl
from jax.experimental.pallas import tpu as pltpu

RMS_EPS = 1e-6
NEG_INF = -1e30

DILATED_GROUPS = ((128, 1), (512, 4), (2048, 16))
HEAD_DIM = 128
ATTN_BLOCK = 128
NUM_BUCKETS = 32
MAX_EXACT = 16
REL_MAX_DISTANCE = 2048
HG_CHUNK = 64

V7X_VMEM_BYTES = 64 * 1024 * 1024
VMEM_LIMIT = V7X_VMEM_BYTES - 8 * 1024 * 1024

BF16 = jnp.bfloat16
F32 = jnp.float32


def _params(semantics, vmem=VMEM_LIMIT):
    return pltpu.CompilerParams(dimension_semantics=semantics, vmem_limit_bytes=vmem)


def _tile(n, pref):
    if n <= pref:
        return n
    t = pref - pref % 128
    while t >= 128:
        if n % t == 0:
            return t
        t -= 128
    raise ValueError(f"no 128-multiple tile of {n} below {pref}")


def _dot(a, b):
    return jnp.dot(a, b, preferred_element_type=F32)


def _dot_nt(a, b):
    return lax.dot_general(a, b, (((1,), (1,)), ((), ())), preferred_element_type=F32)


def _dot_tn(a, b):
    return lax.dot_general(a, b, (((0,), (0,)), ((), ())), preferred_element_type=F32)


def _rms(x, gain):
    return x * lax.rsqrt(jnp.mean(x * x, axis=-1, keepdims=True) + RMS_EPS) * gain


def _silu(x):
    return x * jax.nn.sigmoid(x)


def _strided_rows(r, count, stride):
    return pl.ds(r, count) if stride == 1 else pl.ds(r, count, stride=stride)


def _prenorm_kernel(x_ref, g_ref, o_ref):
    o_ref[...] = _rms(x_ref[...], g_ref[...]).astype(o_ref.dtype)


def _prenorm(x, gain, tr=256):
    m, d = x.shape
    tr = _tile(m, tr)
    return pl.pallas_call(
        _prenorm_kernel,
        grid=(m // tr,),
        in_specs=[pl.BlockSpec((tr, d), lambda i: (i, 0)), pl.BlockSpec((1, d), lambda i: (0, 0))],
        out_specs=pl.BlockSpec((tr, d), lambda i: (i, 0)),
        out_shape=jax.ShapeDtypeStruct((m, d), BF16),
        compiler_params=_params(("parallel",)),
        name="prenorm",
    )(x, gain.reshape(1, d))


def _prenorm_dilated_kernel(x_ref, g_ref, *refs, dilations):
    out_refs, y_ref = refs[:-1], refs[-1]
    y = _rms(x_ref[...], g_ref[...])
    tr = y.shape[0]
    for c in range(y_ref.shape[0]):
        cols = slice(c * 128, (c + 1) * 128)
        y_ref[c] = y[:, cols]
        for o_ref, d in zip(out_refs, dilations):
            for r in range(d):
                o_ref[r, :, cols] = y_ref[c, _strided_rows(r, tr // d, d), :].astype(o_ref.dtype)


def _prenorm_dilated(x, gain, batch, seq, dilations, tr=256):
    m, d = x.shape
    tr = _tile(seq, tr)
    nt = seq // tr
    outs = pl.pallas_call(
        functools.partial(_prenorm_dilated_kernel, dilations=dilations),
        grid=(m // tr,),
        in_specs=[pl.BlockSpec((tr, d), lambda i: (i, 0)), pl.BlockSpec((1, d), lambda i: (0, 0))],
        out_specs=[pl.BlockSpec((None, dil, tr // dil, d), lambda i: (i // nt, 0, i % nt, 0)) for dil in dilations],
        out_shape=[jax.ShapeDtypeStruct((batch, dil, seq // dil, d), BF16) for dil in dilations],
        scratch_shapes=[pltpu.VMEM((d // 128, tr, 128), F32)],
        compiler_params=_params(("parallel",)),
        name="prenorm_dilated",
    )(x, gain.reshape(1, d))
    return [o.reshape(m, d) for o in outs]


def _postnorm_kernel(h_ref, y_ref, ga_ref, gb_ref, h_out_ref, yn_ref):
    h = h_ref[...] + _rms(y_ref[...], ga_ref[...])
    h_out_ref[...] = h
    yn_ref[...] = _rms(h, gb_ref[...]).astype(yn_ref.dtype)


def _postnorm_last_kernel(h_ref, y_ref, ga_ref, h_out_ref):
    h_out_ref[...] = h_ref[...] + _rms(y_ref[...], ga_ref[...])


def _postnorm(h, y, gain_post, gain_next, tr=256):
    m, d = h.shape
    tr = _tile(m, tr)
    row = pl.BlockSpec((tr, d), lambda i: (i, 0))
    vec = pl.BlockSpec((1, d), lambda i: (0, 0))
    if gain_next is None:
        return pl.pallas_call(
            _postnorm_last_kernel,
            grid=(m // tr,),
            in_specs=[row, row, vec],
            out_specs=row,
            out_shape=jax.ShapeDtypeStruct((m, d), F32),
            compiler_params=_params(("parallel",)),
            name="postnorm_last",
        )(h, y, gain_post.reshape(1, d))
    return pl.pallas_call(
        _postnorm_kernel,
        grid=(m // tr,),
        in_specs=[row, row, vec, vec],
        out_specs=[row, row],
        out_shape=[jax.ShapeDtypeStruct((m, d), F32), jax.ShapeDtypeStruct((m, d), BF16)],
        compiler_params=_params(("parallel",)),
        name="postnorm",
    )(h, y, gain_post.reshape(1, d), gain_next.reshape(1, d))


def _matmul_kernel(x_ref, w_ref, o_ref):
    o_ref[...] = _dot(x_ref[...], w_ref[...]).astype(o_ref.dtype)


def _matmul(x, w, out_dtype, bm=1024, bn=1024):
    m, k = x.shape
    n = w.shape[1]
    bm, bn = _tile(m, bm), _tile(n, bn)
    return pl.pallas_call(
        _matmul_kernel,
        grid=(m // bm, n // bn),
        in_specs=[pl.BlockSpec((bm, k), lambda i, j: (i, 0)), pl.BlockSpec((k, bn), lambda i, j: (0, j))],
        out_specs=pl.BlockSpec((bm, bn), lambda i, j: (i, j)),
        out_shape=jax.ShapeDtypeStruct((m, n), out_dtype),
        compiler_params=_params(("parallel", "arbitrary")),
        name="matmul",
    )(x, w)


def _matmul_acc_kernel(x_ref, w_ref, o_ref, acc_ref):
    kk = pl.program_id(2)

    @pl.when(kk == 0)
    def _():
        acc_ref[...] = jnp.zeros_like(acc_ref)

    acc_ref[...] += _dot(x_ref[...], w_ref[...])

    @pl.when(kk == pl.num_programs(2) - 1)
    def _():
        o_ref[...] = acc_ref[...].astype(o_ref.dtype)


def _matmul_ktiled(x, w, out_dtype, bm=1024, bn=512, bk=5504):
    m, k = x.shape
    n = w.shape[1]
    bm, bn, bk = _tile(m, bm), _tile(n, bn), _tile(k, bk)
    return pl.pallas_call(
        _matmul_acc_kernel,
        grid=(m // bm, n // bn, k // bk),
        in_specs=[pl.BlockSpec((bm, bk), lambda i, j, kk: (i, kk)), pl.BlockSpec((bk, bn), lambda i, j, kk: (kk, j))],
        out_specs=pl.BlockSpec((bm, bn), lambda i, j, kk: (i, j)),
        out_shape=jax.ShapeDtypeStruct((m, n), out_dtype),
        scratch_shapes=[pltpu.VMEM((bm, bn), F32)],
        compiler_params=_params(("parallel", "arbitrary", "arbitrary")),
        name="matmul_ktiled",
    )(x, w)


def _proj_kernel(*refs, n_w, has_aux, epilogue):
    x_ref, w_refs = refs[0], refs[1 : 1 + n_w]
    aux_ref = refs[1 + n_w] if has_aux else None
    o_ref = refs[1 + n_w + has_aux]
    wb_refs = refs[2 + n_w + has_aux :]

    @pl.when(pl.program_id(1) == 0)
    def _():
        for w_ref, wb_ref in zip(w_refs, wb_refs):
            wb_ref[...] = w_ref[...].astype(wb_ref.dtype)

    x = x_ref[...]
    accs = [_dot(x, wb_ref[...]) for wb_ref in wb_refs]
    out = epilogue(*accs, aux_ref[...]) if has_aux else epilogue(*accs)
    o_ref[...] = out.astype(o_ref.dtype)


def _proj(x, w, col_offsets, width, epilogue, out_dtype, aux=None, bm=1024, bn=512, name="proj"):
    m, k = x.shape
    bm, bn = _tile(m, bm), _tile(width, bn)
    n_w = len(col_offsets)
    assert all(c % bn == 0 for c in col_offsets), (col_offsets, bn)
    w_specs =[pl.BlockSpec((k, bn), functools.partial(lambda j, i, c: (0, j + c), c=c // bn)) for c in col_offsets]
    aux_specs = [] if aux is None else [pl.BlockSpec((aux.shape[0], bn), lambda j, i: (0, j))]
    return pl.pallas_call(
        functools.partial(_proj_kernel, n_w=n_w, has_aux=aux is not None, epilogue=epilogue),
        grid=(width // bn, m // bm),
        in_specs=[pl.BlockSpec((bm, k), lambda j, i: (i, 0))] + w_specs + aux_specs,
        out_specs=pl.BlockSpec((bm, bn), lambda j, i: (i, j)),
        out_shape=jax.ShapeDtypeStruct((m, width), out_dtype),
        scratch_shapes=[pltpu.VMEM((k, bn), BF16) for _ in range(n_w)],
        compiler_params=_params(("parallel", "arbitrary")),
        name=name,
    )(x, *([w] * n_w), *([] if aux is None else [aux]))


def _identity(acc):
    return acc


def _swiglu(gate, up):
    return _silu(gate) * up


def _t5_bucket_np(dist):
    dist = np.maximum(dist, 0)
    log_ratio = np.log(np.maximum(dist, 1).astype(np.float64) / MAX_EXACT) / math.log(REL_MAX_DISTANCE / MAX_EXACT)
    large = np.minimum(MAX_EXACT + (log_ratio * (NUM_BUCKETS - MAX_EXACT)).astype(np.int64), NUM_BUCKETS - 1)
    return np.where(dist < MAX_EXACT, dist, large)


def _bucket_tables():
    qi = np.arange(ATTN_BLOCK)[:, None]
    ki = np.arange(2 * ATTN_BLOCK)[None, :]
    rel = qi - ki + ATTN_BLOCK
    tables = []
    for window, dilation in DILATED_GROUPS:
        band = (rel >= 0) & (rel <= window // dilation)
        tables.append(np.where(band, _t5_bucket_np(rel * dilation), -1))
    return np.stack(tables).astype(np.int32)


def _bias_kernel(rb_ref, bucket_ref, o_ref, *, heads):
    g, h = pl.program_id(0), pl.program_id(1)
    bucket = bucket_ref[...]
    acc = jnp.full(bucket.shape, NEG_INF, F32)
    for b in range(NUM_BUCKETS):
        acc = jnp.where(bucket == b, rb_ref[b, g * heads + h], acc)
    o_ref[...] = acc


def _bias_tables(rel_bias, heads):
    ng = len(DILATED_GROUPS)
    buckets = jnp.asarray(_bucket_tables())
    return pl.pallas_call(
        functools.partial(_bias_kernel, heads=heads),
        grid=(ng, heads),
        in_specs=[
            pl.BlockSpec(memory_space=pltpu.SMEM),
            pl.BlockSpec((None, ATTN_BLOCK, 2 * ATTN_BLOCK), lambda g, h: (g, 0, 0)),
        ],
        out_specs=pl.BlockSpec((None, None, ATTN_BLOCK, 2 * ATTN_BLOCK), lambda g, h: (g, h, 0, 0)),
        out_shape=jax.ShapeDtypeStruct((ng, heads, ATTN_BLOCK, 2 * ATTN_BLOCK), F32),
        compiler_params=_params(("arbitrary", "arbitrary")),
        name="attn_bias",
    )(rel_bias, buckets)


def _attn_kernel(q_ref, kc_ref, kp_ref, vc_ref, vp_ref, bias_ref, o_ref, lse_ref, *, heads):
    has_prev = pl.program_id(1) > 0
    scale = HEAD_DIM ** -0.5
    lane = lax.broadcasted_iota(jnp.int32, (ATTN_BLOCK, ATTN_BLOCK), 1)
    lse_all = jnp.zeros((ATTN_BLOCK, ATTN_BLOCK), F32)
    for h in range(heads):
        cols = slice(h * HEAD_DIM, (h + 1) * HEAD_DIM)
        q = q_ref[:, cols]
        s_prev = _dot_nt(q, kp_ref[:, cols]) * scale + bias_ref[h, :, :ATTN_BLOCK]
        s_prev = jnp.where(has_prev, s_prev, NEG_INF)
        s_cur = _dot_nt(q, kc_ref[:, cols]) * scale + bias_ref[h, :, ATTN_BLOCK:]
        m = jnp.maximum(jnp.max(s_prev, axis=-1, keepdims=True), jnp.max(s_cur, axis=-1, keepdims=True))
        p_prev = jnp.exp(s_prev - m)
        p_cur = jnp.exp(s_cur - m)
        denom = jnp.sum(p_prev, axis=-1, keepdims=True) + jnp.sum(p_cur, axis=-1, keepdims=True)
        o = _dot(p_prev.astype(BF16), vp_ref[:, cols]) + _dot(p_cur.astype(BF16), vc_ref[:, cols])
        o_ref[:, cols] = (o / denom).astype(o_ref.dtype)
        lse_all = jnp.where(lane == h, m + jnp.log(denom), lse_all)
    lse_ref[...] = lse_all


def _attn_group(proj, bias_g, sub_len, heads):
    rows = proj.shape[0]
    width = heads * HEAD_DIM
    nb = sub_len // ATTN_BLOCK
    blk = (ATTN_BLOCK, width)

    def cur(which):
        return lambda s, i: (s * nb + i, which)

    def prev(which):
        return lambda s, i: (s * nb + jnp.maximum(i - 1, 0), which)

    return pl.pallas_call(
        functools.partial(_attn_kernel, heads=heads),
        grid=(rows // sub_len, nb),
        in_specs=[
            pl.BlockSpec(blk, cur(0)),
            pl.BlockSpec(blk, cur(1)),
            pl.BlockSpec(blk, prev(1)),
            pl.BlockSpec(blk, cur(2)),
            pl.BlockSpec(blk, prev(2)),
            pl.BlockSpec((heads, ATTN_BLOCK, 2 * ATTN_BLOCK), lambda s, i: (0, 0, 0)),
        ],
        out_specs=[
            pl.BlockSpec(blk, lambda s, i: (s * nb + i, 0)),
            pl.BlockSpec((ATTN_BLOCK, ATTN_BLOCK), lambda s, i: (s * nb + i, 0)),
        ],
        out_shape=[jax.ShapeDtypeStruct((rows, width), BF16), jax.ShapeDtypeStruct((rows, ATTN_BLOCK), F32)],
        compiler_params=_params(("parallel", "arbitrary")),
        name="attn_group",
    )(proj, proj, proj, proj, proj, bias_g)


def _merge_outproj_kernel(*refs, heads, dilations):
    ng = len(dilations)
    o_refs, l_refs = refs[:ng], refs[ng : 2 * ng]
    w_ref, y_ref, merged_ref, alpha_ref, acc_ref = refs[2 * ng :]
    bm = acc_ref.shape[1]

    @pl.when(pl.program_id(1) == 0)
    def _():
        for g, d in enumerate(dilations):
            for r in range(d):
                alpha_ref[g, _strided_rows(r, bm // d, d), :] = l_refs[g][r]
        lses = [alpha_ref[g] for g in range(ng)]
        mx = functools.reduce(jnp.maximum, lses)
        es = [jnp.exp(l - mx) for l in lses]
        inv = 1.0 / functools.reduce(jnp.add, es)
        for g in range(ng):
            alpha_ref[g] = es[g] * inv
        for g, d in enumerate(dilations):
            for r in range(d):
                rows = _strided_rows(r, bm // d, d)
                alpha = alpha_ref[g, rows, :]
                for h in range(heads):
                    cols = slice(h * HEAD_DIM, (h + 1) * HEAD_DIM)
                    term = alpha[:, h : h + 1] * o_refs[g][r, :, cols].astype(F32)
                    if g == 0:
                        acc_ref[h, rows, :] = term
                    else:
                        acc_ref[h, rows, :] += term
        for h in range(heads):
            merged_ref[:, h * HEAD_DIM : (h + 1) * HEAD_DIM] = acc_ref[h].astype(merged_ref.dtype)

    y_ref[...] = _dot(merged_ref[...], w_ref[...])


def _merge_outproj(outs, lses, w_out, heads, batch, seq, dilations, bm=512, bn=1024):
    width = heads * HEAD_DIM
    n = w_out.shape[1]
    bm, bn = _tile(seq, bm), _tile(n, bn)
    nt = seq // bm
    o_specs = [pl.BlockSpec((None, d, bm // d, width), lambda i, j: (i // nt, 0, i % nt, 0)) for d in dilations]
    l_specs = [pl.BlockSpec((None, d, bm // d, ATTN_BLOCK), lambda i, j: (i // nt, 0, i % nt, 0)) for d in dilations]
    outs = [o.reshape(batch, d, seq // d, width) for o, d in zip(outs, dilations)]
    lses = [l.reshape(batch, d, seq // d, ATTN_BLOCK) for l, d in zip(lses, dilations)]
    return pl.pallas_call(
        functools.partial(_merge_outproj_kernel, heads=heads, dilations=dilations),
        grid=(batch * nt, n // bn),
        in_specs=o_specs + l_specs + [pl.BlockSpec((width, bn), lambda i, j: (0, j))],
        out_specs=pl.BlockSpec((bm, bn), lambda i, j: (i, j)),
        out_shape=jax.ShapeDtypeStruct((batch * seq, n), F32),
        scratch_shapes=[
            pltpu.VMEM((bm, width), BF16),
            pltpu.VMEM((len(dilations), bm, ATTN_BLOCK), F32),
            pltpu.VMEM((heads, bm, HEAD_DIM), F32),
        ],
        compiler_params=_params(("parallel", "arbitrary")),
        name="attn_merge_outproj",
    )(*outs, *lses, w_out)


def _hgrn_query(acc):
    return _silu(acc) * (HEAD_DIM ** -0.5)


def _hgrn_forget(acc, lb_logits, *, layer):
    p = jnp.exp(lb_logits - jnp.max(lb_logits, axis=0, keepdims=True))
    p = p / jnp.sum(p, axis=0, keepdims=True)
    lb = jnp.sum(p[: layer + 1], axis=0, keepdims=True) - p[0:1]
    return lb + (1.0 - lb) * jax.nn.sigmoid(acc)


def _hgrn_rec_kernel(q_ref, f_ref, v_ref, g_ref, gain_ref, o_ref, state_ref, *, heads_per_step, chunks):
    @pl.when(pl.program_id(2) == 0)
    def _():
        state_ref[...] = jnp.zeros_like(state_ref)

    row = lax.broadcasted_iota(jnp.int32, (HG_CHUNK, HG_CHUNK), 0)
    col = lax.broadcasted_iota(jnp.int32, (HG_CHUNK, HG_CHUNK), 1)
    causal = row >= col
    tri = causal.astype(BF16)
    tri2 = jnp.concatenate([tri, tri], axis=1)
    mid = HG_CHUNK // 2
    gain = gain_ref[...]

    def chunk(c, carry):
        rows = pl.ds(pl.multiple_of(c * HG_CHUNK, HG_CHUNK), HG_CHUNK)
        log_f = jnp.log(f_ref[rows, :])
        hi = log_f.astype(BF16)
        lo = (log_f - hi.astype(F32)).astype(BF16)
        b_all = _dot(tri2, jnp.concatenate([hi, lo], axis=0))
        for h in range(heads_per_step):
            cols = slice(h * HEAD_DIM, (h + 1) * HEAD_DIM)
            b = b_all[:, cols]
            b_mid = b[mid - 1 : mid, :]
            b_last = b[HG_CHUNK - 1 :, :]
            key = 1.0 - f_ref[rows, cols]
            q = q_ref[rows, cols]
            v = v_ref[rows, cols]
            state = state_ref[h]
            scores = _dot_nt((q * jnp.exp(b - b_mid)).astype(BF16), (key * jnp.exp(b_mid - b)).astype(BF16))
            probs = jnp.where(causal, scores, 0.0).astype(BF16)
            lhs = jnp.concatenate([(q * jnp.exp(b)).astype(BF16), probs], axis=1)
            o = _dot(lhs, jnp.concatenate([state.astype(BF16), v], axis=0))
            decay = jnp.transpose(jnp.broadcast_to(jnp.exp(b_last), (8, HEAD_DIM)))[:, 0:1]
            state_ref[h] = state * decay + _dot_tn((key * jnp.exp(b_last - b)).astype(BF16), v)
            o_ref[rows, cols] = (_rms(o, gain) * g_ref[rows, cols]).astype(o_ref.dtype)
        return carry

    lax.fori_loop(0, chunks, chunk, 0, unroll=2)


def _hgrn_recurrence(q, f, v, g, out_gain, batch, seq, heads_per_step=4, ts=512):
    m, width = q.shape
    heads = width // HEAD_DIM
    heads_per_step = math.gcd(heads, heads_per_step)
    ts = _tile(seq, ts)
    bw = heads_per_step * HEAD_DIM
    nt = seq // ts
    spec = pl.BlockSpec((ts, bw), lambda b, hg, t: (b * nt + t, hg))
    return pl.pallas_call(
        functools.partial(_hgrn_rec_kernel, heads_per_step=heads_per_step, chunks=ts // HG_CHUNK),
        grid=(batch, heads // heads_per_step, nt),
        in_specs=[spec, spec, spec, spec, pl.BlockSpec((1, HEAD_DIM), lambda b, hg, t: (0, 0))],
        out_specs=spec,
        out_shape=jax.ShapeDtypeStruct((m, width), BF16),
        scratch_shapes=[pltpu.VMEM((heads_per_step, HEAD_DIM, HEAD_DIM), F32)],
        compiler_params=_params(("parallel", "parallel", "arbitrary")),
        name="hgrn_recurrence",
    )(q, f, v, g, out_gain.reshape(1, HEAD_DIM))


def kernel(x, norm_gains, rel_bias, attn_w_in, attn_w_out, hgrn_w_in, hgrn_lb_logits, hgrn_out_gain, hgrn_w_out,
           ffn_w_in, ffn_w_out):
    batch, seq, d = x.shape
    m = batch * seq
    depth = norm_gains.shape[0]
    attn_heads = attn_w_out.shape[1] // HEAD_DIM
    attn_width = attn_heads * HEAD_DIM
    dilations = tuple(dil for _, dil in DILATED_GROUPS)
    d_ff = ffn_w_out.shape[1]
    h = x.reshape(m, d)
    yn = None
    bias = None
    for i in range(depth):
        gains = norm_gains[i]
        if i % 2 == 0:
            if bias is None:
                bias = _bias_tables(rel_bias, attn_heads)
            yns = _prenorm_dilated(h, gains[0], batch, seq, dilations)
            w_in = attn_w_in[i // 2]
            outs, lses = [], []
            for g, dil in enumerate(dilations):
                proj = _proj(yns[g], w_in, [g * 3 * attn_width], 3 * attn_width, _identity, BF16, name="attn_in")
                o, l = _attn_group(proj, bias[g], seq // dil, attn_heads)
                outs.append(o)
                lses.append(l)
            y = _merge_outproj(outs, lses, attn_w_out[i // 2].astype(BF16), attn_heads, batch, seq, dilations)
        else:
            if yn is None:
                yn = _prenorm(h, gains[0])
            w_in = hgrn_w_in[i // 2]
            width = w_in.shape[1] // 4
            q = _proj(yn, w_in, [0], width, _hgrn_query, F32, name="hgrn_in_q")
            f = _proj(yn, w_in, [width], width, functools.partial(_hgrn_forget, layer=i), F32, aux=hgrn_lb_logits,
                      name="hgrn_in_f")
            v = _proj(yn, w_in, [2 * width], width, _identity, BF16, name="hgrn_in_i")
            g = _proj(yn, w_in, [3 * width], width, _silu, F32, name="hgrn_in_g")
            o = _hgrn_recurrence(q, f, v, g, hgrn_out_gain[i // 2], batch, seq)
            y = _matmul(o, hgrn_w_out[i // 2].astype(BF16), F32)
        h, yn = _postnorm(h, y, gains[1], gains[2])
        hidden = _proj(yn, ffn_w_in[i], [0, d_ff], d_ff, _swiglu, BF16, bn=256, name="ffn_in")
        y = _matmul_ktiled(hidden, ffn_w_out[i].astype(BF16), F32)
        yn = None
        if i + 1 == depth:
            h = _postnorm(h, y, gains[3], None)
        elif (i + 1) % 2 == 0:
            h = _postnorm(h, y, gains[3], None)
        else:
            h, yn = _postnorm(h, y, gains[3], norm_gains[i + 1, 0])
    return h.reshape(batch, seq, d)
```

```python
import functools
import math

import numpy as np
import jax
import jax.numpy as jnp
from jax import lax
from jax.experimental import pallas as pl
from jax.experimental.pallas import tpu as pltpu

RMS_EPS = 1e-6
NEG_INF = -1e30

DILATED_GROUPS = ((128, 1), (512, 4), (2048, 16))
HEAD_DIM = 128
ATTN_BLOCK = 128
NUM_BUCKETS = 32
MAX_EXACT = 16
REL_MAX_DISTANCE = 2048
HG_CHUNK = 64

V7X_VMEM_BYTES = 64 * 1024 * 1024
VMEM_LIMIT = V7X_VMEM_BYTES - 8 * 1024 * 1024

BF16 = jnp.bfloat16
F32 = jnp.float32


def _params(semantics, vmem=VMEM_LIMIT):
    return pltpu.CompilerParams(dimension_semantics=semantics, vmem_limit_bytes=vmem)


def _tile(n, pref):
    if n <= pref:
        return n
    t = pref - pref % 128
    while t >= 128:
        if n % t == 0:
            return t
        t -= 128
    raise ValueError(f"no 128-multiple tile of {n} below {pref}")


def _dot(a, b):
    return jnp.dot(a, b, preferred_element_type=F32)


def _dot_nt(a, b):
    return lax.dot_general(a, b, (((1,), (1,)), ((), ())), preferred_element_type=F32)


def _dot_tn(a, b):
    return lax.dot_general(a, b, (((0,), (0,)), ((), ())), preferred_element_type=F32)


def _rms(x, gain):
    return x * lax.rsqrt(jnp.mean(x * x, axis=-1, keepdims=True) + RMS_EPS) * gain


def _silu(x):
    return x * jax.nn.sigmoid(x)


def _strided_rows(r, count, stride):
    return pl.ds(r, count) if stride == 1 else pl.ds(r, count, stride=stride)


def _prenorm_kernel(x_ref, g_ref, o_ref):
    o_ref[...] = _rms(x_ref[...], g_ref[...]).astype(o_ref.dtype)


def _prenorm(x, gain, tr=256):
    m, d = x.shape
    tr = _tile(m, tr)
    return pl.pallas_call(
        _prenorm_kernel,
        grid=(m // tr,),
        in_specs=[pl.BlockSpec((tr, d), lambda i: (i, 0)), pl.BlockSpec((1, d), lambda i: (0, 0))],
        out_specs=pl.BlockSpec((tr, d), lambda i: (i, 0)),
        out_shape=jax.ShapeDtypeStruct((m, d), BF16),
        compiler_params=_params(("parallel",)),
        name="prenorm",
    )(x, gain.reshape(1, d))


def _prenorm_dilated_kernel(x_ref, g_ref, *refs, dilations):
    out_refs, y_ref = refs[:-1], refs[-1]
    y = _rms(x_ref[...], g_ref[...])
    tr = y.shape[0]
    for c in range(y_ref.shape[0]):
        cols = slice(c * 128, (c + 1) * 128)
        y_ref[c] = y[:, cols]
        for o_ref, d in zip(out_refs, dilations):
            for r in range(d):
                o_ref[r, :, cols] = y_ref[c, _strided_rows(r, tr // d, d), :].astype(o_ref.dtype)


def _prenorm_dilated(x, gain, batch, seq, dilations, tr=256):
    m, d = x.shape
    tr = _tile(seq, tr)
    nt = seq // tr
    outs = pl.pallas_call(
        functools.partial(_prenorm_dilated_kernel, dilations=dilations),
        grid=(m // tr,),
        in_specs=[pl.BlockSpec((tr, d), lambda i: (i, 0)), pl.BlockSpec((1, d), lambda i: (0, 0))],
        out_specs=[pl.BlockSpec((None, dil, tr // dil, d), lambda i: (i // nt, 0, i % nt, 0)) for dil in dilations],
        out_shape=[jax.ShapeDtypeStruct((batch, dil, seq // dil, d), BF16) for dil in dilations],
        scratch_shapes=[pltpu.VMEM((d // 128, tr, 128), F32)],
        compiler_params=_params(("parallel",)),
        name="prenorm_dilated",
    )(x, gain.reshape(1, d))
    return [o.reshape(m, d) for o in outs]


def _postnorm_kernel(h_ref, y_ref, ga_ref, gb_ref, h_out_ref, yn_ref):
    h = h_ref[...] + _rms(y_ref[...], ga_ref[...])
    h_out_ref[...] = h
    yn_ref[...] = _rms(h, gb_ref[...]).astype(yn_ref.dtype)


def _postnorm_last_kernel(h_ref, y_ref, ga_ref, h_out_ref):
    h_out_ref[...] = h_ref[...] + _rms(y_ref[...], ga_ref[...])


def _postnorm(h, y, gain_post, gain_next, tr=256):
    m, d = h.shape
    tr = _tile(m, tr)
    row = pl.BlockSpec((tr, d), lambda i: (i, 0))
    vec = pl.BlockSpec((1, d), lambda i: (0, 0))
    if gain_next is None:
        return pl.pallas_call(
            _postnorm_last_kernel,
            grid=(m // tr,),
            in_specs=[row, row, vec],
            out_specs=row,
            out_shape=jax.ShapeDtypeStruct((m, d), F32),
            compiler_params=_params(("parallel",)),
            name="postnorm_last",
        )(h, y, gain_post.reshape(1, d))
    return pl.pallas_call(
        _postnorm_kernel,
        grid=(m // tr,),
        in_specs=[row, row, vec, vec],
        out_specs=[row, row],
        out_shape=[jax.ShapeDtypeStruct((m, d), F32), jax.ShapeDtypeStruct((m, d), BF16)],
        compiler_params=_params(("parallel",)),
        name="postnorm",
    )(h, y, gain_post.reshape(1, d), gain_next.reshape(1, d))


def _matmul_kernel(x_ref, w_ref, o_ref):
    o_ref[...] = _dot(x_ref[...], w_ref[...]).astype(o_ref.dtype)


def _matmul(x, w, out_dtype, bm=1024, bn=1024):
    m, k = x.shape
    n = w.shape[1]
    bm, bn = _tile(m, bm), _tile(n, bn)
    return pl.pallas_call(
        _matmul_kernel,
        grid=(m // bm, n // bn),
        in_specs=[pl.BlockSpec((bm, k), lambda i, j: (i, 0)), pl.BlockSpec((k, bn), lambda i, j: (0, j))],
        out_specs=pl.BlockSpec((bm, bn), lambda i, j: (i, j)),
        out_shape=jax.ShapeDtypeStruct((m, n), out_dtype),
        compiler_params=_params(("parallel", "arbitrary")),
        name="matmul",
    )(x, w)


def _matmul_acc_kernel(x_ref, w_ref, o_ref, acc_ref):
    kk = pl.program_id(2)

    @pl.when(kk == 0)
    def _():
        acc_ref[...] = jnp.zeros_like(acc_ref)

    acc_ref[...] += _dot(x_ref[...], w_ref[...])

    @pl.when(kk == pl.num_programs(2) - 1)
    def _():
        o_ref[...] = acc_ref[...].astype(o_ref.dtype)


def _matmul_ktiled(x, w, out_dtype, bm=1024, bn=512, bk=5504):
    m, k = x.shape
    n = w.shape[1]
    bm, bn, bk = _tile(m, bm), _tile(n, bn), _tile(k, bk)
    return pl.pallas_call(
        _matmul_acc_kernel,
        grid=(m // bm, n // bn, k // bk),
        in_specs=[pl.BlockSpec((bm, bk), lambda i, j, kk: (i, kk)), pl.BlockSpec((bk, bn), lambda i, j, kk: (kk, j))],
        out_specs=pl.BlockSpec((bm, bn), lambda i, j, kk: (i, j)),
        out_shape=jax.ShapeDtypeStruct((m, n), out_dtype),
        scratch_shapes=[pltpu.VMEM((bm, bn), F32)],
        compiler_params=_params(("parallel", "arbitrary", "arbitrary")),
        name="matmul_ktiled",
    )(x, w)


def _proj_kernel(*refs, n_w, has_aux, epilogue):
    x_ref, w_refs = refs[0], refs[1 : 1 + n_w]
    aux_ref = refs[1 + n_w] if has_aux else None
    o_ref = refs[1 + n_w + has_aux]
    wb_refs = refs[2 + n_w + has_aux :]
    jj, i = pl.program_id(0), pl.program_id(1)
    slab = w_refs[0].shape[0]

    @pl.when(jj < pl.num_programs(0) - 1)
    def _():
        rows = pl.ds(pl.multiple_of(i * slab, slab), slab)
        for w_ref, wb_ref in zip(w_refs, wb_refs):
            wb_ref[jj % 2, rows, :] = w_ref[...].astype(wb_ref.dtype)

    @pl.when(jj > 0)
    def _():
        x = x_ref[...]
        accs = [_dot(x, wb_ref[(jj - 1) % 2]) for wb_ref in wb_refs]
        out = epilogue(*accs, aux_ref[...]) if has_aux else epilogue(*accs)
        o_ref[...] = out.astype(o_ref.dtype)


def _proj(x, w, layer, col_offsets, width, epilogue, out_dtype, aux=None, bm=1024, bn=1024, name="proj"):
    m, k = x.shape
    bm, bn = _tile(m, bm), _tile(width, bn)
    n_w = len(col_offsets)
    nj, ni = width // bn, m // bm
    slab = k // ni
    assert k % ni == 0 and slab % 16 == 0, (k, ni)
    assert all(c % bn == 0 for c in col_offsets), (col_offsets, bn)

    def w_map(jj, i, c):
        return (layer, jnp.where(jj < nj, i, ni - 1), jnp.minimum(jj, nj - 1) + c)

    def row_map(jj, i):
        return jnp.where(jj > 0, i, 0)

    def col_map(jj):
        return jnp.maximum(jj - 1, 0)

    w_specs = [pl.BlockSpec((None, slab, bn), functools.partial(w_map, c=c // bn)) for c in col_offsets]
    aux_specs = [] if aux is None else [pl.BlockSpec((aux.shape[0], bn), lambda jj, i: (0, col_map(jj)))]
    return pl.pallas_call(
        functools.partial(_proj_kernel, n_w=n_w, has_aux=aux is not None, epilogue=epilogue),
        grid=(nj + 1, ni),
        in_specs=[pl.BlockSpec((bm, k), lambda jj, i: (row_map(jj, i), 0))] + w_specs + aux_specs,
        out_specs=pl.BlockSpec((bm, bn), lambda jj, i: (row_map(jj, i), col_map(jj))),
        out_shape=jax.ShapeDtypeStruct((m, width), out_dtype),
        scratch_shapes=[pltpu.VMEM((2, k, bn), BF16) for _ in range(n_w)],
        compiler_params=_params(("arbitrary", "arbitrary")),
        name=name,
    )(x, *([w] * n_w), *([] if aux is None else [aux]))


def _identity(acc):
    return acc


def _swiglu(gate, up):
    return _silu(gate) * up


def _t5_bucket_np(dist):
    dist = np.maximum(dist, 0)
    log_ratio = np.log(np.maximum(dist, 1).astype(np.float64) / MAX_EXACT) / math.log(REL_MAX_DISTANCE / MAX_EXACT)
    large = np.minimum(MAX_EXACT + (log_ratio * (NUM_BUCKETS - MAX_EXACT)).astype(np.int64), NUM_BUCKETS - 1)
    return np.where(dist < MAX_EXACT, dist, large)


def _bucket_tables():
    qi = np.arange(ATTN_BLOCK)[:, None]
    ki = np.arange(2 * ATTN_BLOCK)[None, :]
    rel = qi - ki + ATTN_BLOCK
    tables = []
    for window, dilation in DILATED_GROUPS:
        band = (rel >= 0) & (rel <= window // dilation)
        tables.append(np.where(band, _t5_bucket_np(rel * dilation), -1))
    return np.stack(tables).astype(np.int32)


def _bias_kernel(rb_ref, bucket_ref, o_ref, *, heads):
    g, h = pl.program_id(0), pl.program_id(1)
    bucket = bucket_ref[...]
    acc = jnp.full(bucket.shape, NEG_INF, F32)
    for b in range(NUM_BUCKETS):
        acc = jnp.where(bucket == b, rb_ref[b, g * heads + h], acc)
    o_ref[...] = acc


def _bias_tables(rel_bias, heads):
    ng = len(DILATED_GROUPS)
    buckets = jnp.asarray(_bucket_tables())
    return pl.pallas_call(
        functools.partial(_bias_kernel, heads=heads),
        grid=(ng, heads),
        in_specs=[
            pl.BlockSpec(memory_space=pltpu.SMEM),
            pl.BlockSpec((None, ATTN_BLOCK, 2 * ATTN_BLOCK), lambda g, h: (g, 0, 0)),
        ],
        out_specs=pl.BlockSpec((None, None, ATTN_BLOCK, 2 * ATTN_BLOCK), lambda g, h: (g, h, 0, 0)),
        out_shape=jax.ShapeDtypeStruct((ng, heads, ATTN_BLOCK, 2 * ATTN_BLOCK), F32),
        compiler_params=_params(("arbitrary", "arbitrary")),
        name="attn_bias",
    )(rel_bias, buckets)


def _attn_kernel(q_ref, kc_ref, kp_ref, vc_ref, vp_ref, bias_ref, o_ref, lse_ref, *, heads):
    has_prev = pl.program_id(1) > 0
    scale = HEAD_DIM ** -0.5
    lane = lax.broadcasted_iota(jnp.int32, (ATTN_BLOCK, ATTN_BLOCK), 1)
    lse_all = jnp.zeros((ATTN_BLOCK, ATTN_BLOCK), F32)
    ones = jnp.ones((ATTN_BLOCK, HEAD_DIM), BF16)
    for h in range(heads):
        cols = slice(h * HEAD_DIM, (h + 1) * HEAD_DIM)
        q = q_ref[:, cols]
        s_prev = _dot_nt(q, kp_ref[:, cols]) * scale + bias_ref[h, :, :ATTN_BLOCK]
        s_prev = jnp.where(has_prev, s_prev, NEG_INF)
        s_cur = _dot_nt(q, kc_ref[:, cols]) * scale + bias_ref[h, :, ATTN_BLOCK:]
        m = jnp.maximum(jnp.max(s_prev, axis=-1, keepdims=True), jnp.max(s_cur, axis=-1, keepdims=True))
        p_prev = jnp.exp(s_prev - m).astype(BF16)
        p_cur = jnp.exp(s_cur - m).astype(BF16)
        ov = (_dot(p_prev, jnp.concatenate([vp_ref[:, cols], ones], axis=1))
              + _dot(p_cur, jnp.concatenate([vc_ref[:, cols], ones], axis=1)))
        denom = ov[:, HEAD_DIM:]
        o_ref[:, cols] = (ov[:, :HEAD_DIM] / denom).astype(o_ref.dtype)
        lse_all = jnp.where(lane == h, m + jnp.log(denom[:, 0:1]), lse_all)
    lse_ref[...] = lse_all


def _attn_group(proj, bias_g, sub_len, heads):
    rows = proj.shape[0]
    width = heads * HEAD_DIM
    nb = sub_len // ATTN_BLOCK
    blk = (ATTN_BLOCK, width)

    def cur(which):
        return lambda s, i: (s * nb + i, which)

    def prev(which):
        return lambda s, i: (s * nb + jnp.maximum(i - 1, 0), which)

    return pl.pallas_call(
        functools.partial(_attn_kernel, heads=heads),
        grid=(rows // sub_len, nb),
        in_specs=[
            pl.BlockSpec(blk, cur(0)),
            pl.BlockSpec(blk, cur(1)),
            pl.BlockSpec(blk, prev(1)),
            pl.BlockSpec(blk, cur(2)),
            pl.BlockSpec(blk, prev(2)),
            pl.BlockSpec((heads, ATTN_BLOCK, 2 * ATTN_BLOCK), lambda s, i: (0, 0, 0)),
        ],
        out_specs=[
            pl.BlockSpec(blk, lambda s, i: (s * nb + i, 0)),
            pl.BlockSpec((ATTN_BLOCK, ATTN_BLOCK), lambda s, i: (s * nb + i, 0)),
        ],
        out_shape=[jax.ShapeDtypeStruct((rows, width), BF16), jax.ShapeDtypeStruct((rows, ATTN_BLOCK), F32)],
        compiler_params=_params(("parallel", "arbitrary")),
        name="attn_group",
    )(proj, proj, proj, proj, proj, bias_g)


def _merge_outproj_kernel(*refs, heads, dilations):
    ng = len(dilations)
    o_refs, l_refs = refs[:ng], refs[ng : 2 * ng]
    w_ref, y_ref, merged_ref, alpha_ref, acc_ref = refs[2 * ng :]
    bm = acc_ref.shape[1]

    @pl.when(pl.program_id(1) == 0)
    def _():
        for g, d in enumerate(dilations):
            for r in range(d):
                alpha_ref[g, _strided_rows(r, bm // d, d), :] = l_refs[g][r]
        lses = [alpha_ref[g] for g in range(ng)]
        mx = functools.reduce(jnp.maximum, lses)
        es = [jnp.exp(l - mx) for l in lses]
        inv = 1.0 / functools.reduce(jnp.add, es)
        for g in range(ng):
            alpha_ref[g] = es[g] * inv
        for g, d in enumerate(dilations):
            for r in range(d):
                rows = _strided_rows(r, bm // d, d)
                alpha = alpha_ref[g, rows, :]
                for h in range(heads):
                    cols = slice(h * HEAD_DIM, (h + 1) * HEAD_DIM)
                    term = alpha[:, h : h + 1] * o_refs[g][r, :, cols].astype(F32)
                    if g == 0:
                        acc_ref[h, rows, :] = term
                    else:
                        acc_ref[h, rows, :] += term
        for h in range(heads):
            merged_ref[:, h * HEAD_DIM : (h + 1) * HEAD_DIM] = acc_ref[h].astype(merged_ref.dtype)

    y_ref[...] = _dot(merged_ref[...], w_ref[...])


def _merge_outproj(outs, lses, w_out, heads, batch, seq, dilations, bm=512, bn=2048):
    width = heads * HEAD_DIM
    n = w_out.shape[1]
    bm, bn = _tile(seq, bm), _tile(n, bn)
    nt = seq // bm
    o_specs = [pl.BlockSpec((None, d, bm // d, width), lambda i, j: (i // nt, 0, i % nt, 0)) for d in dilations]
    l_specs = [pl.BlockSpec((None, d, bm // d, ATTN_BLOCK), lambda i, j: (i // nt, 0, i % nt, 0)) for d in dilations]
    outs = [o.reshape(batch, d, seq // d, width) for o, d in zip(outs, dilations)]
    lses = [l.reshape(batch, d, seq // d, ATTN_BLOCK) for l, d in zip(lses, dilations)]
    return pl.pallas_call(
        functools.partial(_merge_outproj_kernel, heads=heads, dilations=dilations),
        grid=(batch * nt, n // bn),
        in_specs=o_specs + l_specs + [pl.BlockSpec((width, bn), lambda i, j: (0, j))],
        out_specs=pl.BlockSpec((bm, bn), lambda i, j: (i, j)),
        out_shape=jax.ShapeDtypeStruct((batch * seq, n), F32),
        scratch_shapes=[
            pltpu.VMEM((bm, width), BF16),
            pltpu.VMEM((len(dilations), bm, ATTN_BLOCK), F32),
            pltpu.VMEM((heads, bm, HEAD_DIM), F32),
        ],
        compiler_params=_params(("parallel", "arbitrary")),
        name="attn_merge_outproj",
    )(*outs, *lses, w_out)


def _hgrn_query(acc):
    return _silu(acc) * (HEAD_DIM ** -0.5)


def _hgrn_forget(acc, lb_logits, *, layer):
    p = jnp.exp(lb_logits - jnp.max(lb_logits, axis=0, keepdims=True))
    p = p / jnp.sum(p, axis=0, keepdims=True)
    lb = jnp.sum(p[: layer + 1], axis=0, keepdims=True) - p[0:1]
    return lb + (1.0 - lb) * jax.nn.sigmoid(acc)


def _hgrn_rec_kernel(q_ref, f_ref, v_ref, g_ref, gain_ref, o_ref, state_ref, *, heads_per_step, chunks):
    @pl.when(pl.program_id(2) == 0)
    def _():
        state_ref[...] = jnp.zeros_like(state_ref)

    row = lax.broadcasted_iota(jnp.int32, (HG_CHUNK, HG_CHUNK), 0)
    col = lax.broadcasted_iota(jnp.int32, (HG_CHUNK, HG_CHUNK), 1)
    causal = row >= col
    tri = causal.astype(BF16)
    tri2 = jnp.concatenate([tri, tri], axis=1)
    mid = HG_CHUNK // 2
    gain = gain_ref[...]

    def chunk(c, carry):
        rows = pl.ds(pl.multiple_of(c * HG_CHUNK, HG_CHUNK), HG_CHUNK)
        log_f = jnp.log(f_ref[rows, :])
        hi = log_f.astype(BF16)
        lo = (log_f - hi.astype(F32)).astype(BF16)
        b_all = _dot(tri2, jnp.concatenate([hi, lo], axis=0))
        for h in range(heads_per_step):
            cols = slice(h * HEAD_DIM, (h + 1) * HEAD_DIM)
            b = b_all[:, cols]
            b_mid = b[mid - 1 : mid, :]
            b_last = b[HG_CHUNK - 1 :, :]
            key = 1.0 - f_ref[rows, cols]
            q = q_ref[rows, cols]
            v = v_ref[rows, cols]
            state = state_ref[h]
            scores = _dot_nt((q * jnp.exp(b - b_mid)).astype(BF16), (key * jnp.exp(b_mid - b)).astype(BF16))
            probs = jnp.where(causal, scores, 0.0).astype(BF16)
            lhs = jnp.concatenate([(q * jnp.exp(b)).astype(BF16), probs], axis=1)
            o = _dot(lhs, jnp.concatenate([state.astype(BF16), v], axis=0))
            decay = jnp.transpose(jnp.broadcast_to(jnp.exp(b_last), (8, HEAD_DIM)))[:, 0:1]
            state_ref[h] = state * decay + _dot_tn((key * jnp.exp(b_last - b)).astype(BF16), v)
            o_ref[rows, cols] = (_rms(o, gain) * g_ref[rows, cols]).astype(o_ref.dtype)
        return carry

    lax.fori_loop(0, chunks, chunk, 0, unroll=2)


def _hgrn_recurrence(q, f, v, g, out_gain, batch, seq, heads_per_step=4, ts=512):
    m, width = q.shape
    heads = width // HEAD_DIM
    heads_per_step = math.gcd(heads, heads_per_step)
    ts = _tile(seq, ts)
    bw = heads_per_step * HEAD_DIM
    nt = seq // ts
    spec = pl.BlockSpec((ts, bw), lambda b, hg, t: (b * nt + t, hg))
    return pl.pallas_call(
        functools.partial(_hgrn_rec_kernel, heads_per_step=heads_per_step, chunks=ts // HG_CHUNK),
        grid=(batch, heads // heads_per_step, nt),
        in_specs=[spec, spec, spec, spec, pl.BlockSpec((1, HEAD_DIM), lambda b, hg, t: (0, 0))],
        out_specs=spec,
        out_shape=jax.ShapeDtypeStruct((m, width), BF16),
        scratch_shapes=[pltpu.VMEM((heads_per_step, HEAD_DIM, HEAD_DIM), F32)],
        compiler_params=_params(("parallel", "parallel", "arbitrary")),
        name="hgrn_recurrence",
    )(q, f, v, g, out_gain.reshape(1, HEAD_DIM))


def kernel(x, norm_gains, rel_bias, attn_w_in, attn_w_out, hgrn_w_in, hgrn_lb_logits, hgrn_out_gain, hgrn_w_out,
           ffn_w_in, ffn_w_out):
    batch, seq, d = x.shape
    m = batch * seq
    depth = norm_gains.shape[0]
    attn_heads = attn_w_out.shape[1] // HEAD_DIM
    attn_width = attn_heads * HEAD_DIM
    dilations = tuple(dil for _, dil in DILATED_GROUPS)
    d_ff = ffn_w_out.shape[1]
    h = x.reshape(m, d)
    yn = None
    bias = None
    for i in range(depth):
        gains = norm_gains[i]
        if i % 2 == 0:
            if bias is None:
                bias = _bias_tables(rel_bias, attn_heads)
            yns = _prenorm_dilated(h, gains[0], batch, seq, dilations)
            outs, lses = [], []
            for g, dil in enumerate(dilations):
                proj = _proj(yns[g], attn_w_in, i // 2, [g * 3 * attn_width], 3 * attn_width, _identity, BF16,
                             name="attn_in")
                o, l = _attn_group(proj, bias[g], seq // dil, attn_heads)
                outs.append(o)
                lses.append(l)
            y = _merge_outproj(outs, lses, attn_w_out[i // 2].astype(BF16), attn_heads, batch, seq, dilations)
        else:
            if yn is None:
                yn = _prenorm(h, gains[0])
            width = hgrn_w_in.shape[2] // 4
            hproj = functools.partial(_proj, yn, hgrn_w_in, i // 2)
            q = hproj([0], width, _hgrn_query, F32, name="hgrn_in_q")
            f = hproj([width], width, functools.partial(_hgrn_forget, layer=i), F32, aux=hgrn_lb_logits,
                      name="hgrn_in_f")
            v = hproj([2 * width], width, _identity, BF16, name="hgrn_in_i")
            g = hproj([3 * width], width, _silu, F32, name="hgrn_in_g")
            o = _hgrn_recurrence(q, f, v, g, hgrn_out_gain[i // 2], batch, seq)
            y = _matmul(o, hgrn_w_out[i // 2].astype(BF16), F32)
        h, yn = _postnorm(h, y, gains[1], gains[2])
        hidden = _proj(yn, ffn_w_in, i, [0, d_ff], d_ff, _swiglu, BF16, bm=2048, bn=256, name="ffn_in")
        y = _matmul_ktiled(hidden, ffn_w_out[i].astype(BF16), F32)
        yn = None
        if i + 1 == depth:
            h = _postnorm(h, y, gains[3], None)
        elif (i + 1) % 2 == 0:
            h = _postnorm(h, y, gains[3], None)
        else:
            h, yn = _postnorm(h, y, gains[3], norm_gains[i + 1, 0])
    return h.reshape(batch, seq, d)
```

```python
import functools
import math

import numpy as np
import jax
import jax.numpy as jnp
from jax import lax
from jax.experimental import pallas as pl
from jax.experimental.pallas import tpu as pltpu

RMS_EPS = 1e-6
NEG_INF = -1e30

DILATED_GROUPS = ((128, 1), (512, 4), (2048, 16))
HEAD_DIM = 128
ATTN_BLOCK = 128
NUM_BUCKETS = 32
MAX_EXACT = 16
REL_MAX_DISTANCE = 2048
HG_CHUNK = 64

V7X_VMEM_BYTES = 64 * 1024 * 1024
VMEM_LIMIT = V7X_VMEM_BYTES - 8 * 1024 * 1024

BF16 = jnp.bfloat16
F32 = jnp.float32


def _params(semantics, vmem=VMEM_LIMIT):
    return pltpu.CompilerParams(dimension_semantics=semantics, vmem_limit_bytes=vmem)


def _tile(n, pref):
    if n <= pref:
        return n
    t = pref - pref % 128
    while t >= 128:
        if n % t == 0:
            return t
        t -= 128
    raise ValueError(f"no 128-multiple tile of {n} below {pref}")


def _dot(a, b):
    return jnp.dot(a, b, preferred_element_type=F32)


def _dot_nt(a, b):
    return lax.dot_general(a, b, (((1,), (1,)), ((), ())), preferred_element_type=F32)


def _dot_tn(a, b):
    return lax.dot_general(a, b, (((0,), (0,)), ((), ())), preferred_element_type=F32)


def _rms(x, gain):
    return x * lax.rsqrt(jnp.mean(x * x, axis=-1, keepdims=True) + RMS_EPS) * gain


def _silu(x):
    return x * jax.nn.sigmoid(x)


def _strided_rows(r, count, stride):
    return pl.ds(r, count) if stride == 1 else pl.ds(r, count, stride=stride)


def _prenorm_kernel(x_ref, g_ref, o_ref):
    o_ref[...] = _rms(x_ref[...], g_ref[...]).astype(o_ref.dtype)


def _prenorm(x, gain, tr=256):
    m, d = x.shape
    tr = _tile(m, tr)
    return pl.pallas_call(
        _prenorm_kernel,
        grid=(m // tr,),
        in_specs=[pl.BlockSpec((tr, d), lambda i: (i, 0)), pl.BlockSpec((1, d), lambda i: (0, 0))],
        out_specs=pl.BlockSpec((tr, d), lambda i: (i, 0)),
        out_shape=jax.ShapeDtypeStruct((m, d), BF16),
        compiler_params=_params(("parallel",)),
        name="prenorm",
    )(x, gain.reshape(1, d))


def _prenorm_dilated_kernel(x_ref, g_ref, *refs, dilations):
    out_refs, y_ref = refs[:-1], refs[-1]
    y = _rms(x_ref[...], g_ref[...])
    tr = y.shape[0]
    for c in range(y_ref.shape[0]):
        cols = slice(c * 128, (c + 1) * 128)
        y_ref[c] = y[:, cols]
        for o_ref, d in zip(out_refs, dilations):
            for r in range(d):
                o_ref[r, :, cols] = y_ref[c, _strided_rows(r, tr // d, d), :].astype(o_ref.dtype)


def _prenorm_dilated(x, gain, batch, seq, dilations, tr=256):
    m, d = x.shape
    tr = _tile(seq, tr)
    nt = seq // tr
    outs = pl.pallas_call(
        functools.partial(_prenorm_dilated_kernel, dilations=dilations),
        grid=(m // tr,),
        in_specs=[pl.BlockSpec((tr, d), lambda i: (i, 0)), pl.BlockSpec((1, d), lambda i: (0, 0))],
        out_specs=[pl.BlockSpec((None, dil, tr // dil, d), lambda i: (i // nt, 0, i % nt, 0)) for dil in dilations],
        out_shape=[jax.ShapeDtypeStruct((batch, dil, seq // dil, d), BF16) for dil in dilations],
        scratch_shapes=[pltpu.VMEM((d // 128, tr, 128), F32)],
        compiler_params=_params(("parallel",)),
        name="prenorm_dilated",
    )(x, gain.reshape(1, d))
    return [o.reshape(m, d) for o in outs]


def _postnorm_kernel(h_ref, y_ref, ga_ref, gb_ref, h_out_ref, yn_ref):
    h = h_ref[...] + _rms(y_ref[...], ga_ref[...])
    h_out_ref[...] = h
    yn_ref[...] = _rms(h, gb_ref[...]).astype(yn_ref.dtype)


def _postnorm_last_kernel(h_ref, y_ref, ga_ref, h_out_ref):
    h_out_ref[...] = h_ref[...] + _rms(y_ref[...], ga_ref[...])


def _postnorm(h, y, gain_post, gain_next, tr=256):
    m, d = h.shape
    tr = _tile(m, tr)
    row = pl.BlockSpec((tr, d), lambda i: (i, 0))
    vec = pl.BlockSpec((1, d), lambda i: (0, 0))
    if gain_next is None:
        return pl.pallas_call(
            _postnorm_last_kernel,
            grid=(m // tr,),
            in_specs=[row, row, vec],
            out_specs=row,
            out_shape=jax.ShapeDtypeStruct((m, d), F32),
            compiler_params=_params(("parallel",)),
            name="postnorm_last",
        )(h, y, gain_post.reshape(1, d))
    return pl.pallas_call(
        _postnorm_kernel,
        grid=(m // tr,),
        in_specs=[row, row, vec, vec],
        out_specs=[row, row],
        out_shape=[jax.ShapeDtypeStruct((m, d), F32), jax.ShapeDtypeStruct((m, d), BF16)],
        compiler_params=_params(("parallel",)),
        name="postnorm",
    )(h, y, gain_post.reshape(1, d), gain_next.reshape(1, d))


def _matmul_kernel(x_ref, w_ref, o_ref):
    o_ref[...] = _dot(x_ref[...], w_ref[...]).astype(o_ref.dtype)


def _matmul(x, w, out_dtype, bm=1024, bn=1024, name="matmul"):
    m, k = x.shape
    n = w.shape[1]
    bm, bn = _tile(m, bm), _tile(n, bn)
    return pl.pallas_call(
        _matmul_kernel,
        grid=(m // bm, n // bn),
        in_specs=[pl.BlockSpec((bm, k), lambda i, j: (i, 0)), pl.BlockSpec((k, bn), lambda i, j: (0, j))],
        out_specs=pl.BlockSpec((bm, bn), lambda i, j: (i, j)),
        out_shape=jax.ShapeDtypeStruct((m, n), out_dtype),
        compiler_params=_params(("parallel", "arbitrary")),
        name=name,
    )(x, w)


def _proj_kernel(*refs, n_w, has_aux, epilogue):
    x_ref, w_refs = refs[0], refs[1 : 1 + n_w]
    aux_ref = refs[1 + n_w] if has_aux else None
    o_ref = refs[1 + n_w + has_aux]
    wb_refs = refs[2 + n_w + has_aux :]
    jj, i = pl.program_id(0), pl.program_id(1)
    slab = w_refs[0].shape[0]

    @pl.when(jj < pl.num_programs(0) - 1)
    def _():
        rows = pl.ds(pl.multiple_of(i * slab, slab), slab)
        for w_ref, wb_ref in zip(w_refs, wb_refs):
            wb_ref[jj % 2, rows, :] = w_ref[...].astype(wb_ref.dtype)

    @pl.when(jj > 0)
    def _():
        x = x_ref[...]
        accs = [_dot(x, wb_ref[(jj - 1) % 2]) for wb_ref in wb_refs]
        out = epilogue(*accs, aux_ref[...]) if has_aux else epilogue(*accs)
        o_ref[...] = out.astype(o_ref.dtype)


def _proj(x, w, layer, col_offsets, width, epilogue, out_dtype, aux=None, bm=1024, bn=1024, name="proj"):
    m, k = x.shape
    bm, bn = _tile(m, bm), _tile(width, bn)
    n_w = len(col_offsets)
    nj, ni = width // bn, m // bm
    slab = k // ni
    assert k % ni == 0 and slab % 16 == 0, (k, ni)
    assert all(c % bn == 0 for c in col_offsets), (col_offsets, bn)

    def w_map(jj, i, c):
        return (layer, jnp.where(jj < nj, i, ni - 1), jnp.minimum(jj, nj - 1) + c)

    def row_map(jj, i):
        return jnp.where(jj > 0, i, 0)

    def col_map(jj):
        return jnp.maximum(jj - 1, 0)

    w_specs = [pl.BlockSpec((None, slab, bn), functools.partial(w_map, c=c // bn)) for c in col_offsets]
    aux_specs = [] if aux is None else [pl.BlockSpec((aux.shape[0], bn), lambda jj, i: (0, col_map(jj)))]
    return pl.pallas_call(
        functools.partial(_proj_kernel, n_w=n_w, has_aux=aux is not None, epilogue=epilogue),
        grid=(nj + 1, ni),
        in_specs=[pl.BlockSpec((bm, k), lambda jj, i: (row_map(jj, i), 0))] + w_specs + aux_specs,
        out_specs=pl.BlockSpec((bm, bn), lambda jj, i: (row_map(jj, i), col_map(jj))),
        out_shape=jax.ShapeDtypeStruct((m, width), out_dtype),
        scratch_shapes=[pltpu.VMEM((2, k, bn), BF16) for _ in range(n_w)],
        compiler_params=_params(("arbitrary", "arbitrary")),
        name=name,
    )(x, *([w] * n_w), *([] if aux is None else [aux]))


def _identity(acc):
    return acc


def _swiglu(gate, up):
    return _silu(gate) * up


def _t5_bucket_np(dist):
    dist = np.maximum(dist, 0)
    log_ratio = np.log(np.maximum(dist, 1).astype(np.float64) / MAX_EXACT) / math.log(REL_MAX_DISTANCE / MAX_EXACT)
    large = np.minimum(MAX_EXACT + (log_ratio * (NUM_BUCKETS - MAX_EXACT)).astype(np.int64), NUM_BUCKETS - 1)
    return np.where(dist < MAX_EXACT, dist, large)


def _bucket_tables():
    qi = np.arange(ATTN_BLOCK)[:, None]
    ki = np.arange(2 * ATTN_BLOCK)[None, :]
    rel = qi - ki + ATTN_BLOCK
    tables = []
    for window, dilation in DILATED_GROUPS:
        band = (rel >= 0) & (rel <= window // dilation)
        tables.append(np.where(band, _t5_bucket_np(rel * dilation), -1))
    return np.stack(tables).astype(np.int32)


def _bias_kernel(rb_ref, bucket_ref, o_ref, *, heads):
    g, h = pl.program_id(0), pl.program_id(1)
    bucket = bucket_ref[...]
    acc = jnp.full(bucket.shape, NEG_INF, F32)
    for b in range(NUM_BUCKETS):
        acc = jnp.where(bucket == b, rb_ref[b, g * heads + h], acc)
    o_ref[...] = acc


def _bias_tables(rel_bias, heads):
    ng = len(DILATED_GROUPS)
    buckets = jnp.asarray(_bucket_tables())
    return pl.pallas_call(
        functools.partial(_bias_kernel, heads=heads),
        grid=(ng, heads),
        in_specs=[
            pl.BlockSpec(memory_space=pltpu.SMEM),
            pl.BlockSpec((None, ATTN_BLOCK, 2 * ATTN_BLOCK), lambda g, h: (g, 0, 0)),
        ],
        out_specs=pl.BlockSpec((None, None, ATTN_BLOCK, 2 * ATTN_BLOCK), lambda g, h: (g, h, 0, 0)),
        out_shape=jax.ShapeDtypeStruct((ng, heads, ATTN_BLOCK, 2 * ATTN_BLOCK), F32),
        compiler_params=_params(("arbitrary", "arbitrary")),
        name="attn_bias",
    )(rel_bias, buckets)


def _attn_kernel(q_ref, kc_ref, kp_ref, vc_ref, vp_ref, bias_ref, o_ref, lse_ref, *, heads):
    has_prev = pl.program_id(1) > 0
    scale = HEAD_DIM ** -0.5
    lane = lax.broadcasted_iota(jnp.int32, (ATTN_BLOCK, ATTN_BLOCK), 1)
    lse_all = jnp.zeros((ATTN_BLOCK, ATTN_BLOCK), F32)
    ones = jnp.ones((ATTN_BLOCK, HEAD_DIM), BF16)
    for h in range(heads):
        cols = slice(h * HEAD_DIM, (h + 1) * HEAD_DIM)
        q = q_ref[:, cols]
        s_prev = _dot_nt(q, kp_ref[:, cols]) * scale + bias_ref[h, :, :ATTN_BLOCK]
        s_prev = jnp.where(has_prev, s_prev, NEG_INF)
        s_cur = _dot_nt(q, kc_ref[:, cols]) * scale + bias_ref[h, :, ATTN_BLOCK:]
        m = jnp.maximum(jnp.max(s_prev, axis=-1, keepdims=True), jnp.max(s_cur, axis=-1, keepdims=True))
        p_prev = jnp.exp(s_prev - m).astype(BF16)
        p_cur = jnp.exp(s_cur - m).astype(BF16)
        ov = (_dot(p_prev, jnp.concatenate([vp_ref[:, cols], ones], axis=1))
              + _dot(p_cur, jnp.concatenate([vc_ref[:, cols], ones], axis=1)))
        denom = ov[:, HEAD_DIM:]
        o_ref[:, cols] = (ov[:, :HEAD_DIM] / denom).astype(o_ref.dtype)
        lse_all = jnp.where(lane == h, m + jnp.log(denom[:, 0:1]), lse_all)
    lse_ref[...] = lse_all


def _attn_group(proj, bias_g, sub_len, heads):
    rows = proj.shape[0]
    width = heads * HEAD_DIM
    nb = sub_len // ATTN_BLOCK
    blk = (ATTN_BLOCK, width)

    def cur(which):
        return lambda s, i: (s * nb + i, which)

    def prev(which):
        return lambda s, i: (s * nb + jnp.maximum(i - 1, 0), which)

    return pl.pallas_call(
        functools.partial(_attn_kernel, heads=heads),
        grid=(rows // sub_len, nb),
        in_specs=[
            pl.BlockSpec(blk, cur(0)),
            pl.BlockSpec(blk, cur(1)),
            pl.BlockSpec(blk, prev(1)),
            pl.BlockSpec(blk, cur(2)),
            pl.BlockSpec(blk, prev(2)),
            pl.BlockSpec((heads, ATTN_BLOCK, 2 * ATTN_BLOCK), lambda s, i: (0, 0, 0)),
        ],
        out_specs=[
            pl.BlockSpec(blk, lambda s, i: (s * nb + i, 0)),
            pl.BlockSpec((ATTN_BLOCK, ATTN_BLOCK), lambda s, i: (s * nb + i, 0)),
        ],
        out_shape=[jax.ShapeDtypeStruct((rows, width), BF16), jax.ShapeDtypeStruct((rows, ATTN_BLOCK), F32)],
        compiler_params=_params(("parallel", "arbitrary")),
        name="attn_group",
    )(proj, proj, proj, proj, proj, bias_g)


def _merge_outproj_kernel(*refs, heads, dilations):
    ng = len(dilations)
    o_refs, l_refs = refs[:ng], refs[ng : 2 * ng]
    w_ref, h_ref, ga_ref, gb_ref, h_out_ref, yn_ref, merged_ref, alpha_ref, acc_ref = refs[2 * ng :]
    bm = acc_ref.shape[1]
    ii = pl.program_id(0)

    @pl.when(ii == 0)
    def _():
        merged_ref[1] = jnp.zeros(merged_ref.shape[1:], merged_ref.dtype)

    for g, d in enumerate(dilations):
        for r in range(d):
            alpha_ref[g, _strided_rows(r, bm // d, d), :] = l_refs[g][r]
    lses = [alpha_ref[g] for g in range(ng)]
    mx = functools.reduce(jnp.maximum, lses)
    es = [jnp.exp(l - mx) for l in lses]
    inv = 1.0 / functools.reduce(jnp.add, es)
    for g in range(ng):
        alpha_ref[g] = es[g] * inv
    for g, d in enumerate(dilations):
        for r in range(d):
            rows = _strided_rows(r, bm // d, d)
            alpha = alpha_ref[g, rows, :]
            for h in range(heads):
                cols = slice(h * HEAD_DIM, (h + 1) * HEAD_DIM)
                term = alpha[:, h : h + 1] * o_refs[g][r, :, cols].astype(F32)
                if g == 0:
                    acc_ref[h, rows, :] = term
                else:
                    acc_ref[h, rows, :] += term
    for h in range(heads):
        merged_ref[ii % 2, :, h * HEAD_DIM : (h + 1) * HEAD_DIM] = acc_ref[h].astype(merged_ref.dtype)

    y = _dot(merged_ref[(ii + 1) % 2], w_ref[...])
    h = h_ref[...] + _rms(y, ga_ref[...])
    h_out_ref[...] = h
    yn_ref[...] = _rms(h, gb_ref[...]).astype(yn_ref.dtype)


def _merge_outproj(outs, lses, w_out, h, gain_post, gain_next, heads, batch, seq, dilations, bm=256):
    width = heads * HEAD_DIM
    n = w_out.shape[1]
    bm = _tile(seq, bm)
    nt = seq // bm
    tiles = batch * nt

    def in_map(ii):
        t = jnp.minimum(ii, tiles - 1)
        return (t // nt, 0, t % nt, 0)

    row = pl.BlockSpec((bm, n), lambda ii: (jnp.maximum(ii - 1, 0), 0))
    vec = pl.BlockSpec((1, n), lambda ii: (0, 0))
    o_specs = [pl.BlockSpec((None, d, bm // d, width), in_map) for d in dilations]
    l_specs = [pl.BlockSpec((None, d, bm // d, ATTN_BLOCK), in_map) for d in dilations]
    outs = [o.reshape(batch, d, seq // d, width) for o, d in zip(outs, dilations)]
    lses = [l.reshape(batch, d, seq // d, ATTN_BLOCK) for l, d in zip(lses, dilations)]
    return pl.pallas_call(
        functools.partial(_merge_outproj_kernel, heads=heads, dilations=dilations),
        grid=(tiles + 1,),
        in_specs=o_specs + l_specs + [pl.BlockSpec((width, n), lambda ii: (0, 0), pipeline_mode=pl.Buffered(1)),
                                      row, vec, vec],
        out_specs=[row, row],
        out_shape=[jax.ShapeDtypeStruct((batch * seq, n), F32), jax.ShapeDtypeStruct((batch * seq, n), BF16)],
        scratch_shapes=[
            pltpu.VMEM((2, bm, width), BF16),
            pltpu.VMEM((len(dilations), bm, ATTN_BLOCK), F32),
            pltpu.VMEM((heads, bm, HEAD_DIM), F32),
        ],
        compiler_params=_params(("arbitrary",)),
        name="attn_merge_outproj",
    )(*outs, *lses, w_out, h, gain_post.reshape(1, n), gain_next.reshape(1, n))


def _hgrn_query(acc):
    return _silu(acc) * (HEAD_DIM ** -0.5)


def _hgrn_forget(acc, lb_logits, *, layer):
    p = jnp.exp(lb_logits - jnp.max(lb_logits, axis=0, keepdims=True))
    p = p / jnp.sum(p, axis=0, keepdims=True)
    lb = jnp.sum(p[: layer + 1], axis=0, keepdims=True) - p[0:1]
    return lb + (1.0 - lb) * jax.nn.sigmoid(acc)


def _hgrn_rec_kernel(q_ref, f_ref, v_ref, g_ref, gain_ref, o_ref, state_ref, *, heads_per_step, chunks):
    @pl.when(pl.program_id(2) == 0)
    def _():
        state_ref[...] = jnp.zeros_like(state_ref)

    row = lax.broadcasted_iota(jnp.int32, (HG_CHUNK, HG_CHUNK), 0)
    col = lax.broadcasted_iota(jnp.int32, (HG_CHUNK, HG_CHUNK), 1)
    causal = row >= col
    tri = causal.astype(BF16)
    tri2 = jnp.concatenate([tri, tri], axis=1)
    mid = HG_CHUNK // 2
    gain = gain_ref[...]

    def chunk(c, carry):
        rows = pl.ds(pl.multiple_of(c * HG_CHUNK, HG_CHUNK), HG_CHUNK)
        log_f = jnp.log(f_ref[rows, :])
        hi = log_f.astype(BF16)
        lo = (log_f - hi.astype(F32)).astype(BF16)
        b_all = _dot(tri2, jnp.concatenate([hi, lo], axis=0))
        for h in range(heads_per_step):
            cols = slice(h * HEAD_DIM, (h + 1) * HEAD_DIM)
            b = b_all[:, cols]
            b_mid = b[mid - 1 : mid, :]
            b_last = b[HG_CHUNK - 1 :, :]
            key = 1.0 - f_ref[rows, cols]
            q = q_ref[rows, cols]
            v = v_ref[rows, cols]
            state = state_ref[h]
            scores = _dot_nt((q * jnp.exp(b - b_mid)).astype(BF16), (key * jnp.exp(b_mid - b)).astype(BF16))
            probs = jnp.where(causal, scores, 0.0).astype(BF16)
            lhs = jnp.concatenate([(q * jnp.exp(b)).astype(BF16), probs], axis=1)
            o = _dot(lhs, jnp.concatenate([state.astype(BF16), v], axis=0))
            decay = jnp.transpose(jnp.broadcast_to(jnp.exp(b_last), (8, HEAD_DIM)))[:, 0:1]
            state_ref[h] = state * decay + _dot_tn((key * jnp.exp(b_last - b)).astype(BF16), v)
            o_ref[rows, cols] = (_rms(o, gain) * g_ref[rows, cols]).astype(o_ref.dtype)
        return carry

    lax.fori_loop(0, chunks, chunk, 0, unroll=2)


def _hgrn_recurrence(q, f, v, g, out_gain, batch, seq, heads_per_step=4, ts=512):
    m, width = q.shape
    heads = width // HEAD_DIM
    heads_per_step = math.gcd(heads, heads_per_step)
    ts = _tile(seq, ts)
    bw = heads_per_step * HEAD_DIM
    nt = seq // ts
    spec = pl.BlockSpec((ts, bw), lambda b, hg, t: (b * nt + t, hg))
    return pl.pallas_call(
        functools.partial(_hgrn_rec_kernel, heads_per_step=heads_per_step, chunks=ts // HG_CHUNK),
        grid=(batch, heads // heads_per_step, nt),
        in_specs=[spec, spec, spec, spec, pl.BlockSpec((1, HEAD_DIM), lambda b, hg, t: (0, 0))],
        out_specs=spec,
        out_shape=jax.ShapeDtypeStruct((m, width), BF16),
        scratch_shapes=[pltpu.VMEM((heads_per_step, HEAD_DIM, HEAD_DIM), F32)],
        compiler_params=_params(("parallel", "parallel", "arbitrary")),
        name="hgrn_recurrence",
    )(q, f, v, g, out_gain.reshape(1, HEAD_DIM))


def kernel(x, norm_gains, rel_bias, attn_w_in, attn_w_out, hgrn_w_in, hgrn_lb_logits, hgrn_out_gain, hgrn_w_out,
           ffn_w_in, ffn_w_out):
    batch, seq, d = x.shape
    m = batch * seq
    depth = norm_gains.shape[0]
    attn_heads = attn_w_out.shape[1] // HEAD_DIM
    attn_width = attn_heads * HEAD_DIM
    dilations = tuple(dil for _, dil in DILATED_GROUPS)
    d_ff = ffn_w_out.shape[1]
    h = x.reshape(m, d)
    yn = None
    bias = None
    for i in range(depth):
        gains = norm_gains[i]
        if i % 2 == 0:
            if bias is None:
                bias = _bias_tables(rel_bias, attn_heads)
            yns = _prenorm_dilated(h, gains[0], batch, seq, dilations)
            outs, lses = [], []
            for g, dil in enumerate(dilations):
                proj = _proj(yns[g], attn_w_in, i // 2, [g * 3 * attn_width], 3 * attn_width, _identity, BF16,
                             name="attn_in")
                o, l = _attn_group(proj, bias[g], seq // dil, attn_heads)
                outs.append(o)
                lses.append(l)
            h, yn = _merge_outproj(outs, lses, attn_w_out[i // 2].astype(BF16), h, gains[1], gains[2], attn_heads,
                                   batch, seq, dilations)
        else:
            if yn is None:
                yn = _prenorm(h, gains[0])
            width = hgrn_w_in.shape[2] // 4
            hproj = functools.partial(_proj, yn, hgrn_w_in, i // 2)
            q = hproj([0], width, _hgrn_query, F32, name="hgrn_in_q")
            f = hproj([width], width, functools.partial(_hgrn_forget, layer=i), F32, aux=hgrn_lb_logits,
                      name="hgrn_in_f")
            v = hproj([2 * width], width, _identity, BF16, name="hgrn_in_i")
            g = hproj([3 * width], width, _silu, F32, name="hgrn_in_g")
            o = _hgrn_recurrence(q, f, v, g, hgrn_out_gain[i // 2], batch, seq)
            y = _matmul(o, hgrn_w_out[i // 2].astype(BF16), F32, name="hgrn_out")
            h, yn = _postnorm(h, y, gains[1], gains[2])
        hidden = _proj(yn, ffn_w_in, i, [0, d_ff], d_ff, _swiglu, BF16, bm=2048, bn=256, name="ffn_in")
        y = _matmul(hidden, ffn_w_out[i].astype(BF16), F32, bm=512, bn=512, name="ffn_out")
        if i + 1 < depth and (i + 1) % 2 == 1:
            h, yn = _postnorm(h, y, gains[3], norm_gains[i + 1, 0])
        else:
            h, yn = _postnorm(h, y, gains[3], None), None
    return h.reshape(batch, seq, d)
```

```python
import functools
import math

import numpy as np
import jax
import jax.numpy as jnp
from jax import lax
from jax.experimental import pallas as pl
from jax.experimental.pallas import tpu as pltpu

RMS_EPS = 1e-6
NEG_INF = -1e30

DILATED_GROUPS = ((128, 1), (512, 4), (2048, 16))
HEAD_DIM = 128
ATTN_BLOCK = 128
NUM_BUCKETS = 32
MAX_EXACT = 16
REL_MAX_DISTANCE = 2048
HG_CHUNK = 64
PROJ_ROW_CHUNKS = 4

V7X_VMEM_BYTES = 64 * 1024 * 1024
VMEM_LIMIT = V7X_VMEM_BYTES - 8 * 1024 * 1024

BF16 = jnp.bfloat16
F32 = jnp.float32


def _params(semantics, vmem=VMEM_LIMIT):
    return pltpu.CompilerParams(dimension_semantics=semantics, vmem_limit_bytes=vmem)


def _tile(n, pref):
    if n <= pref:
        return n
    t = pref - pref % 128
    while t >= 128:
        if n % t == 0:
            return t
        t -= 128
    raise ValueError(f"no 128-multiple tile of {n} below {pref}")


def _dot(a, b):
    return jnp.dot(a, b, preferred_element_type=F32)


def _dot_nt(a, b):
    return lax.dot_general(a, b, (((1,), (1,)), ((), ())), preferred_element_type=F32)


def _dot_tn(a, b):
    return lax.dot_general(a, b, (((0,), (0,)), ((), ())), preferred_element_type=F32)


def _rms(x, gain):
    return x * lax.rsqrt(jnp.mean(x * x, axis=-1, keepdims=True) + RMS_EPS) * gain


def _silu(x):
    return x * jax.nn.sigmoid(x)


def _strided_rows(r, count, stride):
    return pl.ds(r, count) if stride == 1 else pl.ds(r, count, stride=stride)


def _prenorm_kernel(x_ref, g_ref, o_ref):
    o_ref[...] = _rms(x_ref[...], g_ref[...]).astype(o_ref.dtype)


def _prenorm(x, gain, tr=256):
    m, d = x.shape
    tr = _tile(m, tr)
    return pl.pallas_call(
        _prenorm_kernel,
        grid=(m // tr,),
        in_specs=[pl.BlockSpec((tr, d), lambda i: (i, 0)), pl.BlockSpec((1, d), lambda i: (0, 0))],
        out_specs=pl.BlockSpec((tr, d), lambda i: (i, 0)),
        out_shape=jax.ShapeDtypeStruct((m, d), BF16),
        compiler_params=_params(("parallel",)),
        name="prenorm",
    )(x, gain.reshape(1, d))


def _prenorm_dilated_kernel(x_ref, g_ref, *refs, dilations):
    out_refs, y_ref = refs[:-1], refs[-1]
    y = _rms(x_ref[...], g_ref[...])
    tr = y.shape[0]
    for c in range(y_ref.shape[0]):
        cols = slice(c * 128, (c + 1) * 128)
        y_ref[c] = y[:, cols]
        for o_ref, d in zip(out_refs, dilations):
            for r in range(d):
                o_ref[r, :, cols] = y_ref[c, _strided_rows(r, tr // d, d), :].astype(o_ref.dtype)


def _prenorm_dilated(x, gain, batch, seq, dilations, tr=256):
    m, d = x.shape
    tr = _tile(seq, tr)
    nt = seq // tr
    outs = pl.pallas_call(
        functools.partial(_prenorm_dilated_kernel, dilations=dilations),
        grid=(m // tr,),
        in_specs=[pl.BlockSpec((tr, d), lambda i: (i, 0)), pl.BlockSpec((1, d), lambda i: (0, 0))],
        out_specs=[pl.BlockSpec((None, dil, tr // dil, d), lambda i: (i // nt, 0, i % nt, 0)) for dil in dilations],
        out_shape=[jax.ShapeDtypeStruct((batch, dil, seq // dil, d), BF16) for dil in dilations],
        scratch_shapes=[pltpu.VMEM((d // 128, tr, 128), F32)],
        compiler_params=_params(("parallel",)),
        name="prenorm_dilated",
    )(x, gain.reshape(1, d))
    return [o.reshape(m, d) for o in outs]


def _postnorm_kernel(h_ref, y_ref, ga_ref, gb_ref, h_out_ref, yn_ref):
    h = h_ref[...] + _rms(y_ref[...], ga_ref[...])
    h_out_ref[...] = h
    yn_ref[...] = _rms(h, gb_ref[...]).astype(yn_ref.dtype)


def _postnorm_last_kernel(h_ref, y_ref, ga_ref, h_out_ref):
    h_out_ref[...] = h_ref[...] + _rms(y_ref[...], ga_ref[...])


def _postnorm(h, y, gain_post, gain_next, tr=256):
    m, d = h.shape
    tr = _tile(m, tr)
    row = pl.BlockSpec((tr, d), lambda i: (i, 0))
    vec = pl.BlockSpec((1, d), lambda i: (0, 0))
    if gain_next is None:
        return pl.pallas_call(
            _postnorm_last_kernel,
            grid=(m // tr,),
            in_specs=[row, row, vec],
            out_specs=row,
            out_shape=jax.ShapeDtypeStruct((m, d), F32),
            compiler_params=_params(("parallel",)),
            name="postnorm_last",
        )(h, y, gain_post.reshape(1, d))
    return pl.pallas_call(
        _postnorm_kernel,
        grid=(m // tr,),
        in_specs=[row, row, vec, vec],
        out_specs=[row, row],
        out_shape=[jax.ShapeDtypeStruct((m, d), F32), jax.ShapeDtypeStruct((m, d), BF16)],
        compiler_params=_params(("parallel",)),
        name="postnorm",
    )(h, y, gain_post.reshape(1, d), gain_next.reshape(1, d))


def _matmul_kernel(x_ref, w_ref, o_ref):
    o_ref[...] = _dot(x_ref[...], w_ref[...]).astype(o_ref.dtype)


def _matmul(x, w, out_dtype, bm=1024, bn=1024, name="matmul"):
    m, k = x.shape
    n = w.shape[1]
    bm, bn = _tile(m, bm), _tile(n, bn)
    return pl.pallas_call(
        _matmul_kernel,
        grid=(m // bm, n // bn),
        in_specs=[pl.BlockSpec((bm, k), lambda i, j: (i, 0)), pl.BlockSpec((k, bn), lambda i, j: (0, j))],
        out_specs=pl.BlockSpec((bm, bn), lambda i, j: (i, j)),
        out_shape=jax.ShapeDtypeStruct((m, n), out_dtype),
        compiler_params=_params(("parallel", "arbitrary")),
        name=name,
    )(x, w)


def _proj_kernel(*refs, n_w, has_aux, epilogue):
    x_ref, w_refs = refs[0], refs[1 : 1 + n_w]
    aux_ref = refs[1 + n_w] if has_aux else None
    o_ref = refs[1 + n_w + has_aux]
    wb_refs = refs[2 + n_w + has_aux :]
    jj, i = pl.program_id(0), pl.program_id(1)
    slab = w_refs[0].shape[0]

    @pl.when(jj < pl.num_programs(0) - 1)
    def _():
        rows = pl.ds(pl.multiple_of(i * slab, slab), slab)
        for w_ref, wb_ref in zip(w_refs, wb_refs):
            wb_ref[jj % 2, rows, :] = w_ref[...].astype(wb_ref.dtype)

    @pl.when(jj > 0)
    def _():
        chunk = x_ref.shape[0] // PROJ_ROW_CHUNKS
        for r in range(PROJ_ROW_CHUNKS):
            rows = slice(r * chunk, (r + 1) * chunk)
            x = x_ref[rows, :]
            accs = [_dot(x, wb_ref[(jj - 1) % 2]) for wb_ref in wb_refs]
            out = epilogue(*accs, aux_ref[...]) if has_aux else epilogue(*accs)
            o_ref[rows, :] = out.astype(o_ref.dtype)


def _proj(x, w, layer, col_offsets, width, epilogue, out_dtype, aux=None, bm=1024, bn=1024, name="proj"):
    m, k = x.shape
    bm, bn = _tile(m, bm), _tile(width, bn)
    n_w = len(col_offsets)
    nj, ni = width // bn, m // bm
    slab = k // ni
    assert k % ni == 0 and slab % 16 == 0, (k, ni)
    assert all(c % bn == 0 for c in col_offsets), (col_offsets, bn)

    def w_map(jj, i, c):
        return (layer, jnp.where(jj < nj, i, ni - 1), jnp.minimum(jj, nj - 1) + c)

    def row_map(jj, i):
        return jnp.where(jj > 0, i, 0)

    def col_map(jj):
        return jnp.maximum(jj - 1, 0)

    w_specs = [pl.BlockSpec((None, slab, bn), functools.partial(w_map, c=c // bn)) for c in col_offsets]
    aux_specs = [] if aux is None else [pl.BlockSpec((aux.shape[0], bn), lambda jj, i: (0, col_map(jj)))]
    return pl.pallas_call(
        functools.partial(_proj_kernel, n_w=n_w, has_aux=aux is not None, epilogue=epilogue),
        grid=(nj + 1, ni),
        in_specs=[pl.BlockSpec((bm, k), lambda jj, i: (row_map(jj, i), 0))] + w_specs + aux_specs,
        out_specs=pl.BlockSpec((bm, bn), lambda jj, i: (row_map(jj, i), col_map(jj))),
        out_shape=jax.ShapeDtypeStruct((m, width), out_dtype),
        scratch_shapes=[pltpu.VMEM((2, k, bn), BF16) for _ in range(n_w)],
        compiler_params=_params(("arbitrary", "arbitrary")),
        name=name,
    )(x, *([w] * n_w), *([] if aux is None else [aux]))


def _identity(acc):
    return acc


def _swiglu(gate, up):
    return _silu(gate) * up


def _t5_bucket_np(dist):
    dist = np.maximum(dist, 0)
    log_ratio = np.log(np.maximum(dist, 1).astype(np.float64) / MAX_EXACT) / math.log(REL_MAX_DISTANCE / MAX_EXACT)
    large = np.minimum(MAX_EXACT + (log_ratio * (NUM_BUCKETS - MAX_EXACT)).astype(np.int64), NUM_BUCKETS - 1)
    return np.where(dist < MAX_EXACT, dist, large)


def _bucket_tables():
    qi = np.arange(ATTN_BLOCK)[:, None]
    ki = np.arange(2 * ATTN_BLOCK)[None, :]
    rel = qi - ki + ATTN_BLOCK
    tables = []
    for window, dilation in DILATED_GROUPS:
        band = (rel >= 0) & (rel <= window // dilation)
        tables.append(np.where(band, _t5_bucket_np(rel * dilation), -1))
    return np.stack(tables).astype(np.int32)


def _bias_kernel(rb_ref, bucket_ref, o_ref, *, heads):
    g, h = pl.program_id(0), pl.program_id(1)
    bucket = bucket_ref[...]
    acc = jnp.full(bucket.shape, NEG_INF, F32)
    for b in range(NUM_BUCKETS):
        acc = jnp.where(bucket == b, rb_ref[b, g * heads + h], acc)
    o_ref[...] = acc


def _bias_tables(rel_bias, heads):
    ng = len(DILATED_GROUPS)
    buckets = jnp.asarray(_bucket_tables())
    return pl.pallas_call(
        functools.partial(_bias_kernel, heads=heads),
        grid=(ng, heads),
        in_specs=[
            pl.BlockSpec(memory_space=pltpu.SMEM),
            pl.BlockSpec((None, ATTN_BLOCK, 2 * ATTN_BLOCK), lambda g, h: (g, 0, 0)),
        ],
        out_specs=pl.BlockSpec((None, None, ATTN_BLOCK, 2 * ATTN_BLOCK), lambda g, h: (g, h, 0, 0)),
        out_shape=jax.ShapeDtypeStruct((ng, heads, ATTN_BLOCK, 2 * ATTN_BLOCK), F32),
        compiler_params=_params(("arbitrary", "arbitrary")),
        name="attn_bias",
    )(rel_bias, buckets)


def _attn_kernel(q_ref, kc_ref, kp_ref, vc_ref, vp_ref, bias_ref, o_ref, lse_ref, *, heads):
    has_prev = pl.program_id(1) > 0
    scale = HEAD_DIM ** -0.5
    lane = lax.broadcasted_iota(jnp.int32, (ATTN_BLOCK, ATTN_BLOCK), 1)
    lse_all = jnp.zeros((ATTN_BLOCK, ATTN_BLOCK), F32)
    ones = jnp.ones((ATTN_BLOCK, HEAD_DIM), BF16)
    for h in range(heads):
        cols = slice(h * HEAD_DIM, (h + 1) * HEAD_DIM)
        q = q_ref[:, cols]
        s_prev = _dot_nt(q, kp_ref[:, cols]) * scale + bias_ref[h, :, :ATTN_BLOCK]
        s_prev = jnp.where(has_prev, s_prev, NEG_INF)
        s_cur = _dot_nt(q, kc_ref[:, cols]) * scale + bias_ref[h, :, ATTN_BLOCK:]
        m = jnp.maximum(jnp.max(s_prev, axis=-1, keepdims=True), jnp.max(s_cur, axis=-1, keepdims=True))
        p_prev = jnp.exp(s_prev - m).astype(BF16)
        p_cur = jnp.exp(s_cur - m).astype(BF16)
        ov = (_dot(p_prev, jnp.concatenate([vp_ref[:, cols], ones], axis=1))
              + _dot(p_cur, jnp.concatenate([vc_ref[:, cols], ones], axis=1)))
        denom = ov[:, HEAD_DIM:]
        o_ref[:, cols] = (ov[:, :HEAD_DIM] / denom).astype(o_ref.dtype)
        lse_all = jnp.where(lane == h, m + jnp.log(denom[:, 0:1]), lse_all)
    lse_ref[...] = lse_all


def _attn_group(proj, bias_g, sub_len, heads):
    rows = proj.shape[0]
    width = heads * HEAD_DIM
    nb = sub_len // ATTN_BLOCK
    blk = (ATTN_BLOCK, width)

    def cur(which):
        return lambda s, i: (s * nb + i, which)

    def prev(which):
        return lambda s, i: (s * nb + jnp.maximum(i - 1, 0), which)

    return pl.pallas_call(
        functools.partial(_attn_kernel, heads=heads),
        grid=(rows // sub_len, nb),
        in_specs=[
            pl.BlockSpec(blk, cur(0)),
            pl.BlockSpec(blk, cur(1)),
            pl.BlockSpec(blk, prev(1)),
            pl.BlockSpec(blk, cur(2)),
            pl.BlockSpec(blk, prev(2)),
            pl.BlockSpec((heads, ATTN_BLOCK, 2 * ATTN_BLOCK), lambda s, i: (0, 0, 0)),
        ],
        out_specs=[
            pl.BlockSpec(blk, lambda s, i: (s * nb + i, 0)),
            pl.BlockSpec((ATTN_BLOCK, ATTN_BLOCK), lambda s, i: (s * nb + i, 0)),
        ],
        out_shape=[jax.ShapeDtypeStruct((rows, width), BF16), jax.ShapeDtypeStruct((rows, ATTN_BLOCK), F32)],
        compiler_params=_params(("parallel", "arbitrary")),
        name="attn_group",
    )(proj, proj, proj, proj, proj, bias_g)


def _merge_outproj_kernel(*refs, heads, dilations):
    ng = len(dilations)
    o_refs, l_refs = refs[:ng], refs[ng : 2 * ng]
    w_ref, h_ref, ga_ref, gb_ref, h_out_ref, yn_ref, merged_ref, alpha_ref, acc_ref = refs[2 * ng :]
    bm = acc_ref.shape[1]
    ii = pl.program_id(0)

    @pl.when(ii == 0)
    def _():
        merged_ref[1] = jnp.zeros(merged_ref.shape[1:], merged_ref.dtype)

    for g, d in enumerate(dilations):
        for r in range(d):
            alpha_ref[g, _strided_rows(r, bm // d, d), :] = l_refs[g][r]
    lses = [alpha_ref[g] for g in range(ng)]
    mx = functools.reduce(jnp.maximum, lses)
    es = [jnp.exp(l - mx) for l in lses]
    inv = 1.0 / functools.reduce(jnp.add, es)
    for g in range(ng):
        alpha_ref[g] = es[g] * inv
    for g, d in enumerate(dilations):
        for r in range(d):
            rows = _strided_rows(r, bm // d, d)
            alpha = alpha_ref[g, rows, :]
            for h in range(heads):
                cols = slice(h * HEAD_DIM, (h + 1) * HEAD_DIM)
                term = alpha[:, h : h + 1] * o_refs[g][r, :, cols].astype(F32)
                if g == 0:
                    acc_ref[h, rows, :] = term
                else:
                    acc_ref[h, rows, :] += term
    for h in range(heads):
        merged_ref[ii % 2, :, h * HEAD_DIM : (h + 1) * HEAD_DIM] = acc_ref[h].astype(merged_ref.dtype)

    y = _dot(merged_ref[(ii + 1) % 2], w_ref[...])
    h = h_ref[...] + _rms(y, ga_ref[...])
    h_out_ref[...] = h
    yn_ref[...] = _rms(h, gb_ref[...]).astype(yn_ref.dtype)


def _merge_outproj(outs, lses, w_out, h, gain_post, gain_next, heads, batch, seq, dilations, bm=256):
    width = heads * HEAD_DIM
    n = w_out.shape[1]
    bm = _tile(seq, bm)
    nt = seq // bm
    tiles = batch * nt

    def in_map(ii):
        t = jnp.minimum(ii, tiles - 1)
        return (t // nt, 0, t % nt, 0)

    row = pl.BlockSpec((bm, n), lambda ii: (jnp.maximum(ii - 1, 0), 0))
    vec = pl.BlockSpec((1, n), lambda ii: (0, 0))
    o_specs = [pl.BlockSpec((None, d, bm // d, width), in_map) for d in dilations]
    l_specs = [pl.BlockSpec((None, d, bm // d, ATTN_BLOCK), in_map) for d in dilations]
    outs = [o.reshape(batch, d, seq // d, width) for o, d in zip(outs, dilations)]
    lses = [l.reshape(batch, d, seq // d, ATTN_BLOCK) for l, d in zip(lses, dilations)]
    return pl.pallas_call(
        functools.partial(_merge_outproj_kernel, heads=heads, dilations=dilations),
        grid=(tiles + 1,),
        in_specs=o_specs + l_specs + [pl.BlockSpec((width, n), lambda ii: (0, 0), pipeline_mode=pl.Buffered(1)),
                                      row, vec, vec],
        out_specs=[row, row],
        out_shape=[jax.ShapeDtypeStruct((batch * seq, n), F32), jax.ShapeDtypeStruct((batch * seq, n), BF16)],
        scratch_shapes=[
            pltpu.VMEM((2, bm, width), BF16),
            pltpu.VMEM((len(dilations), bm, ATTN_BLOCK), F32),
            pltpu.VMEM((heads, bm, HEAD_DIM), F32),
        ],
        compiler_params=_params(("arbitrary",)),
        name="attn_merge_outproj",
    )(*outs, *lses, w_out, h, gain_post.reshape(1, n), gain_next.reshape(1, n))


def _hgrn_query(acc):
    return _silu(acc) * (HEAD_DIM ** -0.5)


def _hgrn_forget(acc, lb_logits, *, layer):
    p = jnp.exp(lb_logits - jnp.max(lb_logits, axis=0, keepdims=True))
    p = p / jnp.sum(p, axis=0, keepdims=True)
    lb = jnp.sum(p[: layer + 1], axis=0, keepdims=True) - p[0:1]
    return lb + (1.0 - lb) * jax.nn.sigmoid(acc)


def _hgrn_rec_kernel(q_ref, f_ref, v_ref, g_ref, gain_ref, o_ref, state_ref, b_ref, *, heads_per_step, chunks):
    @pl.when(pl.program_id(2) == 0)
    def _():
        state_ref[...] = jnp.zeros_like(state_ref)

    row = lax.broadcasted_iota(jnp.int32, (HG_CHUNK, HG_CHUNK), 0)
    col = lax.broadcasted_iota(jnp.int32, (HG_CHUNK, HG_CHUNK), 1)
    causal = row >= col
    tri = causal.astype(BF16)
    tri2 = jnp.concatenate([tri, tri], axis=1)
    mid = HG_CHUNK // 2
    gain = gain_ref[...]
    heads = range(heads_per_step)
    cols = [slice(h * HEAD_DIM, (h + 1) * HEAD_DIM) for h in heads]

    for c in range(chunks):
        rows = pl.ds(c * HG_CHUNK, HG_CHUNK)
        log_f = jnp.log(f_ref[rows, :])
        hi = log_f.astype(BF16)
        lo = (log_f - hi.astype(F32)).astype(BF16)
        b_ref[rows, :] = _dot(tri2, jnp.concatenate([hi, lo], axis=0))

    def chunk(c, carry):
        rows = pl.ds(pl.multiple_of(c * HG_CHUNK, HG_CHUNK), HG_CHUNK)
        b = [b_ref[rows, cols[h]] for h in heads]
        b_mid = [b[h][mid - 1 : mid, :] for h in heads]
        b_last = [b[h][HG_CHUNK - 1 :, :] for h in heads]
        key = [1.0 - f_ref[rows, cols[h]] for h in heads]
        q = [q_ref[rows, cols[h]] for h in heads]
        v = [v_ref[rows, cols[h]] for h in heads]
        scores = [_dot_nt((q[h] * jnp.exp(b[h] - b_mid[h])).astype(BF16),
                          (key[h] * jnp.exp(b_mid[h] - b[h])).astype(BF16)) for h in heads]
        probs = [jnp.where(causal, scores[h], 0.0).astype(BF16) for h in heads]
        state = [state_ref[h] for h in heads]
        o = [_dot(jnp.concatenate([(q[h] * jnp.exp(b[h])).astype(BF16), probs[h]], axis=1),
                  jnp.concatenate([state[h].astype(BF16), v[h]], axis=0)) for h in heads]
        update = [_dot_tn((key[h] * jnp.exp(b_last[h] - b[h])).astype(BF16), v[h]) for h in heads]
        for h in heads:
            decay = jnp.transpose(jnp.broadcast_to(jnp.exp(b_last[h]), (8, HEAD_DIM)))[:, 0:1]
            state_ref[h] = state[h] * decay + update[h]
        for h in heads:
            o_ref[rows, cols[h]] = (_rms(o[h], gain) * g_ref[rows, cols[h]]).astype(o_ref.dtype)
        return carry

    lax.fori_loop(0, chunks, chunk, 0)


def _hgrn_recurrence(q, f, v, g, out_gain, batch, seq, heads_per_step=8, ts=512):
    m, width = q.shape
    heads = width // HEAD_DIM
    heads_per_step = math.gcd(heads, heads_per_step)
    ts = _tile(seq, ts)
    bw = heads_per_step * HEAD_DIM
    nt = seq // ts
    spec = pl.BlockSpec((ts, bw), lambda b, hg, t: (b * nt + t, hg))
    return pl.pallas_call(
        functools.partial(_hgrn_rec_kernel, heads_per_step=heads_per_step, chunks=ts // HG_CHUNK),
        grid=(batch, heads // heads_per_step, nt),
        in_specs=[spec, spec, spec, spec, pl.BlockSpec((1, HEAD_DIM), lambda b, hg, t: (0, 0))],
        out_specs=spec,
        out_shape=jax.ShapeDtypeStruct((m, width), BF16),
        scratch_shapes=[pltpu.VMEM((heads_per_step, HEAD_DIM, HEAD_DIM), F32), pltpu.VMEM((ts, bw), F32)],
        compiler_params=_params(("parallel", "parallel", "arbitrary")),
        name="hgrn_recurrence",
    )(q, f, v, g, out_gain.reshape(1, HEAD_DIM))


def kernel(x, norm_gains, rel_bias, attn_w_in, attn_w_out, hgrn_w_in, hgrn_lb_logits, hgrn_out_gain, hgrn_w_out,
           ffn_w_in, ffn_w_out):
    batch, seq, d = x.shape
    m = batch * seq
    depth = norm_gains.shape[0]
    attn_heads = attn_w_out.shape[1] // HEAD_DIM
    attn_width = attn_heads * HEAD_DIM
    dilations = tuple(dil for _, dil in DILATED_GROUPS)
    d_ff = ffn_w_out.shape[1]
    h = x.reshape(m, d)
    yn = None
    bias = None
    for i in range(depth):
        gains = norm_gains[i]
        if i % 2 == 0:
            if bias is None:
                bias = _bias_tables(rel_bias, attn_heads)
            yns = _prenorm_dilated(h, gains[0], batch, seq, dilations)
            outs, lses = [], []
            for g, dil in enumerate(dilations):
                proj = _proj(yns[g], attn_w_in, i // 2, [g * 3 * attn_width], 3 * attn_width, _identity, BF16,
                             name="attn_in")
                o, l = _attn_group(proj, bias[g], seq // dil, attn_heads)
                outs.append(o)
                lses.append(l)
            h, yn = _merge_outproj(outs, lses, attn_w_out[i // 2].astype(BF16), h, gains[1], gains[2], attn_heads,
                                   batch, seq, dilations)
        else:
            if yn is None:
                yn = _prenorm(h, gains[0])
            width = hgrn_w_in.shape[2] // 4
            hproj = functools.partial(_proj, yn, hgrn_w_in, i // 2)
            q = hproj([0], width, _hgrn_query, F32, name="hgrn_in_q")
            f = hproj([width], width, functools.partial(_hgrn_forget, layer=i), F32, aux=hgrn_lb_logits,
                      name="hgrn_in_f")
            v = hproj([2 * width], width, _identity, BF16, name="hgrn_in_i")
            g = hproj([3 * width], width, _silu, F32, name="hgrn_in_g")
            o = _hgrn_recurrence(q, f, v, g, hgrn_out_gain[i // 2], batch, seq)
            y = _matmul(o, hgrn_w_out[i // 2].astype(BF16), F32, name="hgrn_out")
            h, yn = _postnorm(h, y, gains[1], gains[2])
        hidden = _proj(yn, ffn_w_in, i, [0, d_ff], d_ff, _swiglu, BF16, bm=2048, bn=256, name="ffn_in")
        y = _matmul(hidden, ffn_w_out[i].astype(BF16), F32, bm=512, bn=512, name="ffn_out")
        if i + 1 < depth and (i + 1) % 2 == 1:
            h, yn = _postnorm(h, y, gains[3], norm_gains[i + 1, 0])
        else:
            h, yn = _postnorm(h, y, gains[3], None), None
    return h.reshape(batch, seq, d)
```

```python
import functools
import math

import numpy as np
import jax
import jax.numpy as jnp
from jax import lax
from jax.experimental import pallas as pl
from jax.experimental.pallas import tpu as pltpu

RMS_EPS = 1e-6
NEG_INF = -1e30

DILATED_GROUPS = ((128, 1), (512, 4), (2048, 16))
HEAD_DIM = 128
ATTN_BLOCK = 128
NUM_BUCKETS = 32
MAX_EXACT = 16
REL_MAX_DISTANCE = 2048
HG_CHUNK = 64
PROJ_ROW_CHUNKS = 4

V7X_VMEM_BYTES = 64 * 1024 * 1024
VMEM_LIMIT = V7X_VMEM_BYTES - 8 * 1024 * 1024

BF16 = jnp.bfloat16
F32 = jnp.float32


def _params(semantics, vmem=VMEM_LIMIT):
    return pltpu.CompilerParams(dimension_semantics=semantics, vmem_limit_bytes=vmem)


def _tile(n, pref):
    if n <= pref:
        return n
    t = pref - pref % 128
    while t >= 128:
        if n % t == 0:
            return t
        t -= 128
    raise ValueError(f"no 128-multiple tile of {n} below {pref}")


def _dot(a, b):
    return jnp.dot(a, b, preferred_element_type=F32)


def _dot_nt(a, b):
    return lax.dot_general(a, b, (((1,), (1,)), ((), ())), preferred_element_type=F32)


def _dot_tn(a, b):
    return lax.dot_general(a, b, (((0,), (0,)), ((), ())), preferred_element_type=F32)


def _rms(x, gain):
    return x * lax.rsqrt(jnp.mean(x * x, axis=-1, keepdims=True) + RMS_EPS) * gain


def _silu(x):
    return x * jax.nn.sigmoid(x)


def _strided_rows(r, count, stride):
    return pl.ds(r, count) if stride == 1 else pl.ds(r, count, stride=stride)


def _prenorm_kernel(x_ref, g_ref, o_ref):
    o_ref[...] = _rms(x_ref[...], g_ref[...]).astype(o_ref.dtype)


def _prenorm(x, gain, tr=256):
    m, d = x.shape
    tr = _tile(m, tr)
    return pl.pallas_call(
        _prenorm_kernel,
        grid=(m // tr,),
        in_specs=[pl.BlockSpec((tr, d), lambda i: (i, 0)), pl.BlockSpec((1, d), lambda i: (0, 0))],
        out_specs=pl.BlockSpec((tr, d), lambda i: (i, 0)),
        out_shape=jax.ShapeDtypeStruct((m, d), BF16),
        compiler_params=_params(("parallel",)),
        name="prenorm",
    )(x, gain.reshape(1, d))


def _prenorm_dilated_kernel(x_ref, g_ref, *refs, dilations):
    out_refs, y_ref = refs[:-1], refs[-1]
    y = _rms(x_ref[...], g_ref[...])
    tr = y.shape[0]
    for c in range(y_ref.shape[0]):
        cols = slice(c * 128, (c + 1) * 128)
        y_ref[c] = y[:, cols]
        for o_ref, d in zip(out_refs, dilations):
            for r in range(d):
                o_ref[r, :, cols] = y_ref[c, _strided_rows(r, tr // d, d), :].astype(o_ref.dtype)


def _prenorm_dilated(x, gain, batch, seq, dilations, tr=256):
    m, d = x.shape
    tr = _tile(seq, tr)
    nt = seq // tr
    outs = pl.pallas_call(
        functools.partial(_prenorm_dilated_kernel, dilations=dilations),
        grid=(m // tr,),
        in_specs=[pl.BlockSpec((tr, d), lambda i: (i, 0)), pl.BlockSpec((1, d), lambda i: (0, 0))],
        out_specs=[pl.BlockSpec((None, dil, tr // dil, d), lambda i: (i // nt, 0, i % nt, 0)) for dil in dilations],
        out_shape=[jax.ShapeDtypeStruct((batch, dil, seq // dil, d), BF16) for dil in dilations],
        scratch_shapes=[pltpu.VMEM((d // 128, tr, 128), F32)],
        compiler_params=_params(("parallel",)),
        name="prenorm_dilated",
    )(x, gain.reshape(1, d))
    return [o.reshape(m, d) for o in outs]


def _postnorm_kernel(h_ref, y_ref, ga_ref, gb_ref, h_out_ref, yn_ref):
    h = h_ref[...] + _rms(y_ref[...], ga_ref[...])
    h_out_ref[...] = h
    yn_ref[...] = _rms(h, gb_ref[...]).astype(yn_ref.dtype)


def _postnorm_last_kernel(h_ref, y_ref, ga_ref, h_out_ref):
    h_out_ref[...] = h_ref[...] + _rms(y_ref[...], ga_ref[...])


def _postnorm(h, y, gain_post, gain_next, tr=256):
    m, d = h.shape
    tr = _tile(m, tr)
    row = pl.BlockSpec((tr, d), lambda i: (i, 0))
    vec = pl.BlockSpec((1, d), lambda i: (0, 0))
    if gain_next is None:
        return pl.pallas_call(
            _postnorm_last_kernel,
            grid=(m // tr,),
            in_specs=[row, row, vec],
            out_specs=row,
            out_shape=jax.ShapeDtypeStruct((m, d), F32),
            compiler_params=_params(("parallel",)),
            name="postnorm_last",
        )(h, y, gain_post.reshape(1, d))
    return pl.pallas_call(
        _postnorm_kernel,
        grid=(m // tr,),
        in_specs=[row, row, vec, vec],
        out_specs=[row, row],
        out_shape=[jax.ShapeDtypeStruct((m, d), F32), jax.ShapeDtypeStruct((m, d), BF16)],
        compiler_params=_params(("parallel",)),
        name="postnorm",
    )(h, y, gain_post.reshape(1, d), gain_next.reshape(1, d))


def _matmul_kernel(x_ref, w_ref, o_ref):
    o_ref[...] = _dot(x_ref[...], w_ref[...]).astype(o_ref.dtype)


def _matmul(x, w, layer, out_dtype, bm=1024, bn=1024, name="matmul"):
    m, k = x.shape
    n = w.shape[2]
    bm, bn = _tile(m, bm), _tile(n, bn)
    return pl.pallas_call(
        _matmul_kernel,
        grid=(m // bm, n // bn),
        in_specs=[pl.BlockSpec((bm, k), lambda i, j: (i, 0)), pl.BlockSpec((None, k, bn), lambda i, j: (layer, 0, j))],
        out_specs=pl.BlockSpec((bm, bn), lambda i, j: (i, j)),
        out_shape=jax.ShapeDtypeStruct((m, n), out_dtype),
        compiler_params=_params(("parallel", "arbitrary")),
        name=name,
    )(x, w)


def _proj_kernel(*refs, n_w, has_aux, epilogue):
    x_ref, w_refs = refs[0], refs[1 : 1 + n_w]
    aux_ref = refs[1 + n_w] if has_aux else None
    o_ref = refs[1 + n_w + has_aux]
    wb_refs = refs[2 + n_w + has_aux :]
    jj, i = pl.program_id(0), pl.program_id(1)
    slab = w_refs[0].shape[0]

    @pl.when(jj < pl.num_programs(0) - 1)
    def _():
        rows = pl.ds(pl.multiple_of(i * slab, slab), slab)
        for w_ref, wb_ref in zip(w_refs, wb_refs):
            wb_ref[jj % 2, rows, :] = w_ref[...].astype(wb_ref.dtype)

    @pl.when(jj > 0)
    def _():
        chunk = x_ref.shape[0] // PROJ_ROW_CHUNKS
        for r in range(PROJ_ROW_CHUNKS):
            rows = slice(r * chunk, (r + 1) * chunk)
            x = x_ref[rows, :]
            accs = [_dot(x, wb_ref[(jj - 1) % 2]) for wb_ref in wb_refs]
            out = epilogue(*accs, aux_ref[...]) if has_aux else epilogue(*accs)
            o_ref[rows, :] = out.astype(o_ref.dtype)


def _proj(x, w, layer, col_offsets, width, epilogue, out_dtype, aux=None, bm=1024, bn=1024, name="proj"):
    m, k = x.shape
    bm, bn = _tile(m, bm), _tile(width, bn)
    n_w = len(col_offsets)
    nj, ni = width // bn, m // bm
    slab = k // ni
    assert k % ni == 0 and slab % 16 == 0, (k, ni)
    assert all(c % bn == 0 for c in col_offsets), (col_offsets, bn)

    def w_map(jj, i, c):
        return (layer, jnp.where(jj < nj, i, ni - 1), jnp.minimum(jj, nj - 1) + c)

    def row_map(jj, i):
        return jnp.where(jj > 0, i, 0)

    def col_map(jj):
        return jnp.maximum(jj - 1, 0)

    w_specs = [pl.BlockSpec((None, slab, bn), functools.partial(w_map, c=c // bn)) for c in col_offsets]
    aux_specs = [] if aux is None else [pl.BlockSpec((aux.shape[0], bn), lambda jj, i: (0, col_map(jj)))]
    return pl.pallas_call(
        functools.partial(_proj_kernel, n_w=n_w, has_aux=aux is not None, epilogue=epilogue),
        grid=(nj + 1, ni),
        in_specs=[pl.BlockSpec((bm, k), lambda jj, i: (row_map(jj, i), 0))] + w_specs + aux_specs,
        out_specs=pl.BlockSpec((bm, bn), lambda jj, i: (row_map(jj, i), col_map(jj))),
        out_shape=jax.ShapeDtypeStruct((m, width), out_dtype),
        scratch_shapes=[pltpu.VMEM((2, k, bn), BF16) for _ in range(n_w)],
        compiler_params=_params(("arbitrary", "arbitrary")),
        name=name,
    )(x, *([w] * n_w), *([] if aux is None else [aux]))


def _identity(acc):
    return acc


def _swiglu(gate, up):
    return _silu(gate) * up


def _t5_bucket_np(dist):
    dist = np.maximum(dist, 0)
    log_ratio = np.log(np.maximum(dist, 1).astype(np.float64) / MAX_EXACT) / math.log(REL_MAX_DISTANCE / MAX_EXACT)
    large = np.minimum(MAX_EXACT + (log_ratio * (NUM_BUCKETS - MAX_EXACT)).astype(np.int64), NUM_BUCKETS - 1)
    return np.where(dist < MAX_EXACT, dist, large)


def _bucket_tables():
    qi = np.arange(ATTN_BLOCK)[:, None]
    ki = np.arange(2 * ATTN_BLOCK)[None, :]
    rel = qi - ki + ATTN_BLOCK
    tables = []
    for window, dilation in DILATED_GROUPS:
        band = (rel >= 0) & (rel <= window // dilation)
        tables.append(np.where(band, _t5_bucket_np(rel * dilation), -1))
    return np.stack(tables).astype(np.int32)


def _bias_kernel(rb_ref, bucket_ref, o_ref, *, heads):
    g, h = pl.program_id(0), pl.program_id(1)
    bucket = bucket_ref[...]
    acc = jnp.full(bucket.shape, NEG_INF, F32)
    for b in range(NUM_BUCKETS):
        acc = jnp.where(bucket == b, rb_ref[b, g * heads + h], acc)
    o_ref[...] = acc


def _bias_tables(rel_bias, heads):
    ng = len(DILATED_GROUPS)
    buckets = jnp.asarray(_bucket_tables())
    return pl.pallas_call(
        functools.partial(_bias_kernel, heads=heads),
        grid=(ng, heads),
        in_specs=[
            pl.BlockSpec(memory_space=pltpu.SMEM),
            pl.BlockSpec((None, ATTN_BLOCK, 2 * ATTN_BLOCK), lambda g, h: (g, 0, 0)),
        ],
        out_specs=pl.BlockSpec((None, None, ATTN_BLOCK, 2 * ATTN_BLOCK), lambda g, h: (g, h, 0, 0)),
        out_shape=jax.ShapeDtypeStruct((ng, heads, ATTN_BLOCK, 2 * ATTN_BLOCK), F32),
        compiler_params=_params(("arbitrary", "arbitrary")),
        name="attn_bias",
    )(rel_bias, buckets)


def _attn_kernel(q_ref, kc_ref, kp_ref, vc_ref, vp_ref, bias_ref, o_ref, lse_ref, *, heads):
    has_prev = pl.program_id(1) > 0
    scale = HEAD_DIM ** -0.5
    lane = lax.broadcasted_iota(jnp.int32, (ATTN_BLOCK, ATTN_BLOCK), 1)
    lse_all = jnp.zeros((ATTN_BLOCK, ATTN_BLOCK), F32)
    ones = jnp.ones((ATTN_BLOCK, HEAD_DIM), BF16)
    for h in range(heads):
        cols = slice(h * HEAD_DIM, (h + 1) * HEAD_DIM)
        q = q_ref[:, cols]
        s_prev = _dot_nt(q, kp_ref[:, cols]) * scale + bias_ref[h, :, :ATTN_BLOCK]
        s_prev = jnp.where(has_prev, s_prev, NEG_INF)
        s_cur = _dot_nt(q, kc_ref[:, cols]) * scale + bias_ref[h, :, ATTN_BLOCK:]
        m = jnp.maximum(jnp.max(s_prev, axis=-1, keepdims=True), jnp.max(s_cur, axis=-1, keepdims=True))
        p_prev = jnp.exp(s_prev - m).astype(BF16)
        p_cur = jnp.exp(s_cur - m).astype(BF16)
        ov = (_dot(p_prev, jnp.concatenate([vp_ref[:, cols], ones], axis=1))
              + _dot(p_cur, jnp.concatenate([vc_ref[:, cols], ones], axis=1)))
        denom = ov[:, HEAD_DIM:]
        o_ref[:, cols] = (ov[:, :HEAD_DIM] / denom).astype(o_ref.dtype)
        lse_all = jnp.where(lane == h, m + jnp.log(denom[:, 0:1]), lse_all)
    lse_ref[...] = lse_all


def _attn_group(proj, bias_g, sub_len, heads):
    rows = proj.shape[0]
    width = heads * HEAD_DIM
    nb = sub_len // ATTN_BLOCK
    blk = (ATTN_BLOCK, width)

    def cur(which):
        return lambda s, i: (s * nb + i, which)

    def prev(which):
        return lambda s, i: (s * nb + jnp.maximum(i - 1, 0), which)

    return pl.pallas_call(
        functools.partial(_attn_kernel, heads=heads),
        grid=(rows // sub_len, nb),
        in_specs=[
            pl.BlockSpec(blk, cur(0)),
            pl.BlockSpec(blk, cur(1)),
            pl.BlockSpec(blk, prev(1)),
            pl.BlockSpec(blk, cur(2)),
            pl.BlockSpec(blk, prev(2)),
            pl.BlockSpec((heads, ATTN_BLOCK, 2 * ATTN_BLOCK), lambda s, i: (0, 0, 0)),
        ],
        out_specs=[
            pl.BlockSpec(blk, lambda s, i: (s * nb + i, 0)),
            pl.BlockSpec((ATTN_BLOCK, ATTN_BLOCK), lambda s, i: (s * nb + i, 0)),
        ],
        out_shape=[jax.ShapeDtypeStruct((rows, width), BF16), jax.ShapeDtypeStruct((rows, ATTN_BLOCK), F32)],
        compiler_params=_params(("parallel", "arbitrary")),
        name="attn_group",
    )(proj, proj, proj, proj, proj, bias_g)


def _merge_outproj_kernel(*refs, heads, dilations):
    ng = len(dilations)
    o_refs, l_refs = refs[:ng], refs[ng : 2 * ng]
    w_ref, h_ref, ga_ref, gb_ref, h_out_ref, yn_ref, merged_even_ref, merged_odd_ref, alpha_ref, acc_ref = refs[2 * ng :]
    bm = acc_ref.shape[1]
    ii = pl.program_id(0)

    @pl.when(ii == 0)
    def _():
        merged_odd_ref[...] = jnp.zeros_like(merged_odd_ref)

    def step(merge_into_ref, multiply_ref):
        for g, d in enumerate(dilations):
            for r in range(d):
                alpha_ref[g, _strided_rows(r, bm // d, d), :] = l_refs[g][r]
        lses = [alpha_ref[g] for g in range(ng)]
        mx = functools.reduce(jnp.maximum, lses)
        es = [jnp.exp(l - mx) for l in lses]
        inv = 1.0 / functools.reduce(jnp.add, es)
        for g in range(ng):
            alpha_ref[g] = es[g] * inv
        for g, d in enumerate(dilations):
            for r in range(d):
                rows = _strided_rows(r, bm // d, d)
                alpha = alpha_ref[g, rows, :]
                for h in range(heads):
                    cols = slice(h * HEAD_DIM, (h + 1) * HEAD_DIM)
                    term = alpha[:, h : h + 1] * o_refs[g][r, :, cols].astype(F32)
                    if g == 0:
                        acc_ref[h, rows, :] = term
                    else:
                        acc_ref[h, rows, :] += term
        for h in range(heads):
            merge_into_ref[:, h * HEAD_DIM : (h + 1) * HEAD_DIM] = acc_ref[h].astype(merge_into_ref.dtype)

        y = _dot(multiply_ref[...], w_ref[...])
        h_new = h_ref[...] + _rms(y, ga_ref[...])
        h_out_ref[...] = h_new
        yn_ref[...] = _rms(h_new, gb_ref[...]).astype(yn_ref.dtype)

    pl.when(ii % 2 == 0)(functools.partial(step, merged_even_ref, merged_odd_ref))
    pl.when(ii % 2 == 1)(functools.partial(step, merged_odd_ref, merged_even_ref))


def _merge_outproj(outs, lses, w_out, h, gain_post, gain_next, heads, batch, seq, dilations, bm=256):
    width = heads * HEAD_DIM
    n = w_out.shape[1]
    bm = _tile(seq, bm)
    nt = seq // bm
    tiles = batch * nt

    def in_map(ii):
        t = jnp.minimum(ii, tiles - 1)
        return (t // nt, 0, t % nt, 0)

    row = pl.BlockSpec((bm, n), lambda ii: (jnp.maximum(ii - 1, 0), 0))
    vec = pl.BlockSpec((1, n), lambda ii: (0, 0))
    o_specs = [pl.BlockSpec((None, d, bm // d, width), in_map) for d in dilations]
    l_specs = [pl.BlockSpec((None, d, bm // d, ATTN_BLOCK), in_map) for d in dilations]
    outs = [o.reshape(batch, d, seq // d, width) for o, d in zip(outs, dilations)]
    lses = [l.reshape(batch, d, seq // d, ATTN_BLOCK) for l, d in zip(lses, dilations)]
    return pl.pallas_call(
        functools.partial(_merge_outproj_kernel, heads=heads, dilations=dilations),
        grid=(tiles + 1,),
        in_specs=o_specs + l_specs + [pl.BlockSpec((width, n), lambda ii: (0, 0), pipeline_mode=pl.Buffered(1)),
                                      row, vec, vec],
        out_specs=[row, row],
        out_shape=[jax.ShapeDtypeStruct((batch * seq, n), F32), jax.ShapeDtypeStruct((batch * seq, n), BF16)],
        scratch_shapes=[
            pltpu.VMEM((bm, width), BF16),
            pltpu.VMEM((bm, width), BF16),
            pltpu.VMEM((len(dilations), bm, ATTN_BLOCK), F32),
            pltpu.VMEM((heads, bm, HEAD_DIM), F32),
        ],
        compiler_params=_params(("arbitrary",)),
        name="attn_merge_outproj",
    )(*outs, *lses, w_out, h, gain_post.reshape(1, n), gain_next.reshape(1, n))


def _hgrn_query(acc):
    return _silu(acc) * (HEAD_DIM ** -0.5)


def _hgrn_forget(acc, lb_logits, *, layer):
    p = jnp.exp(lb_logits - jnp.max(lb_logits, axis=0, keepdims=True))
    p = p / jnp.sum(p, axis=0, keepdims=True)
    lb = jnp.sum(p[: layer + 1], axis=0, keepdims=True) - p[0:1]
    return lb + (1.0 - lb) * jax.nn.sigmoid(acc)


def _hgrn_rec_kernel(q_ref, f_ref, v_ref, g_ref, gain_ref, o_ref, state_ref, b_ref, *, heads_per_step, chunks):
    @pl.when(pl.program_id(2) == 0)
    def _():
        state_ref[...] = jnp.zeros_like(state_ref)

    row = lax.broadcasted_iota(jnp.int32, (HG_CHUNK, HG_CHUNK), 0)
    col = lax.broadcasted_iota(jnp.int32, (HG_CHUNK, HG_CHUNK), 1)
    causal = row >= col
    tri = causal.astype(BF16)
    tri2 = jnp.concatenate([tri, tri], axis=1)
    mid = HG_CHUNK // 2
    gain = gain_ref[...]
    heads = range(heads_per_step)
    cols = [slice(h * HEAD_DIM, (h + 1) * HEAD_DIM) for h in heads]

    for c in range(chunks):
        rows = pl.ds(c * HG_CHUNK, HG_CHUNK)
        log_f = jnp.log(f_ref[rows, :])
        hi = log_f.astype(BF16)
        lo = (log_f - hi.astype(F32)).astype(BF16)
        b_ref[rows, :] = _dot(tri2, jnp.concatenate([hi, lo], axis=0))

    def chunk(c, carry):
        rows = pl.ds(pl.multiple_of(c * HG_CHUNK, HG_CHUNK), HG_CHUNK)
        b = [b_ref[rows, cols[h]] for h in heads]
        b_mid = [b[h][mid - 1 : mid, :] for h in heads]
        b_last = [b[h][HG_CHUNK - 1 :, :] for h in heads]
        key = [1.0 - f_ref[rows, cols[h]] for h in heads]
        q = [q_ref[rows, cols[h]] for h in heads]
        v = [v_ref[rows, cols[h]] for h in heads]
        scores = [_dot_nt((q[h] * jnp.exp(b[h] - b_mid[h])).astype(BF16),
                          (key[h] * jnp.exp(b_mid[h] - b[h])).astype(BF16)) for h in heads]
        probs = [jnp.where(causal, scores[h], 0.0).astype(BF16) for h in heads]
        state = [state_ref[h] for h in heads]
        o = [_dot(jnp.concatenate([(q[h] * jnp.exp(b[h])).astype(BF16), probs[h]], axis=1),
                  jnp.concatenate([state[h].astype(BF16), v[h]], axis=0)) for h in heads]
        update = [_dot_tn((key[h] * jnp.exp(b_last[h] - b[h])).astype(BF16), v[h]) for h in heads]
        for h in heads:
            decay = jnp.transpose(jnp.broadcast_to(jnp.exp(b_last[h]), (8, HEAD_DIM)))[:, 0:1]
            state_ref[h] = state[h] * decay + update[h]
        for h in heads:
            o_ref[rows, cols[h]] = (_rms(o[h], gain) * g_ref[rows, cols[h]]).astype(o_ref.dtype)
        return carry

    lax.fori_loop(0, chunks, chunk, 0)


def _hgrn_recurrence(q, f, v, g, out_gain, batch, seq, heads_per_step=8, ts=512):
    m, width = q.shape
    heads = width // HEAD_DIM
    heads_per_step = math.gcd(heads, heads_per_step)
    ts = _tile(seq, ts)
    bw = heads_per_step * HEAD_DIM
    nt = seq // ts
    spec = pl.BlockSpec((ts, bw), lambda b, hg, t: (b * nt + t, hg))
    return pl.pallas_call(
        functools.partial(_hgrn_rec_kernel, heads_per_step=heads_per_step, chunks=ts // HG_CHUNK),
        grid=(batch, heads // heads_per_step, nt),
        in_specs=[spec, spec, spec, spec, pl.BlockSpec((1, HEAD_DIM), lambda b, hg, t: (0, 0))],
        out_specs=spec,
        out_shape=jax.ShapeDtypeStruct((m, width), BF16),
        scratch_shapes=[pltpu.VMEM((heads_per_step, HEAD_DIM, HEAD_DIM), F32), pltpu.VMEM((ts, bw), F32)],
        compiler_params=_params(("parallel", "parallel", "arbitrary")),
        name="hgrn_recurrence",
    )(q, f, v, g, out_gain.reshape(1, HEAD_DIM))


def kernel(x, norm_gains, rel_bias, attn_w_in, attn_w_out, hgrn_w_in, hgrn_lb_logits, hgrn_out_gain, hgrn_w_out,
           ffn_w_in, ffn_w_out):
    batch, seq, d = x.shape
    m = batch * seq
    depth = norm_gains.shape[0]
    attn_heads = attn_w_out.shape[1] // HEAD_DIM
    attn_width = attn_heads * HEAD_DIM
    dilations = tuple(dil for _, dil in DILATED_GROUPS)
    d_ff = ffn_w_out.shape[1]
    h = x.reshape(m, d)
    ffn_w_out_bf16 = ffn_w_out.astype(BF16)
    hgrn_w_out_bf16 = hgrn_w_out.astype(BF16)
    yn = None
    bias = None
    for i in range(depth):
        gains = norm_gains[i]
        if i % 2 == 0:
            if bias is None:
                bias = _bias_tables(rel_bias, attn_heads)
            yns = _prenorm_dilated(h, gains[0], batch, seq, dilations)
            outs, lses = [], []
            for g, dil in enumerate(dilations):
                proj = _proj(yns[g], attn_w_in, i // 2, [g * 3 * attn_width], 3 * attn_width, _identity, BF16,
                             name="attn_in")
                o, l = _attn_group(proj, bias[g], seq // dil, attn_heads)
                outs.append(o)
                lses.append(l)
            h, yn = _merge_outproj(outs, lses, attn_w_out[i // 2].astype(BF16), h, gains[1], gains[2], attn_heads,
                                   batch, seq, dilations)
        else:
            if yn is None:
                yn = _prenorm(h, gains[0])
            width = hgrn_w_in.shape[2] // 4
            hproj = functools.partial(_proj, yn, hgrn_w_in, i // 2)
            q = hproj([0], width, _hgrn_query, F32, name="hgrn_in_q")
            f = hproj([width], width, functools.partial(_hgrn_forget, layer=i), F32, aux=hgrn_lb_logits,
                      name="hgrn_in_f")
            v = hproj([2 * width], width, _identity, BF16, name="hgrn_in_i")
            g = hproj([3 * width], width, _silu, F32, name="hgrn_in_g")
            o = _hgrn_recurrence(q, f, v, g, hgrn_out_gain[i // 2], batch, seq)
            y = _matmul(o, hgrn_w_out_bf16, i // 2, F32, name="hgrn_out")
            h, yn = _postnorm(h, y, gains[1], gains[2])
        hidden = _proj(yn, ffn_w_in, i, [0, d_ff], d_ff, _swiglu, BF16, bm=2048, bn=256, name="ffn_in")
        y = _matmul(hidden, ffn_w_out_bf16, i, F32, bm=512, bn=512, name="ffn_out")
        if i + 1 < depth and (i + 1) % 2 == 1:
            h, yn = _postnorm(h, y, gains[3], norm_gains[i + 1, 0])
        else:
            h, yn = _postnorm(h, y, gains[3], None), None
    return h.reshape(batch, seq, d)
```

```python
import functools
import math

import numpy as np
import jax
import jax.numpy as jnp
from jax import lax
from jax.experimental import pallas as pl
from jax.experimental.pallas import tpu as pltpu

RMS_EPS = 1e-6
NEG_INF = -1e30

DILATED_GROUPS = ((128, 1), (512, 4), (2048, 16))
HEAD_DIM = 128
ATTN_BLOCK = 128
NUM_BUCKETS = 32
MAX_EXACT = 16
REL_MAX_DISTANCE = 2048
HG_CHUNK = 64
PROJ_ROW_CHUNKS = 4
PROJ_BN = 1024
FFN_BN = 256
FFN_HEAD_COLS = 768
HGRN_HEAD_COLS = 1024

V7X_VMEM_BYTES = 64 * 1024 * 1024
VMEM_LIMIT = V7X_VMEM_BYTES - 8 * 1024 * 1024

BF16 = jnp.bfloat16
F32 = jnp.float32


def _params(semantics, vmem=VMEM_LIMIT):
    return pltpu.CompilerParams(dimension_semantics=semantics, vmem_limit_bytes=vmem)


def _tile(n, pref):
    if n <= pref:
        return n
    t = pref - pref % 128
    while t >= 128:
        if n % t == 0:
            return t
        t -= 128
    raise ValueError(f"no 128-multiple tile of {n} below {pref}")


def _dot(a, b):
    return jnp.dot(a, b, preferred_element_type=F32)


def _dot_nt(a, b):
    return lax.dot_general(a, b, (((1,), (1,)), ((), ())), preferred_element_type=F32)


def _dot_tn(a, b):
    return lax.dot_general(a, b, (((0,), (0,)), ((), ())), preferred_element_type=F32)


def _rms(x, gain):
    return x * lax.rsqrt(jnp.mean(x * x, axis=-1, keepdims=True) + RMS_EPS) * gain


def _silu(x):
    return x * jax.nn.sigmoid(x)


def _strided_rows(r, count, stride):
    return pl.ds(r, count) if stride == 1 else pl.ds(r, count, stride=stride)


def _prenorm_kernel(x_ref, g_ref, o_ref):
    o_ref[...] = _rms(x_ref[...], g_ref[...]).astype(o_ref.dtype)


def _prenorm(x, gain, tr=256):
    m, d = x.shape
    tr = _tile(m, tr)
    return pl.pallas_call(
        _prenorm_kernel,
        grid=(m // tr,),
        in_specs=[pl.BlockSpec((tr, d), lambda i: (i, 0)), pl.BlockSpec((1, d), lambda i: (0, 0))],
        out_specs=pl.BlockSpec((tr, d), lambda i: (i, 0)),
        out_shape=jax.ShapeDtypeStruct((m, d), BF16),
        compiler_params=_params(("parallel",)),
        name="prenorm",
    )(x, gain.reshape(1, d))


def _prenorm_dilated_kernel(x_ref, g_ref, *refs, dilations):
    out_refs, y_ref = refs[:-1], refs[-1]
    y = _rms(x_ref[...], g_ref[...])
    tr = y.shape[0]
    for c in range(y_ref.shape[0]):
        cols = slice(c * 128, (c + 1) * 128)
        y_ref[c] = y[:, cols]
        for o_ref, d in zip(out_refs, dilations):
            for r in range(d):
                o_ref[r, :, cols] = y_ref[c, _strided_rows(r, tr // d, d), :].astype(o_ref.dtype)


def _prenorm_dilated(x, gain, batch, seq, dilations, tr=256):
    m, d = x.shape
    tr = _tile(seq, tr)
    nt = seq // tr
    outs = pl.pallas_call(
        functools.partial(_prenorm_dilated_kernel, dilations=dilations),
        grid=(m // tr,),
        in_specs=[pl.BlockSpec((tr, d), lambda i: (i, 0)), pl.BlockSpec((1, d), lambda i: (0, 0))],
        out_specs=[pl.BlockSpec((None, dil, tr // dil, d), lambda i: (i // nt, 0, i % nt, 0)) for dil in dilations],
        out_shape=[jax.ShapeDtypeStruct((batch, dil, seq // dil, d), BF16) for dil in dilations],
        scratch_shapes=[pltpu.VMEM((d // 128, tr, 128), F32)],
        compiler_params=_params(("parallel",)),
        name="prenorm_dilated",
    )(x, gain.reshape(1, d))
    return [o.reshape(m, d) for o in outs]


def _postnorm_kernel(h_ref, y_ref, ga_ref, gb_ref, h_out_ref, yn_ref):
    h = h_ref[...] + _rms(y_ref[...], ga_ref[...])
    h_out_ref[...] = h
    yn_ref[...] = _rms(h, gb_ref[...]).astype(yn_ref.dtype)


def _postnorm_last_kernel(h_ref, y_ref, ga_ref, h_out_ref):
    h_out_ref[...] = h_ref[...] + _rms(y_ref[...], ga_ref[...])


def _postnorm(h, y, gain_post, gain_next, tr=256):
    m, d = h.shape
    tr = _tile(m, tr)
    row = pl.BlockSpec((tr, d), lambda i: (i, 0))
    vec = pl.BlockSpec((1, d), lambda i: (0, 0))
    if gain_next is None:
        return pl.pallas_call(
            _postnorm_last_kernel,
            grid=(m // tr,),
            in_specs=[row, row, vec],
            out_specs=row,
            out_shape=jax.ShapeDtypeStruct((m, d), F32),
            compiler_params=_params(("parallel",)),
            name="postnorm_last",
        )(h, y, gain_post.reshape(1, d))
    return pl.pallas_call(
        _postnorm_kernel,
        grid=(m // tr,),
        in_specs=[row, row, vec, vec],
        out_specs=[row, row],
        out_shape=[jax.ShapeDtypeStruct((m, d), F32), jax.ShapeDtypeStruct((m, d), BF16)],
        compiler_params=_params(("parallel",)),
        name="postnorm",
    )(h, y, gain_post.reshape(1, d), gain_next.reshape(1, d))


def _postnorm_head_kernel(h_ref, y_ref, ga_ref, gb_ref, *refs, n_w, epilogue):
    w_refs = refs[:n_w]
    h_out_ref, yn_ref, head_ref = refs[n_w:]
    h = h_ref[...] + _rms(y_ref[...], ga_ref[...])
    h_out_ref[...] = h
    yn = _rms(h, gb_ref[...]).astype(yn_ref.dtype)
    yn_ref[...] = yn
    head_ref[...] = epilogue(*[_dot(yn, w_ref[...]) for w_ref in w_refs]).astype(head_ref.dtype)


def _postnorm_head(h, y, gain_post, gain_next, w_heads, width, epilogue, out_dtype, tr=256, name="postnorm_head"):
    m, d = h.shape
    tr = _tile(m, tr)
    head_cols = w_heads[0].shape[1]
    row = pl.BlockSpec((tr, d), lambda i: (i, 0))
    vec = pl.BlockSpec((1, d), lambda i: (0, 0))
    w_spec = pl.BlockSpec((d, head_cols), lambda i: (0, 0), pipeline_mode=pl.Buffered(1))
    return pl.pallas_call(
        functools.partial(_postnorm_head_kernel, n_w=len(w_heads), epilogue=epilogue),
        grid=(m // tr,),
        in_specs=[row, row, vec, vec] + [w_spec] * len(w_heads),
        out_specs=[row, row, pl.BlockSpec((tr, head_cols), lambda i: (i, 0))],
        out_shape=[jax.ShapeDtypeStruct((m, d), F32), jax.ShapeDtypeStruct((m, d), BF16),
                   jax.ShapeDtypeStruct((m, width), out_dtype)],
        compiler_params=_params(("parallel",)),
        name=name,
    )(h, y, gain_post.reshape(1, d), gain_next.reshape(1, d), *w_heads)


def _matmul_kernel(x_ref, w_ref, o_ref):
    o_ref[...] = _dot(x_ref[...], w_ref[...]).astype(o_ref.dtype)


def _matmul(x, w, layer, out_dtype, bm=1024, bn=1024, name="matmul"):
    m, k = x.shape
    n = w.shape[2]
    bm, bn = _tile(m, bm), _tile(n, bn)
    return pl.pallas_call(
        _matmul_kernel,
        grid=(m // bm, n // bn),
        in_specs=[pl.BlockSpec((bm, k), lambda i, j: (i, 0)), pl.BlockSpec((None, k, bn), lambda i, j: (layer, 0, j))],
        out_specs=pl.BlockSpec((bm, bn), lambda i, j: (i, j)),
        out_shape=jax.ShapeDtypeStruct((m, n), out_dtype),
        compiler_params=_params(("parallel", "arbitrary")),
        name=name,
    )(x, w)


def _proj_kernel(*refs, n_w, has_aux, has_init, epilogue):
    x_ref, w_refs = refs[0], refs[1 : 1 + n_w]
    aux_ref = refs[1 + n_w] if has_aux else None
    n_in = 1 + n_w + has_aux + has_init
    o_ref = refs[n_in]
    wb_refs = refs[n_in + 1 :]
    jj, i = pl.program_id(0), pl.program_id(1)
    slab = w_refs[0].shape[0]

    @pl.when(jj < pl.num_programs(0) - 1)
    def _():
        rows = pl.ds(pl.multiple_of(i * slab, slab), slab)
        for w_ref, wb_ref in zip(w_refs, wb_refs):
            wb_ref[jj % 2, rows, :] = w_ref[...].astype(wb_ref.dtype)

    @pl.when(jj > 0)
    def _():
        chunk = x_ref.shape[0] // PROJ_ROW_CHUNKS
        for r in range(PROJ_ROW_CHUNKS):
            rows = slice(r * chunk, (r + 1) * chunk)
            x = x_ref[rows, :]
            accs = [_dot(x, wb_ref[(jj - 1) % 2]) for wb_ref in wb_refs]
            out = epilogue(*accs, aux_ref[...]) if has_aux else epilogue(*accs)
            o_ref[rows, :] = out.astype(o_ref.dtype)


def _proj(x, w, layer, col_offsets, width, epilogue, out_dtype, aux=None, bm=1024, bn=1024, name="proj", col0=0,
          out_init=None):
    m, k = x.shape
    bm, bn = _tile(m, bm), _tile(width, bn)
    n_w = len(col_offsets)
    nj, ni = (width - col0) // bn, m // bm
    slab = k // ni
    assert k % ni == 0 and slab % 16 == 0, (k, ni)
    assert all(c % bn == 0 for c in col_offsets) and col0 % bn == 0, (col_offsets, col0, bn)

    def w_map(jj, i, c):
        return (layer, jnp.where(jj < nj, i, ni - 1), jnp.minimum(jj, nj - 1) + c + col0 // bn)

    def row_map(jj, i):
        return jnp.where(jj > 0, i, 0)

    def col_map(jj):
        return jnp.maximum(jj - 1, 0) + col0 // bn

    w_specs = [pl.BlockSpec((None, slab, bn), functools.partial(w_map, c=c // bn)) for c in col_offsets]
    aux_specs = [] if aux is None else [pl.BlockSpec((aux.shape[0], bn), lambda jj, i: (0, col_map(jj)))]
    init_specs = [] if out_init is None else [pl.BlockSpec(memory_space=pl.ANY)]
    n_in = 1 + n_w + len(aux_specs) + len(init_specs)
    return pl.pallas_call(
        functools.partial(_proj_kernel, n_w=n_w, has_aux=aux is not None, has_init=out_init is not None,
                          epilogue=epilogue),
        grid=(nj + 1, ni),
        in_specs=[pl.BlockSpec((bm, k), lambda jj, i: (row_map(jj, i), 0))] + w_specs + aux_specs + init_specs,
        out_specs=pl.BlockSpec((bm, bn), lambda jj, i: (row_map(jj, i), col_map(jj))),
        out_shape=jax.ShapeDtypeStruct((m, width), out_dtype),
        scratch_shapes=[pltpu.VMEM((2, k, bn), BF16) for _ in range(n_w)],
        input_output_aliases={} if out_init is None else {n_in - 1: 0},
        compiler_params=_params(("arbitrary", "arbitrary")),
        name=name,
    )(x, *([w] * n_w), *([] if aux is None else [aux]), *([] if out_init is None else [out_init]))


def _identity(acc):
    return acc


def _swiglu(gate, up):
    return _silu(gate) * up


def _t5_bucket_np(dist):
    dist = np.maximum(dist, 0)
    log_ratio = np.log(np.maximum(dist, 1).astype(np.float64) / MAX_EXACT) / math.log(REL_MAX_DISTANCE / MAX_EXACT)
    large = np.minimum(MAX_EXACT + (log_ratio * (NUM_BUCKETS - MAX_EXACT)).astype(np.int64), NUM_BUCKETS - 1)
    return np.where(dist < MAX_EXACT, dist, large)


def _bucket_tables():
    qi = np.arange(ATTN_BLOCK)[:, None]
    ki = np.arange(2 * ATTN_BLOCK)[None, :]
    rel = qi - ki + ATTN_BLOCK
    tables = []
    for window, dilation in DILATED_GROUPS:
        band = (rel >= 0) & (rel <= window // dilation)
        tables.append(np.where(band, _t5_bucket_np(rel * dilation), -1))
    return np.stack(tables).astype(np.int32)


def _bias_kernel(rb_ref, bucket_ref, o_ref, *, heads):
    g, h = pl.program_id(0), pl.program_id(1)
    bucket = bucket_ref[...]
    acc = jnp.full(bucket.shape, NEG_INF, F32)
    for b in range(NUM_BUCKETS):
        acc = jnp.where(bucket == b, rb_ref[b, g * heads + h], acc)
    o_ref[...] = acc


def _bias_tables(rel_bias, heads):
    ng = len(DILATED_GROUPS)
    buckets = jnp.asarray(_bucket_tables())
    return pl.pallas_call(
        functools.partial(_bias_kernel, heads=heads),
        grid=(ng, heads),
        in_specs=[
            pl.BlockSpec(memory_space=pltpu.SMEM),
            pl.BlockSpec((None, ATTN_BLOCK, 2 * ATTN_BLOCK), lambda g, h: (g, 0, 0)),
        ],
        out_specs=pl.BlockSpec((None, None, ATTN_BLOCK, 2 * ATTN_BLOCK), lambda g, h: (g, h, 0, 0)),
        out_shape=jax.ShapeDtypeStruct((ng, heads, ATTN_BLOCK, 2 * ATTN_BLOCK), F32),
        compiler_params=_params(("arbitrary", "arbitrary")),
        name="attn_bias",
    )(rel_bias, buckets)


def _attn_kernel(q_ref, kc_ref, kp_ref, vc_ref, vp_ref, bias_ref, o_ref, lse_ref, *, heads):
    has_prev = pl.program_id(1) > 0
    scale = HEAD_DIM ** -0.5
    lane = lax.broadcasted_iota(jnp.int32, (ATTN_BLOCK, ATTN_BLOCK), 1)
    lse_all = jnp.zeros((ATTN_BLOCK, ATTN_BLOCK), F32)
    ones = jnp.ones((ATTN_BLOCK, HEAD_DIM), BF16)
    for h in range(heads):
        cols = slice(h * HEAD_DIM, (h + 1) * HEAD_DIM)
        q = q_ref[:, cols]
        s_prev = _dot_nt(q, kp_ref[:, cols]) * scale + bias_ref[h, :, :ATTN_BLOCK]
        s_prev = jnp.where(has_prev, s_prev, NEG_INF)
        s_cur = _dot_nt(q, kc_ref[:, cols]) * scale + bias_ref[h, :, ATTN_BLOCK:]
        m = jnp.maximum(jnp.max(s_prev, axis=-1, keepdims=True), jnp.max(s_cur, axis=-1, keepdims=True))
        p_prev = jnp.exp(s_prev - m).astype(BF16)
        p_cur = jnp.exp(s_cur - m).astype(BF16)
        ov = (_dot(p_prev, jnp.concatenate([vp_ref[:, cols], ones], axis=1))
              + _dot(p_cur, jnp.concatenate([vc_ref[:, cols], ones], axis=1)))
        denom = ov[:, HEAD_DIM:]
        o_ref[:, cols] = (ov[:, :HEAD_DIM] / denom).astype(o_ref.dtype)
        lse_all = jnp.where(lane == h, m + jnp.log(denom[:, 0:1]), lse_all)
    lse_ref[...] = lse_all


def _attn_group(proj, bias_g, sub_len, heads):
    rows = proj.shape[0]
    width = heads * HEAD_DIM
    nb = sub_len // ATTN_BLOCK
    blk = (ATTN_BLOCK, width)

    def cur(which):
        return lambda s, i: (s * nb + i, which)

    def prev(which):
        return lambda s, i: (s * nb + jnp.maximum(i - 1, 0), which)

    return pl.pallas_call(
        functools.partial(_attn_kernel, heads=heads),
        grid=(rows // sub_len, nb),
        in_specs=[
            pl.BlockSpec(blk, cur(0)),
            pl.BlockSpec(blk, cur(1)),
            pl.BlockSpec(blk, prev(1)),
            pl.BlockSpec(blk, cur(2)),
            pl.BlockSpec(blk, prev(2)),
            pl.BlockSpec((heads, ATTN_BLOCK, 2 * ATTN_BLOCK), lambda s, i: (0, 0, 0)),
        ],
        out_specs=[
            pl.BlockSpec(blk, lambda s, i: (s * nb + i, 0)),
            pl.BlockSpec((ATTN_BLOCK, ATTN_BLOCK), lambda s, i: (s * nb + i, 0)),
        ],
        out_shape=[jax.ShapeDtypeStruct((rows, width), BF16), jax.ShapeDtypeStruct((rows, ATTN_BLOCK), F32)],
        compiler_params=_params(("parallel", "arbitrary")),
        name="attn_group",
    )(proj, proj, proj, proj, proj, bias_g)


def _merge_outproj_kernel(*refs, heads, dilations):
    ng = len(dilations)
    o_refs, l_refs = refs[:ng], refs[ng : 2 * ng]
    w_ref, h_ref, ga_ref, gb_ref, h_out_ref, yn_ref, merged_even_ref, merged_odd_ref, alpha_ref, acc_ref = refs[2 * ng :]
    bm = acc_ref.shape[1]
    ii = pl.program_id(0)

    @pl.when(ii == 0)
    def _():
        merged_odd_ref[...] = jnp.zeros_like(merged_odd_ref)

    def step(merge_into_ref, multiply_ref):
        for g, d in enumerate(dilations):
            for r in range(d):
                alpha_ref[g, _strided_rows(r, bm // d, d), :] = l_refs[g][r]
        lses = [alpha_ref[g] for g in range(ng)]
        mx = functools.reduce(jnp.maximum, lses)
        es = [jnp.exp(l - mx) for l in lses]
        inv = 1.0 / functools.reduce(jnp.add, es)
        for g in range(ng):
            alpha_ref[g] = es[g] * inv
        for g, d in enumerate(dilations):
            for r in range(d):
                rows = _strided_rows(r, bm // d, d)
                alpha = alpha_ref[g, rows, :]
                for h in range(heads):
                    cols = slice(h * HEAD_DIM, (h + 1) * HEAD_DIM)
                    term = alpha[:, h : h + 1] * o_refs[g][r, :, cols].astype(F32)
                    if g == 0:
                        acc_ref[h, rows, :] = term
                    else:
                        acc_ref[h, rows, :] += term
        for h in range(heads):
            merge_into_ref[:, h * HEAD_DIM : (h + 1) * HEAD_DIM] = acc_ref[h].astype(merge_into_ref.dtype)

        y = _dot(multiply_ref[...], w_ref[...])
        h_new = h_ref[...] + _rms(y, ga_ref[...])
        h_out_ref[...] = h_new
        yn_ref[...] = _rms(h_new, gb_ref[...]).astype(yn_ref.dtype)

    pl.when(ii % 2 == 0)(functools.partial(step, merged_even_ref, merged_odd_ref))
    pl.when(ii % 2 == 1)(functools.partial(step, merged_odd_ref, merged_even_ref))


def _merge_outproj(outs, lses, w_out, h, gain_post, gain_next, heads, batch, seq, dilations, bm=256):
    width = heads * HEAD_DIM
    n = w_out.shape[1]
    bm = _tile(seq, bm)
    nt = seq // bm
    tiles = batch * nt

    def in_map(ii):
        t = jnp.minimum(ii, tiles - 1)
        return (t // nt, 0, t % nt, 0)

    row = pl.BlockSpec((bm, n), lambda ii: (jnp.maximum(ii - 1, 0), 0))
    vec = pl.BlockSpec((1, n), lambda ii: (0, 0))
    o_specs = [pl.BlockSpec((None, d, bm // d, width), in_map) for d in dilations]
    l_specs = [pl.BlockSpec((None, d, bm // d, ATTN_BLOCK), in_map) for d in dilations]
    outs = [o.reshape(batch, d, seq // d, width) for o, d in zip(outs, dilations)]
    lses = [l.reshape(batch, d, seq // d, ATTN_BLOCK) for l, d in zip(lses, dilations)]
    return pl.pallas_call(
        functools.partial(_merge_outproj_kernel, heads=heads, dilations=dilations),
        grid=(tiles + 1,),
        in_specs=o_specs + l_specs + [pl.BlockSpec((width, n), lambda ii: (0, 0), pipeline_mode=pl.Buffered(1)),
                                      row, vec, vec],
        out_specs=[row, row],
        out_shape=[jax.ShapeDtypeStruct((batch * seq, n), F32), jax.ShapeDtypeStruct((batch * seq, n), BF16)],
        scratch_shapes=[
            pltpu.VMEM((bm, width), BF16),
            pltpu.VMEM((bm, width), BF16),
            pltpu.VMEM((len(dilations), bm, ATTN_BLOCK), F32),
            pltpu.VMEM((heads, bm, HEAD_DIM), F32),
        ],
        compiler_params=_params(("arbitrary",)),
        name="attn_merge_outproj",
    )(*outs, *lses, w_out, h, gain_post.reshape(1, n), gain_next.reshape(1, n))


def _hgrn_query(acc):
    return _silu(acc) * (HEAD_DIM ** -0.5)


def _hgrn_forget(acc, lb_logits, *, layer):
    p = jnp.exp(lb_logits - jnp.max(lb_logits, axis=0, keepdims=True))
    p = p / jnp.sum(p, axis=0, keepdims=True)
    lb = jnp.sum(p[: layer + 1], axis=0, keepdims=True) - p[0:1]
    return lb + (1.0 - lb) * jax.nn.sigmoid(acc)


def _hgrn_rec_kernel(q_ref, f_ref, v_ref, g_ref, gain_ref, o_ref, state_ref, b_ref, *, heads_per_step, chunks):
    @pl.when(pl.program_id(2) == 0)
    def _():
        state_ref[...] = jnp.zeros_like(state_ref)

    row = lax.broadcasted_iota(jnp.int32, (HG_CHUNK, HG_CHUNK), 0)
    col = lax.broadcasted_iota(jnp.int32, (HG_CHUNK, HG_CHUNK), 1)
    causal = row >= col
    tri = causal.astype(BF16)
    tri2 = jnp.concatenate([tri, tri], axis=1)
    mid = HG_CHUNK // 2
    gain = gain_ref[...]
    heads = range(heads_per_step)
    cols = [slice(h * HEAD_DIM, (h + 1) * HEAD_DIM) for h in heads]

    for c in range(chunks):
        rows = pl.ds(c * HG_CHUNK, HG_CHUNK)
        log_f = jnp.log(f_ref[rows, :])
        hi = log_f.astype(BF16)
        lo = (log_f - hi.astype(F32)).astype(BF16)
        b_ref[rows, :] = _dot(tri2, jnp.concatenate([hi, lo], axis=0))

    def chunk(c, carry):
        rows = pl.ds(pl.multiple_of(c * HG_CHUNK, HG_CHUNK), HG_CHUNK)
        b = [b_ref[rows, cols[h]] for h in heads]
        b_mid = [b[h][mid - 1 : mid, :] for h in heads]
        b_last = [b[h][HG_CHUNK - 1 :, :] for h in heads]
        key = [1.0 - f_ref[rows, cols[h]] for h in heads]
        q = [q_ref[rows, cols[h]] for h in heads]
        v = [v_ref[rows, cols[h]] for h in heads]
        scores = [_dot_nt((q[h] * jnp.exp(b[h] - b_mid[h])).astype(BF16),
                          (key[h] * jnp.exp(b_mid[h] - b[h])).astype(BF16)) for h in heads]
        probs = [jnp.where(causal, scores[h], 0.0).astype(BF16) for h in heads]
        state = [state_ref[h] for h in heads]
        o = [_dot(jnp.concatenate([(q[h] * jnp.exp(b[h])).astype(BF16), probs[h]], axis=1),
                  jnp.concatenate([state[h].astype(BF16), v[h]], axis=0)) for h in heads]
        update = [_dot_tn((key[h] * jnp.exp(b_last[h] - b[h])).astype(BF16), v[h]) for h in heads]
        for h in heads:
            decay = jnp.transpose(jnp.broadcast_to(jnp.exp(b_last[h]), (8, HEAD_DIM)))[:, 0:1]
            state_ref[h] = state[h] * decay + update[h]
        for h in heads:
            o_ref[rows, cols[h]] = (_rms(o[h], gain) * g_ref[rows, cols[h]]).astype(o_ref.dtype)
        return carry

    lax.fori_loop(0, chunks, chunk, 0)


def _hgrn_recurrence(q, f, v, g, out_gain, batch, seq, heads_per_step=8, ts=512):
    m, width = q.shape
    heads = width // HEAD_DIM
    heads_per_step = math.gcd(heads, heads_per_step)
    ts = _tile(seq, ts)
    bw = heads_per_step * HEAD_DIM
    nt = seq // ts
    spec = pl.BlockSpec((ts, bw), lambda b, hg, t: (b * nt + t, hg))
    return pl.pallas_call(
        functools.partial(_hgrn_rec_kernel, heads_per_step=heads_per_step, chunks=ts // HG_CHUNK),
        grid=(batch, heads // heads_per_step, nt),
        in_specs=[spec, spec, spec, spec, pl.BlockSpec((1, HEAD_DIM), lambda b, hg, t: (0, 0))],
        out_specs=spec,
        out_shape=jax.ShapeDtypeStruct((m, width), BF16),
        scratch_shapes=[pltpu.VMEM((heads_per_step, HEAD_DIM, HEAD_DIM), F32), pltpu.VMEM((ts, bw), F32)],
        compiler_params=_params(("parallel", "parallel", "arbitrary")),
        name="hgrn_recurrence",
    )(q, f, v, g, out_gain.reshape(1, HEAD_DIM))


def _head_cols(width, bn, most):
    return min(most // bn, width // bn // 2) * bn


def kernel(x, norm_gains, rel_bias, attn_w_in, attn_w_out, hgrn_w_in, hgrn_lb_logits, hgrn_out_gain, hgrn_w_out,
           ffn_w_in, ffn_w_out):
    batch, seq, d = x.shape
    m = batch * seq
    depth = norm_gains.shape[0]
    attn_heads = attn_w_out.shape[1] // HEAD_DIM
    attn_width = attn_heads * HEAD_DIM
    dilations = tuple(dil for _, dil in DILATED_GROUPS)
    d_ff = ffn_w_out.shape[1]
    hg_width = hgrn_w_in.shape[2] // 4
    ffn_head = _head_cols(d_ff, FFN_BN, FFN_HEAD_COLS)
    hg_bn = _tile(hg_width, PROJ_BN)
    hg_head = _head_cols(hg_width, hg_bn, HGRN_HEAD_COLS)
    h = x.reshape(m, d)
    ffn_w_out_bf16 = ffn_w_out.astype(BF16)
    hgrn_w_out_bf16 = hgrn_w_out.astype(BF16)
    yn = q_head = hidden_head = None
    bias = None
    for i in range(depth):
        gains = norm_gains[i]
        if i % 2 == 0:
            if bias is None:
                bias = _bias_tables(rel_bias, attn_heads)
            yns = _prenorm_dilated(h, gains[0], batch, seq, dilations)
            outs, lses = [], []
            for g, dil in enumerate(dilations):
                proj = _proj(yns[g], attn_w_in, i // 2, [g * 3 * attn_width], 3 * attn_width, _identity, BF16,
                             bn=PROJ_BN, name="attn_in")
                o, l = _attn_group(proj, bias[g], seq // dil, attn_heads)
                outs.append(o)
                lses.append(l)
            h, yn = _merge_outproj(outs, lses, attn_w_out[i // 2].astype(BF16), h, gains[1], gains[2], attn_heads,
                                   batch, seq, dilations)
        else:
            if yn is None:
                yn = _prenorm(h, gains[0])
            hproj = functools.partial(_proj, yn, hgrn_w_in, i // 2, bn=PROJ_BN)
            q = hproj([0], hg_width, _hgrn_query, F32, name="hgrn_in_q", col0=0 if q_head is None else hg_head,
                      out_init=q_head)
            f = hproj([hg_width], hg_width, functools.partial(_hgrn_forget, layer=i), F32, aux=hgrn_lb_logits,
                      name="hgrn_in_f")
            v = hproj([2 * hg_width], hg_width, _identity, BF16, name="hgrn_in_i")
            g = hproj([3 * hg_width], hg_width, _silu, F32, name="hgrn_in_g")
            o = _hgrn_recurrence(q, f, v, g, hgrn_out_gain[i // 2], batch, seq)
            y = _matmul(o, hgrn_w_out_bf16, i // 2, F32, name="hgrn_out")
            if ffn_head:
                w_heads = [ffn_w_in[i, :, c : c + ffn_head].astype(BF16) for c in (0, d_ff)]
                h, yn, hidden_head = _postnorm_head(h, y, gains[1], gains[2], w_heads, d_ff, _swiglu, BF16,
                                                    name="postnorm_ffn_head")
            else:
                h, yn = _postnorm(h, y, gains[1], gains[2])
        hidden = _proj(yn, ffn_w_in, i, [0, d_ff], d_ff, _swiglu, BF16, bm=2048, bn=FFN_BN, name="ffn_in",
                       col0=0 if hidden_head is None else ffn_head, out_init=hidden_head)
        y = _matmul(hidden, ffn_w_out_bf16, i, F32, bm=512, bn=512, name="ffn_out")
        yn = q_head = hidden_head = None
        if i + 1 < depth and (i + 1) % 2 == 1:
            if hg_head:
                w_heads = [hgrn_w_in[(i + 1) // 2, :, :hg_head].astype(BF16)]
                h, yn, q_head = _postnorm_head(h, y, gains[3], norm_gains[i + 1, 0], w_heads, hg_width, _hgrn_query,
                                               F32, name="postnorm_hgrn_head")
            else:
                h, yn = _postnorm(h, y, gains[3], norm_gains[i + 1, 0])
        else:
            h = _postnorm(h, y, gains[3], None)
    return h.reshape(batch, seq, d)
```

```python
import functools
import math

import numpy as np
import jax
import jax.numpy as jnp
from jax import lax
from jax.experimental import pallas as pl
from jax.experimental.pallas import tpu as pltpu

RMS_EPS = 1e-6
NEG_INF = -1e30

DILATED_GROUPS = ((128, 1), (512, 4), (2048, 16))
HEAD_DIM = 128
ATTN_BLOCK = 128
NUM_BUCKETS = 32
MAX_EXACT = 16
REL_MAX_DISTANCE = 2048
HG_CHUNK = 64
PROJ_ROW_CHUNKS = 4
PROJ_BN = 1024
FFN_BN = 256
FFN_HEAD_COLS = 768
HGRN_HEAD_COLS = 1024

V7X_VMEM_BYTES = 64 * 1024 * 1024
VMEM_LIMIT = V7X_VMEM_BYTES - 8 * 1024 * 1024

BF16 = jnp.bfloat16
F32 = jnp.float32


def _params(semantics, vmem=VMEM_LIMIT):
    return pltpu.CompilerParams(dimension_semantics=semantics, vmem_limit_bytes=vmem)


def _tile(n, pref):
    if n <= pref:
        return n
    t = pref - pref % 128
    while t >= 128:
        if n % t == 0:
            return t
        t -= 128
    raise ValueError(f"no 128-multiple tile of {n} below {pref}")


def _dot(a, b):
    return jnp.dot(a, b, preferred_element_type=F32)


def _dot_nt(a, b):
    return lax.dot_general(a, b, (((1,), (1,)), ((), ())), preferred_element_type=F32)


def _dot_tn(a, b):
    return lax.dot_general(a, b, (((0,), (0,)), ((), ())), preferred_element_type=F32)


def _rms(x, gain):
    return x * lax.rsqrt(jnp.mean(x * x, axis=-1, keepdims=True) + RMS_EPS) * gain


def _silu(x):
    return x * jax.nn.sigmoid(x)


def _strided_rows(r, count, stride):
    return pl.ds(r, count) if stride == 1 else pl.ds(r, count, stride=stride)


def _prenorm_kernel(x_ref, g_ref, o_ref):
    o_ref[...] = _rms(x_ref[...], g_ref[...]).astype(o_ref.dtype)


def _prenorm(x, gain, tr=256):
    m, d = x.shape
    tr = _tile(m, tr)
    return pl.pallas_call(
        _prenorm_kernel,
        grid=(m // tr,),
        in_specs=[pl.BlockSpec((tr, d), lambda i: (i, 0)), pl.BlockSpec((1, d), lambda i: (0, 0))],
        out_specs=pl.BlockSpec((tr, d), lambda i: (i, 0)),
        out_shape=jax.ShapeDtypeStruct((m, d), BF16),
        compiler_params=_params(("parallel",)),
        name="prenorm",
    )(x, gain.reshape(1, d))


def _prenorm_dilated_kernel(x_ref, g_ref, *refs, dilations):
    out_refs, y_ref = refs[:-1], refs[-1]
    y = _rms(x_ref[...], g_ref[...])
    tr = y.shape[0]
    for c in range(y_ref.shape[0]):
        cols = slice(c * 128, (c + 1) * 128)
        y_ref[c] = y[:, cols]
        for o_ref, d in zip(out_refs, dilations):
            for r in range(d):
                o_ref[r, :, cols] = y_ref[c, _strided_rows(r, tr // d, d), :].astype(o_ref.dtype)


def _prenorm_dilated(x, gain, batch, seq, dilations, tr=256):
    m, d = x.shape
    tr = _tile(seq, tr)
    nt = seq // tr
    outs = pl.pallas_call(
        functools.partial(_prenorm_dilated_kernel, dilations=dilations),
        grid=(m // tr,),
        in_specs=[pl.BlockSpec((tr, d), lambda i: (i, 0)), pl.BlockSpec((1, d), lambda i: (0, 0))],
        out_specs=[pl.BlockSpec((None, dil, tr // dil, d), lambda i: (i // nt, 0, i % nt, 0)) for dil in dilations],
        out_shape=[jax.ShapeDtypeStruct((batch, dil, seq // dil, d), BF16) for dil in dilations],
        scratch_shapes=[pltpu.VMEM((d // 128, tr, 128), F32)],
        compiler_params=_params(("parallel",)),
        name="prenorm_dilated",
    )(x, gain.reshape(1, d))
    return [o.reshape(m, d) for o in outs]


def _postnorm_kernel(h_ref, y_ref, ga_ref, gb_ref, h_out_ref, yn_ref):
    h = h_ref[...] + _rms(y_ref[...], ga_ref[...])
    h_out_ref[...] = h
    yn_ref[...] = _rms(h, gb_ref[...]).astype(yn_ref.dtype)


def _postnorm_last_kernel(h_ref, y_ref, ga_ref, h_out_ref):
    h_out_ref[...] = h_ref[...] + _rms(y_ref[...], ga_ref[...])


def _postnorm(h, y, gain_post, gain_next, tr=256):
    m, d = h.shape
    tr = _tile(m, tr)
    row = pl.BlockSpec((tr, d), lambda i: (i, 0))
    vec = pl.BlockSpec((1, d), lambda i: (0, 0))
    if gain_next is None:
        return pl.pallas_call(
            _postnorm_last_kernel,
            grid=(m // tr,),
            in_specs=[row, row, vec],
            out_specs=row,
            out_shape=jax.ShapeDtypeStruct((m, d), F32),
            compiler_params=_params(("parallel",)),
            name="postnorm_last",
        )(h, y, gain_post.reshape(1, d))
    return pl.pallas_call(
        _postnorm_kernel,
        grid=(m // tr,),
        in_specs=[row, row, vec, vec],
        out_specs=[row, row],
        out_shape=[jax.ShapeDtypeStruct((m, d), F32), jax.ShapeDtypeStruct((m, d), BF16)],
        compiler_params=_params(("parallel",)),
        name="postnorm",
    )(h, y, gain_post.reshape(1, d), gain_next.reshape(1, d))


def _postnorm_head_kernel(h_ref, y_ref, ga_ref, gb_ref, *refs, n_w, epilogue):
    w_refs = refs[:n_w]
    h_out_ref, yn_ref, head_ref = refs[n_w:]
    h = h_ref[...] + _rms(y_ref[...], ga_ref[...])
    h_out_ref[...] = h
    yn = _rms(h, gb_ref[...]).astype(yn_ref.dtype)
    yn_ref[...] = yn
    head_ref[...] = epilogue(*[_dot(yn, w_ref[...]) for w_ref in w_refs]).astype(head_ref.dtype)


def _postnorm_head(h, y, gain_post, gain_next, w_heads, width, epilogue, out_dtype, tr=256, name="postnorm_head"):
    m, d = h.shape
    tr = _tile(m, tr)
    head_cols = w_heads[0].shape[1]
    row = pl.BlockSpec((tr, d), lambda i: (i, 0))
    vec = pl.BlockSpec((1, d), lambda i: (0, 0))
    w_spec = pl.BlockSpec((d, head_cols), lambda i: (0, 0), pipeline_mode=pl.Buffered(1))
    return pl.pallas_call(
        functools.partial(_postnorm_head_kernel, n_w=len(w_heads), epilogue=epilogue),
        grid=(m // tr,),
        in_specs=[row, row, vec, vec] + [w_spec] * len(w_heads),
        out_specs=[row, row, pl.BlockSpec((tr, head_cols), lambda i: (i, 0))],
        out_shape=[jax.ShapeDtypeStruct((m, d), F32), jax.ShapeDtypeStruct((m, d), BF16),
                   jax.ShapeDtypeStruct((m, width), out_dtype)],
        compiler_params=_params(("parallel",)),
        name=name,
    )(h, y, gain_post.reshape(1, d), gain_next.reshape(1, d), *w_heads)


def _matmul_kernel(x_ref, w_ref, o_ref):
    o_ref[...] = _dot(x_ref[...], w_ref[...]).astype(o_ref.dtype)


def _matmul(x, w, layer, out_dtype, bm=1024, bn=1024, name="matmul"):
    m, k = x.shape
    n = w.shape[2]
    bm, bn = _tile(m, bm), _tile(n, bn)
    return pl.pallas_call(
        _matmul_kernel,
        grid=(m // bm, n // bn),
        in_specs=[pl.BlockSpec((bm, k), lambda i, j: (i, 0)), pl.BlockSpec((None, k, bn), lambda i, j: (layer, 0, j))],
        out_specs=pl.BlockSpec((bm, bn), lambda i, j: (i, j)),
        out_shape=jax.ShapeDtypeStruct((m, n), out_dtype),
        compiler_params=_params(("parallel", "arbitrary")),
        name=name,
    )(x, w)


def _proj_kernel(*refs, n_w, has_aux, has_init, epilogue):
    x_ref, w_refs = refs[0], refs[1 : 1 + n_w]
    aux_ref = refs[1 + n_w] if has_aux else None
    n_in = 1 + n_w + has_aux + has_init
    o_ref = refs[n_in]
    wb_refs = refs[n_in + 1 :]
    jj, i = pl.program_id(0), pl.program_id(1)
    slab = w_refs[0].shape[0]

    @pl.when(jj < pl.num_programs(0) - 1)
    def _():
        rows = pl.ds(pl.multiple_of(i * slab, slab), slab)
        for w_ref, wb_ref in zip(w_refs, wb_refs):
            wb_ref[jj % 2, rows, :] = w_ref[...].astype(wb_ref.dtype)

    @pl.when(jj > 0)
    def _():
        chunk = x_ref.shape[0] // PROJ_ROW_CHUNKS
        for r in range(PROJ_ROW_CHUNKS):
            rows = slice(r * chunk, (r + 1) * chunk)
            x = x_ref[rows, :]
            accs = [_dot(x, wb_ref[(jj - 1) % 2]) for wb_ref in wb_refs]
            out = epilogue(*accs, aux_ref[...]) if has_aux else epilogue(*accs)
            o_ref[rows, :] = out.astype(o_ref.dtype)


def _proj(x, w, layer, col_offsets, width, epilogue, out_dtype, aux=None, bm=1024, bn=1024, name="proj", col0=0,
          out_init=None):
    m, k = x.shape
    bm, bn = _tile(m, bm), _tile(width, bn)
    n_w = len(col_offsets)
    nj, ni = (width - col0) // bn, m // bm
    slab = k // ni
    assert k % ni == 0 and slab % 16 == 0, (k, ni)
    assert all(c % bn == 0 for c in col_offsets) and col0 % bn == 0, (col_offsets, col0, bn)

    def w_map(jj, i, c):
        return (layer, jnp.where(jj < nj, i, ni - 1), jnp.minimum(jj, nj - 1) + c + col0 // bn)

    def row_map(jj, i):
        return jnp.where(jj > 0, i, 0)

    def col_map(jj):
        return jnp.maximum(jj - 1, 0) + col0 // bn

    w_specs = [pl.BlockSpec((None, slab, bn), functools.partial(w_map, c=c // bn)) for c in col_offsets]
    aux_specs = [] if aux is None else [pl.BlockSpec((aux.shape[0], bn), lambda jj, i: (0, col_map(jj)))]
    init_specs = [] if out_init is None else [pl.BlockSpec(memory_space=pl.ANY)]
    n_in = 1 + n_w + len(aux_specs) + len(init_specs)
    return pl.pallas_call(
        functools.partial(_proj_kernel, n_w=n_w, has_aux=aux is not None, has_init=out_init is not None,
                          epilogue=epilogue),
        grid=(nj + 1, ni),
        in_specs=[pl.BlockSpec((bm, k), lambda jj, i: (row_map(jj, i), 0))] + w_specs + aux_specs + init_specs,
        out_specs=pl.BlockSpec((bm, bn), lambda jj, i: (row_map(jj, i), col_map(jj))),
        out_shape=jax.ShapeDtypeStruct((m, width), out_dtype),
        scratch_shapes=[pltpu.VMEM((2, k, bn), BF16) for _ in range(n_w)],
        input_output_aliases={} if out_init is None else {n_in - 1: 0},
        compiler_params=_params(("arbitrary", "arbitrary")),
        name=name,
    )(x, *([w] * n_w), *([] if aux is None else [aux]), *([] if out_init is None else [out_init]))


def _identity(acc):
    return acc


def _swiglu(gate, up):
    return _silu(gate) * up


def _t5_bucket_np(dist):
    dist = np.maximum(dist, 0)
    log_ratio = np.log(np.maximum(dist, 1).astype(np.float64) / MAX_EXACT) / math.log(REL_MAX_DISTANCE / MAX_EXACT)
    large = np.minimum(MAX_EXACT + (log_ratio * (NUM_BUCKETS - MAX_EXACT)).astype(np.int64), NUM_BUCKETS - 1)
    return np.where(dist < MAX_EXACT, dist, large)


def _bucket_tables():
    qi = np.arange(ATTN_BLOCK)[:, None]
    ki = np.arange(2 * ATTN_BLOCK)[None, :]
    rel = qi - ki + ATTN_BLOCK
    tables = []
    for window, dilation in DILATED_GROUPS:
        band = (rel >= 0) & (rel <= window // dilation)
        tables.append(np.where(band, _t5_bucket_np(rel * dilation), -1))
    return np.stack(tables).astype(np.int32)


def _bias_kernel(rb_ref, bucket_ref, o_ref, *, heads):
    g, h = pl.program_id(0), pl.program_id(1)
    bucket = bucket_ref[...]
    acc = jnp.full(bucket.shape, NEG_INF, F32)
    for b in range(NUM_BUCKETS):
        acc = jnp.where(bucket == b, rb_ref[b, g * heads + h], acc)
    o_ref[...] = acc


def _bias_tables(rel_bias, heads):
    ng = len(DILATED_GROUPS)
    buckets = jnp.asarray(_bucket_tables())
    return pl.pallas_call(
        functools.partial(_bias_kernel, heads=heads),
        grid=(ng, heads),
        in_specs=[
            pl.BlockSpec(memory_space=pltpu.SMEM),
            pl.BlockSpec((None, ATTN_BLOCK, 2 * ATTN_BLOCK), lambda g, h: (g, 0, 0)),
        ],
        out_specs=pl.BlockSpec((None, None, ATTN_BLOCK, 2 * ATTN_BLOCK), lambda g, h: (g, h, 0, 0)),
        out_shape=jax.ShapeDtypeStruct((ng, heads, ATTN_BLOCK, 2 * ATTN_BLOCK), F32),
        compiler_params=_params(("arbitrary", "arbitrary")),
        name="attn_bias",
    )(rel_bias, buckets)


def _attn_kernel(q_ref, kc_ref, kp_ref, vc_ref, vp_ref, bias_ref, o_ref, lse_ref, *, heads):
    has_prev = pl.program_id(1) > 0
    scale = HEAD_DIM ** -0.5
    lane = lax.broadcasted_iota(jnp.int32, (ATTN_BLOCK, ATTN_BLOCK), 1)
    lse_all = jnp.zeros((ATTN_BLOCK, ATTN_BLOCK), F32)
    ones = jnp.ones((ATTN_BLOCK, HEAD_DIM), BF16)
    for h in range(heads):
        cols = slice(h * HEAD_DIM, (h + 1) * HEAD_DIM)
        q = q_ref[:, cols]
        s_prev = _dot_nt(q, kp_ref[:, cols]) * scale + bias_ref[h, :, :ATTN_BLOCK]
        s_prev = jnp.where(has_prev, s_prev, NEG_INF)
        s_cur = _dot_nt(q, kc_ref[:, cols]) * scale + bias_ref[h, :, ATTN_BLOCK:]
        m = jnp.maximum(jnp.max(s_prev, axis=-1, keepdims=True), jnp.max(s_cur, axis=-1, keepdims=True))
        p_prev = jnp.exp(s_prev - m).astype(BF16)
        p_cur = jnp.exp(s_cur - m).astype(BF16)
        ov = (_dot(p_prev, jnp.concatenate([vp_ref[:, cols], ones], axis=1))
              + _dot(p_cur, jnp.concatenate([vc_ref[:, cols], ones], axis=1)))
        denom = ov[:, HEAD_DIM:]
        o_ref[:, cols] = (ov[:, :HEAD_DIM] / denom).astype(o_ref.dtype)
        lse_all = jnp.where(lane == h, m + jnp.log(denom[:, 0:1]), lse_all)
    lse_ref[...] = lse_all


def _attn_group(proj, bias_g, sub_len, heads):
    rows = proj.shape[0]
    width = heads * HEAD_DIM
    nb = sub_len // ATTN_BLOCK
    blk = (ATTN_BLOCK, width)

    def cur(which):
        return lambda s, i: (s * nb + i, which)

    def prev(which):
        return lambda s, i: (s * nb + jnp.maximum(i - 1, 0), which)

    return pl.pallas_call(
        functools.partial(_attn_kernel, heads=heads),
        grid=(rows // sub_len, nb),
        in_specs=[
            pl.BlockSpec(blk, cur(0)),
            pl.BlockSpec(blk, cur(1)),
            pl.BlockSpec(blk, prev(1)),
            pl.BlockSpec(blk, cur(2)),
            pl.BlockSpec(blk, prev(2)),
            pl.BlockSpec((heads, ATTN_BLOCK, 2 * ATTN_BLOCK), lambda s, i: (0, 0, 0)),
        ],
        out_specs=[
            pl.BlockSpec(blk, lambda s, i: (s * nb + i, 0)),
            pl.BlockSpec((ATTN_BLOCK, ATTN_BLOCK), lambda s, i: (s * nb + i, 0)),
        ],
        out_shape=[jax.ShapeDtypeStruct((rows, width), BF16), jax.ShapeDtypeStruct((rows, ATTN_BLOCK), F32)],
        compiler_params=_params(("parallel", "arbitrary")),
        name="attn_group",
    )(proj, proj, proj, proj, proj, bias_g)


def _merge_outproj_kernel(*refs, heads, dilations):
    ng = len(dilations)
    o_refs, l_refs = refs[:ng], refs[ng : 2 * ng]
    w_ref, h_ref, ga_ref, gb_ref, h_out_ref, yn_ref, merged_even_ref, merged_odd_ref, alpha_ref, acc_ref = refs[2 * ng :]
    bm = acc_ref.shape[1]
    ii = pl.program_id(0)

    @pl.when(ii == 0)
    def _():
        merged_odd_ref[...] = jnp.zeros_like(merged_odd_ref)

    def step(merge_into_ref, multiply_ref):
        for g, d in enumerate(dilations):
            for r in range(d):
                alpha_ref[g, _strided_rows(r, bm // d, d), :] = l_refs[g][r]
        lses = [alpha_ref[g] for g in range(ng)]
        mx = functools.reduce(jnp.maximum, lses)
        es = [jnp.exp(l - mx) for l in lses]
        inv = 1.0 / functools.reduce(jnp.add, es)
        for g in range(ng):
            alpha_ref[g] = es[g] * inv
        for g, d in enumerate(dilations):
            for r in range(d):
                rows = _strided_rows(r, bm // d, d)
                alpha = alpha_ref[g, rows, :]
                for h in range(heads):
                    cols = slice(h * HEAD_DIM, (h + 1) * HEAD_DIM)
                    term = alpha[:, h : h + 1] * o_refs[g][r, :, cols].astype(F32)
                    if g == 0:
                        acc_ref[h, rows, :] = term
                    else:
                        acc_ref[h, rows, :] += term
        for h in range(heads):
            merge_into_ref[:, h * HEAD_DIM : (h + 1) * HEAD_DIM] = acc_ref[h].astype(merge_into_ref.dtype)

        y = _dot(multiply_ref[...], w_ref[...])
        h_new = h_ref[...] + _rms(y, ga_ref[...])
        h_out_ref[...] = h_new
        yn_ref[...] = _rms(h_new, gb_ref[...]).astype(yn_ref.dtype)

    pl.when(ii % 2 == 0)(functools.partial(step, merged_even_ref, merged_odd_ref))
    pl.when(ii % 2 == 1)(functools.partial(step, merged_odd_ref, merged_even_ref))


def _merge_outproj(outs, lses, w_out, h, gain_post, gain_next, heads, batch, seq, dilations, bm=256):
    width = heads * HEAD_DIM
    n = w_out.shape[1]
    bm = _tile(seq, bm)
    nt = seq // bm
    tiles = batch * nt

    def in_map(ii):
        t = jnp.minimum(ii, tiles - 1)
        return (t // nt, 0, t % nt, 0)

    row = pl.BlockSpec((bm, n), lambda ii: (jnp.maximum(ii - 1, 0), 0))
    vec = pl.BlockSpec((1, n), lambda ii: (0, 0))
    o_specs = [pl.BlockSpec((None, d, bm // d, width), in_map) for d in dilations]
    l_specs = [pl.BlockSpec((None, d, bm // d, ATTN_BLOCK), in_map) for d in dilations]
    outs = [o.reshape(batch, d, seq // d, width) for o, d in zip(outs, dilations)]
    lses = [l.reshape(batch, d, seq // d, ATTN_BLOCK) for l, d in zip(lses, dilations)]
    return pl.pallas_call(
        functools.partial(_merge_outproj_kernel, heads=heads, dilations=dilations),
        grid=(tiles + 1,),
        in_specs=o_specs + l_specs + [pl.BlockSpec((width, n), lambda ii: (0, 0), pipeline_mode=pl.Buffered(1)),
                                      row, vec, vec],
        out_specs=[row, row],
        out_shape=[jax.ShapeDtypeStruct((batch * seq, n), F32), jax.ShapeDtypeStruct((batch * seq, n), BF16)],
        scratch_shapes=[
            pltpu.VMEM((bm, width), BF16),
            pltpu.VMEM((bm, width), BF16),
            pltpu.VMEM((len(dilations), bm, ATTN_BLOCK), F32),
            pltpu.VMEM((heads, bm, HEAD_DIM), F32),
        ],
        compiler_params=_params(("arbitrary",)),
        name="attn_merge_outproj",
    )(*outs, *lses, w_out, h, gain_post.reshape(1, n), gain_next.reshape(1, n))


def _hgrn_query(acc):
    return _silu(acc) * (HEAD_DIM ** -0.5)


def _hgrn_forget(acc, lb_logits, *, layer):
    p = jnp.exp(lb_logits - jnp.max(lb_logits, axis=0, keepdims=True))
    p = p / jnp.sum(p, axis=0, keepdims=True)
    lb = jnp.sum(p[: layer + 1], axis=0, keepdims=True) - p[0:1]
    return lb + (1.0 - lb) * jax.nn.sigmoid(acc)


def _hgrn_rec_kernel(q_ref, f_ref, v_ref, g_ref, gain_ref, o_ref, state_ref, b_ref, *, heads_per_step, chunks):
    @pl.when(pl.program_id(2) == 0)
    def _():
        state_ref[...] = jnp.zeros_like(state_ref)

    row = lax.broadcasted_iota(jnp.int32, (HG_CHUNK, HG_CHUNK), 0)
    col = lax.broadcasted_iota(jnp.int32, (HG_CHUNK, HG_CHUNK), 1)
    causal = row >= col
    tri = causal.astype(BF16)
    tri2 = jnp.concatenate([tri, tri], axis=1)
    mid = HG_CHUNK // 2
    gain = gain_ref[...]
    heads = range(heads_per_step)
    cols = [slice(h * HEAD_DIM, (h + 1) * HEAD_DIM) for h in heads]

    for c in range(chunks):
        rows = pl.ds(c * HG_CHUNK, HG_CHUNK)
        log_f = jnp.log(f_ref[rows, :])
        hi = log_f.astype(BF16)
        lo = (log_f - hi.astype(F32)).astype(BF16)
        b_ref[rows, :] = _dot(tri2, jnp.concatenate([hi, lo], axis=0))

    def chunk(c, carry):
        rows = pl.ds(pl.multiple_of(c * HG_CHUNK, HG_CHUNK), HG_CHUNK)
        b = [b_ref[rows, cols[h]] for h in heads]
        b_mid = [b[h][mid - 1 : mid, :] for h in heads]
        b_last = [b[h][HG_CHUNK - 1 :, :] for h in heads]
        key = [1.0 - f_ref[rows, cols[h]] for h in heads]
        q = [q_ref[rows, cols[h]] for h in heads]
        v = [v_ref[rows, cols[h]] for h in heads]
        scores = [_dot_nt((q[h] * jnp.exp(b[h] - b_mid[h])).astype(BF16),
                          (key[h] * jnp.exp(b_mid[h] - b[h])).astype(BF16)) for h in heads]
        probs = [jnp.where(causal, scores[h], 0.0).astype(BF16) for h in heads]
        state = [state_ref[h] for h in heads]
        o = [_dot(jnp.concatenate([(q[h] * jnp.exp(b[h])).astype(BF16), probs[h]], axis=1),
                  jnp.concatenate([state[h].astype(BF16), v[h]], axis=0)) for h in heads]
        update = [_dot_tn((key[h] * jnp.exp(b_last[h] - b[h])).astype(BF16), v[h]) for h in heads]
        for h in heads:
            decay = jnp.transpose(jnp.broadcast_to(jnp.exp(b_last[h]), (8, HEAD_DIM)))[:, 0:1]
            state_ref[h] = state[h] * decay + update[h]
        for h in heads:
            o_ref[rows, cols[h]] = (_rms(o[h], gain) * g_ref[rows, cols[h]]).astype(o_ref.dtype)
        return carry

    lax.fori_loop(0, chunks, chunk, 0)


def _hgrn_recurrence(q, f, v, g, out_gain, batch, seq, heads_per_step=8, ts=512):
    m, width = q.shape
    heads = width // HEAD_DIM
    heads_per_step = math.gcd(heads, heads_per_step)
    ts = _tile(seq, ts)
    bw = heads_per_step * HEAD_DIM
    nt = seq // ts
    spec = pl.BlockSpec((ts, bw), lambda b, hg, t: (b * nt + t, hg))
    return pl.pallas_call(
        functools.partial(_hgrn_rec_kernel, heads_per_step=heads_per_step, chunks=ts // HG_CHUNK),
        grid=(batch, heads // heads_per_step, nt),
        in_specs=[spec, spec, spec, spec, pl.BlockSpec((1, HEAD_DIM), lambda b, hg, t: (0, 0))],
        out_specs=spec,
        out_shape=jax.ShapeDtypeStruct((m, width), BF16),
        scratch_shapes=[pltpu.VMEM((heads_per_step, HEAD_DIM, HEAD_DIM), F32), pltpu.VMEM((ts, bw), F32)],
        compiler_params=_params(("parallel", "parallel", "arbitrary")),
        name="hgrn_recurrence",
    )(q, f, v, g, out_gain.reshape(1, HEAD_DIM))


def _cast_slice(w_slice):
    return lax.optimization_barrier(w_slice).astype(BF16)


def _head_cols(width, bn, most):
    return min(most // bn, width // bn // 2) * bn


def kernel(x, norm_gains, rel_bias, attn_w_in, attn_w_out, hgrn_w_in, hgrn_lb_logits, hgrn_out_gain, hgrn_w_out,
           ffn_w_in, ffn_w_out):
    batch, seq, d = x.shape
    m = batch * seq
    depth = norm_gains.shape[0]
    attn_heads = attn_w_out.shape[1] // HEAD_DIM
    attn_width = attn_heads * HEAD_DIM
    dilations = tuple(dil for _, dil in DILATED_GROUPS)
    d_ff = ffn_w_out.shape[1]
    hg_width = hgrn_w_in.shape[2] // 4
    ffn_head = _head_cols(d_ff, FFN_BN, FFN_HEAD_COLS)
    hg_bn = _tile(hg_width, PROJ_BN)
    hg_head = _head_cols(hg_width, hg_bn, HGRN_HEAD_COLS)
    h = x.reshape(m, d)
    ffn_w_out_bf16 = ffn_w_out.astype(BF16)
    hgrn_w_out_bf16 = hgrn_w_out.astype(BF16)
    yn = q_head = hidden_head = None
    bias = None
    for i in range(depth):
        gains = norm_gains[i]
        if i % 2 == 0:
            if bias is None:
                bias = _bias_tables(rel_bias, attn_heads)
            yns = _prenorm_dilated(h, gains[0], batch, seq, dilations)
            outs, lses = [], []
            for g, dil in enumerate(dilations):
                proj = _proj(yns[g], attn_w_in, i // 2, [g * 3 * attn_width], 3 * attn_width, _identity, BF16,
                             bn=PROJ_BN, name="attn_in")
                o, l = _attn_group(proj, bias[g], seq // dil, attn_heads)
                outs.append(o)
                lses.append(l)
            h, yn = _merge_outproj(outs, lses, attn_w_out[i // 2].astype(BF16), h, gains[1], gains[2], attn_heads,
                                   batch, seq, dilations)
        else:
            if yn is None:
                yn = _prenorm(h, gains[0])
            hproj = functools.partial(_proj, yn, hgrn_w_in, i // 2, bn=PROJ_BN)
            q = hproj([0], hg_width, _hgrn_query, F32, name="hgrn_in_q", col0=0 if q_head is None else hg_head,
                      out_init=q_head)
            f = hproj([hg_width], hg_width, functools.partial(_hgrn_forget, layer=i), F32, aux=hgrn_lb_logits,
                      name="hgrn_in_f")
            v = hproj([2 * hg_width], hg_width, _identity, BF16, name="hgrn_in_i")
            g = hproj([3 * hg_width], hg_width, _silu, F32, name="hgrn_in_g")
            o = _hgrn_recurrence(q, f, v, g, hgrn_out_gain[i // 2], batch, seq)
            y = _matmul(o, hgrn_w_out_bf16, i // 2, F32, name="hgrn_out")
            if ffn_head:
                w_heads = [_cast_slice(ffn_w_in[i, :, c : c + ffn_head]) for c in (0, d_ff)]
                h, yn, hidden_head = _postnorm_head(h, y, gains[1], gains[2], w_heads, d_ff, _swiglu, BF16,
                                                    name="postnorm_ffn_head")
            else:
                h, yn = _postnorm(h, y, gains[1], gains[2])
        hidden = _proj(yn, ffn_w_in, i, [0, d_ff], d_ff, _swiglu, BF16, bm=2048, bn=FFN_BN, name="ffn_in",
                       col0=0 if hidden_head is None else ffn_head, out_init=hidden_head)
        y = _matmul(hidden, ffn_w_out_bf16, i, F32, bm=512, bn=512, name="ffn_out")
        yn = q_head = hidden_head = None
        if i + 1 < depth and (i + 1) % 2 == 1:
            if hg_head:
                w_heads = [_cast_slice(hgrn_w_in[(i + 1) // 2, :, :hg_head])]
                h, yn, q_head = _postnorm_head(h, y, gains[3], norm_gains[i + 1, 0], w_heads, hg_width, _hgrn_query,
                                               F32, name="postnorm_hgrn_head")
            else:
                h, yn = _postnorm(h, y, gains[3], norm_gains[i + 1, 0])
        else:
            h = _postnorm(h, y, gains[3], None)
    return h.reshape(batch, seq, d)
```

```python
import functools
import math

import numpy as np
import jax
import jax.numpy as jnp
from jax import lax
from jax.experimental import pallas as pl
from jax.experimental.pallas import tpu as pltpu

RMS_EPS = 1e-6
NEG_INF = -1e30

DILATED_GROUPS = ((128, 1), (512, 4), (2048, 16))
HEAD_DIM = 128
ATTN_BLOCK = 128
NUM_BUCKETS = 32
MAX_EXACT = 16
REL_MAX_DISTANCE = 2048
HG_CHUNK = 64
PROJ_ROW_CHUNKS = 4
PROJ_BN = 1024
FFN_BN = 256
FFN_HEAD_COLS = 768
HGRN_HEAD_COLS = 1024
ATTN_HEAD_COLS = 1024

V7X_VMEM_BYTES = 64 * 1024 * 1024
VMEM_LIMIT = V7X_VMEM_BYTES - 8 * 1024 * 1024

BF16 = jnp.bfloat16
F32 = jnp.float32


def _params(semantics, vmem=VMEM_LIMIT):
    return pltpu.CompilerParams(dimension_semantics=semantics, vmem_limit_bytes=vmem)


def _tile(n, pref):
    if n <= pref:
        return n
    t = pref - pref % 128
    while t >= 128:
        if n % t == 0:
            return t
        t -= 128
    raise ValueError(f"no 128-multiple tile of {n} below {pref}")


def _dot(a, b):
    return jnp.dot(a, b, preferred_element_type=F32)


def _dot_nt(a, b):
    return lax.dot_general(a, b, (((1,), (1,)), ((), ())), preferred_element_type=F32)


def _dot_tn(a, b):
    return lax.dot_general(a, b, (((0,), (0,)), ((), ())), preferred_element_type=F32)


def _rms(x, gain):
    return x * lax.rsqrt(jnp.mean(x * x, axis=-1, keepdims=True) + RMS_EPS) * gain


def _silu(x):
    return x * jax.nn.sigmoid(x)


def _strided_rows(r, count, stride):
    return pl.ds(r, count) if stride == 1 else pl.ds(r, count, stride=stride)


def _prenorm_kernel(x_ref, g_ref, o_ref):
    o_ref[...] = _rms(x_ref[...], g_ref[...]).astype(o_ref.dtype)


def _prenorm(x, gain, tr=256):
    m, d = x.shape
    tr = _tile(m, tr)
    return pl.pallas_call(
        _prenorm_kernel,
        grid=(m // tr,),
        in_specs=[pl.BlockSpec((tr, d), lambda i: (i, 0)), pl.BlockSpec((1, d), lambda i: (0, 0))],
        out_specs=pl.BlockSpec((tr, d), lambda i: (i, 0)),
        out_shape=jax.ShapeDtypeStruct((m, d), BF16),
        compiler_params=_params(("parallel",)),
        name="prenorm",
    )(x, gain.reshape(1, d))


def _prenorm_dilated_kernel(x_ref, g_ref, *refs, dilations, has_head):
    w_ref = refs[0] if has_head else None
    out_refs = refs[has_head : has_head + len(dilations)]
    head_ref = refs[has_head + len(dilations)] if has_head else None
    y_ref = refs[-1]
    y = _rms(x_ref[...], g_ref[...])
    tr = y.shape[0]
    if has_head:
        head_ref[...] = _dot(y.astype(BF16), w_ref[...]).astype(head_ref.dtype)
    for c in range(y_ref.shape[0]):
        cols = slice(c * 128, (c + 1) * 128)
        y_ref[c] = y[:, cols]
        for o_ref, d in zip(out_refs, dilations):
            for r in range(d):
                o_ref[r, :, cols] = y_ref[c, _strided_rows(r, tr // d, d), :].astype(o_ref.dtype)


def _prenorm_dilated(x, gain, batch, seq, dilations, w_head=None, head_width=None, tr=256):
    m, d = x.shape
    tr = _tile(seq, tr)
    nt = seq // tr
    has_head = w_head is not None
    head_in = [pl.BlockSpec(w_head.shape, lambda i: (0, 0), pipeline_mode=pl.Buffered(1))] if has_head else []
    head_out = [pl.BlockSpec((tr, w_head.shape[1]), lambda i: (i, 0))] if has_head else []
    head_shape = [jax.ShapeDtypeStruct((m, head_width), BF16)] if has_head else []
    outs = pl.pallas_call(
        functools.partial(_prenorm_dilated_kernel, dilations=dilations, has_head=has_head),
        grid=(m // tr,),
        in_specs=[pl.BlockSpec((tr, d), lambda i: (i, 0)), pl.BlockSpec((1, d), lambda i: (0, 0))] + head_in,
        out_specs=[pl.BlockSpec((None, dil, tr // dil, d), lambda i: (i // nt, 0, i % nt, 0)) for dil in dilations]
        + head_out,
        out_shape=[jax.ShapeDtypeStruct((batch, dil, seq // dil, d), BF16) for dil in dilations] + head_shape,
        scratch_shapes=[pltpu.VMEM((d // 128, tr, 128), F32)],
        compiler_params=_params(("parallel",)),
        name="prenorm_dilated",
    )(x, gain.reshape(1, d), *([w_head] if has_head else []))
    return [o.reshape(m, d) for o in outs[: len(dilations)]], (outs[-1] if has_head else None)


def _postnorm_kernel(h_ref, y_ref, ga_ref, gb_ref, h_out_ref, yn_ref):
    h = h_ref[...] + _rms(y_ref[...], ga_ref[...])
    h_out_ref[...] = h
    yn_ref[...] = _rms(h, gb_ref[...]).astype(yn_ref.dtype)


def _postnorm_last_kernel(h_ref, y_ref, ga_ref, h_out_ref):
    h_out_ref[...] = h_ref[...] + _rms(y_ref[...], ga_ref[...])


def _postnorm(h, y, gain_post, gain_next, tr=256):
    m, d = h.shape
    tr = _tile(m, tr)
    row = pl.BlockSpec((tr, d), lambda i: (i, 0))
    vec = pl.BlockSpec((1, d), lambda i: (0, 0))
    if gain_next is None:
        return pl.pallas_call(
            _postnorm_last_kernel,
            grid=(m // tr,),
            in_specs=[row, row, vec],
            out_specs=row,
            out_shape=jax.ShapeDtypeStruct((m, d), F32),
            compiler_params=_params(("parallel",)),
            name="postnorm_last",
        )(h, y, gain_post.reshape(1, d))
    return pl.pallas_call(
        _postnorm_kernel,
        grid=(m // tr,),
        in_specs=[row, row, vec, vec],
        out_specs=[row, row],
        out_shape=[jax.ShapeDtypeStruct((m, d), F32), jax.ShapeDtypeStruct((m, d), BF16)],
        compiler_params=_params(("parallel",)),
        name="postnorm",
    )(h, y, gain_post.reshape(1, d), gain_next.reshape(1, d))


def _postnorm_head_kernel(h_ref, y_ref, ga_ref, gb_ref, *refs, n_w, epilogue):
    w_refs = refs[:n_w]
    h_out_ref, yn_ref, head_ref = refs[n_w:]
    h = h_ref[...] + _rms(y_ref[...], ga_ref[...])
    h_out_ref[...] = h
    yn = _rms(h, gb_ref[...]).astype(yn_ref.dtype)
    yn_ref[...] = yn
    head_ref[...] = epilogue(*[_dot(yn, w_ref[...]) for w_ref in w_refs]).astype(head_ref.dtype)


def _postnorm_head(h, y, gain_post, gain_next, w_heads, width, epilogue, out_dtype, tr=256, name="postnorm_head"):
    m, d = h.shape
    tr = _tile(m, tr)
    head_cols = w_heads[0].shape[1]
    row = pl.BlockSpec((tr, d), lambda i: (i, 0))
    vec = pl.BlockSpec((1, d), lambda i: (0, 0))
    w_spec = pl.BlockSpec((d, head_cols), lambda i: (0, 0), pipeline_mode=pl.Buffered(1))
    return pl.pallas_call(
        functools.partial(_postnorm_head_kernel, n_w=len(w_heads), epilogue=epilogue),
        grid=(m // tr,),
        in_specs=[row, row, vec, vec] + [w_spec] * len(w_heads),
        out_specs=[row, row, pl.BlockSpec((tr, head_cols), lambda i: (i, 0))],
        out_shape=[jax.ShapeDtypeStruct((m, d), F32), jax.ShapeDtypeStruct((m, d), BF16),
                   jax.ShapeDtypeStruct((m, width), out_dtype)],
        compiler_params=_params(("parallel",)),
        name=name,
    )(h, y, gain_post.reshape(1, d), gain_next.reshape(1, d), *w_heads)


def _matmul_kernel(x_ref, w_ref, o_ref):
    o_ref[...] = _dot(x_ref[...], w_ref[...]).astype(o_ref.dtype)


def _matmul(x, w, layer, out_dtype, bm=1024, bn=1024, name="matmul"):
    m, k = x.shape
    n = w.shape[2]
    bm, bn = _tile(m, bm), _tile(n, bn)
    return pl.pallas_call(
        _matmul_kernel,
        grid=(m // bm, n // bn),
        in_specs=[pl.BlockSpec((bm, k), lambda i, j: (i, 0)), pl.BlockSpec((None, k, bn), lambda i, j: (layer, 0, j))],
        out_specs=pl.BlockSpec((bm, bn), lambda i, j: (i, j)),
        out_shape=jax.ShapeDtypeStruct((m, n), out_dtype),
        compiler_params=_params(("parallel", "arbitrary")),
        name=name,
    )(x, w)


def _proj_kernel(*refs, n_w, has_aux, has_init, epilogue):
    x_ref, w_refs = refs[0], refs[1 : 1 + n_w]
    aux_ref = refs[1 + n_w] if has_aux else None
    n_in = 1 + n_w + has_aux + has_init
    o_ref = refs[n_in]
    wb_refs = refs[n_in + 1 :]
    jj, i = pl.program_id(0), pl.program_id(1)
    slab = w_refs[0].shape[0]

    @pl.when(jj < pl.num_programs(0) - 1)
    def _():
        rows = pl.ds(pl.multiple_of(i * slab, slab), slab)
        for w_ref, wb_ref in zip(w_refs, wb_refs):
            wb_ref[jj % 2, rows, :] = w_ref[...].astype(wb_ref.dtype)

    @pl.when(jj > 0)
    def _():
        chunk = x_ref.shape[0] // PROJ_ROW_CHUNKS
        for r in range(PROJ_ROW_CHUNKS):
            rows = slice(r * chunk, (r + 1) * chunk)
            x = x_ref[rows, :]
            accs = [_dot(x, wb_ref[(jj - 1) % 2]) for wb_ref in wb_refs]
            out = epilogue(*accs, aux_ref[...]) if has_aux else epilogue(*accs)
            o_ref[rows, :] = out.astype(o_ref.dtype)


def _proj(x, w, layer, col_offsets, width, epilogue, out_dtype, aux=None, bm=1024, bn=1024, name="proj", col0=0,
          out_init=None):
    m, k = x.shape
    bm, bn = _tile(m, bm), _tile(width, bn)
    n_w = len(col_offsets)
    nj, ni = (width - col0) // bn, m // bm
    slab = k // ni
    assert k % ni == 0 and slab % 16 == 0, (k, ni)
    assert all(c % bn == 0 for c in col_offsets) and col0 % bn == 0, (col_offsets, col0, bn)

    def w_map(jj, i, c):
        return (layer, jnp.where(jj < nj, i, ni - 1), jnp.minimum(jj, nj - 1) + c + col0 // bn)

    def row_map(jj, i):
        return jnp.where(jj > 0, i, 0)

    def col_map(jj):
        return jnp.maximum(jj - 1, 0) + col0 // bn

    w_specs = [pl.BlockSpec((None, slab, bn), functools.partial(w_map, c=c // bn)) for c in col_offsets]
    aux_specs = [] if aux is None else [pl.BlockSpec((aux.shape[0], bn), lambda jj, i: (0, col_map(jj)))]
    init_specs = [] if out_init is None else [pl.BlockSpec(memory_space=pl.ANY)]
    n_in = 1 + n_w + len(aux_specs) + len(init_specs)
    return pl.pallas_call(
        functools.partial(_proj_kernel, n_w=n_w, has_aux=aux is not None, has_init=out_init is not None,
                          epilogue=epilogue),
        grid=(nj + 1, ni),
        in_specs=[pl.BlockSpec((bm, k), lambda jj, i: (row_map(jj, i), 0))] + w_specs + aux_specs + init_specs,
        out_specs=pl.BlockSpec((bm, bn), lambda jj, i: (row_map(jj, i), col_map(jj))),
        out_shape=jax.ShapeDtypeStruct((m, width), out_dtype),
        scratch_shapes=[pltpu.VMEM((2, k, bn), BF16) for _ in range(n_w)],
        input_output_aliases={} if out_init is None else {n_in - 1: 0},
        compiler_params=_params(("arbitrary", "arbitrary")),
        name=name,
    )(x, *([w] * n_w), *([] if aux is None else [aux]), *([] if out_init is None else [out_init]))


def _identity(acc):
    return acc


def _swiglu(gate, up):
    return _silu(gate) * up


def _t5_bucket_np(dist):
    dist = np.maximum(dist, 0)
    log_ratio = np.log(np.maximum(dist, 1).astype(np.float64) / MAX_EXACT) / math.log(REL_MAX_DISTANCE / MAX_EXACT)
    large = np.minimum(MAX_EXACT + (log_ratio * (NUM_BUCKETS - MAX_EXACT)).astype(np.int64), NUM_BUCKETS - 1)
    return np.where(dist < MAX_EXACT, dist, large)


def _bucket_tables():
    qi = np.arange(ATTN_BLOCK)[:, None]
    ki = np.arange(2 * ATTN_BLOCK)[None, :]
    rel = qi - ki + ATTN_BLOCK
    tables = []
    for window, dilation in DILATED_GROUPS:
        band = (rel >= 0) & (rel <= window // dilation)
        tables.append(np.where(band, _t5_bucket_np(rel * dilation), -1))
    return np.stack(tables).astype(np.int32)


def _bias_kernel(rb_ref, bucket_ref, o_ref, *, heads):
    g, h = pl.program_id(0), pl.program_id(1)
    bucket = bucket_ref[...]
    acc = jnp.full(bucket.shape, NEG_INF, F32)
    for b in range(NUM_BUCKETS):
        acc = jnp.where(bucket == b, rb_ref[b, g * heads + h], acc)
    o_ref[...] = acc


def _bias_tables(rel_bias, heads):
    ng = len(DILATED_GROUPS)
    buckets = jnp.asarray(_bucket_tables())
    return pl.pallas_call(
        functools.partial(_bias_kernel, heads=heads),
        grid=(ng, heads),
        in_specs=[
            pl.BlockSpec(memory_space=pltpu.SMEM),
            pl.BlockSpec((None, ATTN_BLOCK, 2 * ATTN_BLOCK), lambda g, h: (g, 0, 0)),
        ],
        out_specs=pl.BlockSpec((None, None, ATTN_BLOCK, 2 * ATTN_BLOCK), lambda g, h: (g, h, 0, 0)),
        out_shape=jax.ShapeDtypeStruct((ng, heads, ATTN_BLOCK, 2 * ATTN_BLOCK), F32),
        compiler_params=_params(("arbitrary", "arbitrary")),
        name="attn_bias",
    )(rel_bias, buckets)


def _attn_kernel(q_ref, kc_ref, kp_ref, vc_ref, vp_ref, bias_ref, o_ref, lse_ref, *, heads):
    has_prev = pl.program_id(1) > 0
    scale = HEAD_DIM ** -0.5
    lane = lax.broadcasted_iota(jnp.int32, (ATTN_BLOCK, ATTN_BLOCK), 1)
    lse_all = jnp.zeros((ATTN_BLOCK, ATTN_BLOCK), F32)
    ones = jnp.ones((ATTN_BLOCK, HEAD_DIM), BF16)
    for h in range(heads):
        cols = slice(h * HEAD_DIM, (h + 1) * HEAD_DIM)
        q = q_ref[:, cols]
        s_prev = _dot_nt(q, kp_ref[:, cols]) * scale + bias_ref[h, :, :ATTN_BLOCK]
        s_prev = jnp.where(has_prev, s_prev, NEG_INF)
        s_cur = _dot_nt(q, kc_ref[:, cols]) * scale + bias_ref[h, :, ATTN_BLOCK:]
        m = jnp.maximum(jnp.max(s_prev, axis=-1, keepdims=True), jnp.max(s_cur, axis=-1, keepdims=True))
        p_prev = jnp.exp(s_prev - m).astype(BF16)
        p_cur = jnp.exp(s_cur - m).astype(BF16)
        ov = (_dot(p_prev, jnp.concatenate([vp_ref[:, cols], ones], axis=1))
              + _dot(p_cur, jnp.concatenate([vc_ref[:, cols], ones], axis=1)))
        denom = ov[:, HEAD_DIM:]
        o_ref[:, cols] = (ov[:, :HEAD_DIM] / denom).astype(o_ref.dtype)
        lse_all = jnp.where(lane == h, m + jnp.log(denom[:, 0:1]), lse_all)
    lse_ref[...] = lse_all


def _attn_group(proj, bias_g, sub_len, heads):
    rows = proj.shape[0]
    width = heads * HEAD_DIM
    nb = sub_len // ATTN_BLOCK
    blk = (ATTN_BLOCK, width)

    def cur(which):
        return lambda s, i: (s * nb + i, which)

    def prev(which):
        return lambda s, i: (s * nb + jnp.maximum(i - 1, 0), which)

    return pl.pallas_call(
        functools.partial(_attn_kernel, heads=heads),
        grid=(rows // sub_len, nb),
        in_specs=[
            pl.BlockSpec(blk, cur(0)),
            pl.BlockSpec(blk, cur(1)),
            pl.BlockSpec(blk, prev(1)),
            pl.BlockSpec(blk, cur(2)),
            pl.BlockSpec(blk, prev(2)),
            pl.BlockSpec((heads, ATTN_BLOCK, 2 * ATTN_BLOCK), lambda s, i: (0, 0, 0)),
        ],
        out_specs=[
            pl.BlockSpec(blk, lambda s, i: (s * nb + i, 0)),
            pl.BlockSpec((ATTN_BLOCK, ATTN_BLOCK), lambda s, i: (s * nb + i, 0)),
        ],
        out_shape=[jax.ShapeDtypeStruct((rows, width), BF16), jax.ShapeDtypeStruct((rows, ATTN_BLOCK), F32)],
        compiler_params=_params(("parallel", "arbitrary")),
        name="attn_group",
    )(proj, proj, proj, proj, proj, bias_g)


def _merge_outproj_kernel(*refs, heads, dilations):
    ng = len(dilations)
    o_refs, l_refs = refs[:ng], refs[ng : 2 * ng]
    w_ref, h_ref, ga_ref, gb_ref, h_out_ref, yn_ref, merged_even_ref, merged_odd_ref, alpha_ref, acc_ref = refs[2 * ng :]
    bm = acc_ref.shape[1]
    ii = pl.program_id(0)

    @pl.when(ii == 0)
    def _():
        merged_odd_ref[...] = jnp.zeros_like(merged_odd_ref)

    def step(merge_into_ref, multiply_ref):
        for g, d in enumerate(dilations):
            for r in range(d):
                alpha_ref[g, _strided_rows(r, bm // d, d), :] = l_refs[g][r]
        lses = [alpha_ref[g] for g in range(ng)]
        mx = functools.reduce(jnp.maximum, lses)
        es = [jnp.exp(l - mx) for l in lses]
        inv = 1.0 / functools.reduce(jnp.add, es)
        for g in range(ng):
            alpha_ref[g] = es[g] * inv
        for g, d in enumerate(dilations):
            for r in range(d):
                rows = _strided_rows(r, bm // d, d)
                alpha = alpha_ref[g, rows, :]
                for h in range(heads):
                    cols = slice(h * HEAD_DIM, (h + 1) * HEAD_DIM)
                    term = alpha[:, h : h + 1] * o_refs[g][r, :, cols].astype(F32)
                    if g == 0:
                        acc_ref[h, rows, :] = term
                    else:
                        acc_ref[h, rows, :] += term
        for h in range(heads):
            merge_into_ref[:, h * HEAD_DIM : (h + 1) * HEAD_DIM] = acc_ref[h].astype(merge_into_ref.dtype)

        y = _dot(multiply_ref[...], w_ref[...])
        h_new = h_ref[...] + _rms(y, ga_ref[...])
        h_out_ref[...] = h_new
        yn_ref[...] = _rms(h_new, gb_ref[...]).astype(yn_ref.dtype)

    pl.when(ii % 2 == 0)(functools.partial(step, merged_even_ref, merged_odd_ref))
    pl.when(ii % 2 == 1)(functools.partial(step, merged_odd_ref, merged_even_ref))


def _merge_outproj(outs, lses, w_out, h, gain_post, gain_next, heads, batch, seq, dilations, bm=256):
    width = heads * HEAD_DIM
    n = w_out.shape[1]
    bm = _tile(seq, bm)
    nt = seq // bm
    tiles = batch * nt

    def in_map(ii):
        t = jnp.minimum(ii, tiles - 1)
        return (t // nt, 0, t % nt, 0)

    row = pl.BlockSpec((bm, n), lambda ii: (jnp.maximum(ii - 1, 0), 0))
    vec = pl.BlockSpec((1, n), lambda ii: (0, 0))
    o_specs = [pl.BlockSpec((None, d, bm // d, width), in_map) for d in dilations]
    l_specs = [pl.BlockSpec((None, d, bm // d, ATTN_BLOCK), in_map) for d in dilations]
    outs = [o.reshape(batch, d, seq // d, width) for o, d in zip(outs, dilations)]
    lses = [l.reshape(batch, d, seq // d, ATTN_BLOCK) for l, d in zip(lses, dilations)]
    return pl.pallas_call(
        functools.partial(_merge_outproj_kernel, heads=heads, dilations=dilations),
        grid=(tiles + 1,),
        in_specs=o_specs + l_specs + [pl.BlockSpec((width, n), lambda ii: (0, 0), pipeline_mode=pl.Buffered(1)),
                                      row, vec, vec],
        out_specs=[row, row],
        out_shape=[jax.ShapeDtypeStruct((batch * seq, n), F32), jax.ShapeDtypeStruct((batch * seq, n), BF16)],
        scratch_shapes=[
            pltpu.VMEM((bm, width), BF16),
            pltpu.VMEM((bm, width), BF16),
            pltpu.VMEM((len(dilations), bm, ATTN_BLOCK), F32),
            pltpu.VMEM((heads, bm, HEAD_DIM), F32),
        ],
        compiler_params=_params(("arbitrary",)),
        name="attn_merge_outproj",
    )(*outs, *lses, w_out, h, gain_post.reshape(1, n), gain_next.reshape(1, n))


def _hgrn_query(acc):
    return _silu(acc) * (HEAD_DIM ** -0.5)


def _hgrn_forget(acc, lb_logits, *, layer):
    p = jnp.exp(lb_logits - jnp.max(lb_logits, axis=0, keepdims=True))
    p = p / jnp.sum(p, axis=0, keepdims=True)
    lb = jnp.sum(p[: layer + 1], axis=0, keepdims=True) - p[0:1]
    return lb + (1.0 - lb) * jax.nn.sigmoid(acc)


def _hgrn_rec_kernel(q_ref, f_ref, v_ref, g_ref, gain_ref, o_ref, state_ref, b_ref, *, heads_per_step, chunks):
    @pl.when(pl.program_id(2) == 0)
    def _():
        state_ref[...] = jnp.zeros_like(state_ref)

    row = lax.broadcasted_iota(jnp.int32, (HG_CHUNK, HG_CHUNK), 0)
    col = lax.broadcasted_iota(jnp.int32, (HG_CHUNK, HG_CHUNK), 1)
    causal = row >= col
    tri = causal.astype(BF16)
    tri2 = jnp.concatenate([tri, tri], axis=1)
    mid = HG_CHUNK // 2
    gain = gain_ref[...]
    heads = range(heads_per_step)
    cols = [slice(h * HEAD_DIM, (h + 1) * HEAD_DIM) for h in heads]

    for c in range(chunks):
        rows = pl.ds(c * HG_CHUNK, HG_CHUNK)
        log_f = jnp.log(f_ref[rows, :])
        hi = log_f.astype(BF16)
        lo = (log_f - hi.astype(F32)).astype(BF16)
        b_ref[rows, :] = _dot(tri2, jnp.concatenate([hi, lo], axis=0))

    def chunk(c, carry):
        rows = pl.ds(pl.multiple_of(c * HG_CHUNK, HG_CHUNK), HG_CHUNK)
        b = [b_ref[rows, cols[h]] for h in heads]
        b_mid = [b[h][mid - 1 : mid, :] for h in heads]
        b_last = [b[h][HG_CHUNK - 1 :, :] for h in heads]
        key = [1.0 - f_ref[rows, cols[h]] for h in heads]
        q = [q_ref[rows, cols[h]] for h in heads]
        v = [v_ref[rows, cols[h]] for h in heads]
        scores = [_dot_nt((q[h] * jnp.exp(b[h] - b_mid[h])).astype(BF16),
                          (key[h] * jnp.exp(b_mid[h] - b[h])).astype(BF16)) for h in heads]
        probs = [jnp.where(causal, scores[h], 0.0).astype(BF16) for h in heads]
        state = [state_ref[h] for h in heads]
        o = [_dot(jnp.concatenate([(q[h] * jnp.exp(b[h])).astype(BF16), probs[h]], axis=1),
                  jnp.concatenate([state[h].astype(BF16), v[h]], axis=0)) for h in heads]
        update = [_dot_tn((key[h] * jnp.exp(b_last[h] - b[h])).astype(BF16), v[h]) for h in heads]
        for h in heads:
            decay = jnp.transpose(jnp.broadcast_to(jnp.exp(b_last[h]), (8, HEAD_DIM)))[:, 0:1]
            state_ref[h] = state[h] * decay + update[h]
        for h in heads:
            o_ref[rows, cols[h]] = (_rms(o[h], gain) * g_ref[rows, cols[h]]).astype(o_ref.dtype)
        return carry

    lax.fori_loop(0, chunks, chunk, 0)


def _hgrn_recurrence(q, f, v, g, out_gain, batch, seq, heads_per_step=16, ts=512):
    m, width = q.shape
    heads = width // HEAD_DIM
    heads_per_step = math.gcd(heads, heads_per_step)
    ts = _tile(seq, ts)
    bw = heads_per_step * HEAD_DIM
    nt = seq // ts
    spec = pl.BlockSpec((ts, bw), lambda b, hg, t: (b * nt + t, hg))
    return pl.pallas_call(
        functools.partial(_hgrn_rec_kernel, heads_per_step=heads_per_step, chunks=ts // HG_CHUNK),
        grid=(batch, heads // heads_per_step, nt),
        in_specs=[spec, spec, spec, spec, pl.BlockSpec((1, HEAD_DIM), lambda b, hg, t: (0, 0))],
        out_specs=spec,
        out_shape=jax.ShapeDtypeStruct((m, width), BF16),
        scratch_shapes=[pltpu.VMEM((heads_per_step, HEAD_DIM, HEAD_DIM), F32), pltpu.VMEM((ts, bw), F32)],
        compiler_params=_params(("parallel", "parallel", "arbitrary")),
        name="hgrn_recurrence",
    )(q, f, v, g, out_gain.reshape(1, HEAD_DIM))


def _cast_slice(w_slice):
    return lax.optimization_barrier(w_slice).astype(BF16)


def _head_cols(width, bn, most):
    return min(most // bn, width // bn // 2) * bn


def kernel(x, norm_gains, rel_bias, attn_w_in, attn_w_out, hgrn_w_in, hgrn_lb_logits, hgrn_out_gain, hgrn_w_out,
           ffn_w_in, ffn_w_out):
    batch, seq, d = x.shape
    m = batch * seq
    depth = norm_gains.shape[0]
    attn_heads = attn_w_out.shape[1] // HEAD_DIM
    attn_width = attn_heads * HEAD_DIM
    dilations = tuple(dil for _, dil in DILATED_GROUPS)
    d_ff = ffn_w_out.shape[1]
    hg_width = hgrn_w_in.shape[2] // 4
    ffn_head = _head_cols(d_ff, FFN_BN, FFN_HEAD_COLS)
    hg_bn = _tile(hg_width, PROJ_BN)
    hg_head = _head_cols(hg_width, hg_bn, HGRN_HEAD_COLS)
    attn_head = _head_cols(3 * attn_width, _tile(3 * attn_width, PROJ_BN), ATTN_HEAD_COLS)
    h = x.reshape(m, d)
    ffn_w_out_bf16 = ffn_w_out.astype(BF16)
    hgrn_w_out_bf16 = hgrn_w_out.astype(BF16)
    yn = q_head = hidden_head = None
    bias = None
    for i in range(depth):
        gains = norm_gains[i]
        if i % 2 == 0:
            if bias is None:
                bias = _bias_tables(rel_bias, attn_heads)
            g_head = dilations.index(1) if attn_head and 1 in dilations else None
            w_head = None if g_head is None else _cast_slice(
                attn_w_in[i // 2, :, g_head * 3 * attn_width : g_head * 3 * attn_width + attn_head])
            yns, proj_head = _prenorm_dilated(h, gains[0], batch, seq, dilations, w_head, 3 * attn_width)
            outs, lses = [], []
            for g, dil in enumerate(dilations):
                proj = _proj(yns[g], attn_w_in, i // 2, [g * 3 * attn_width], 3 * attn_width, _identity, BF16,
                             bn=PROJ_BN, name="attn_in", col0=attn_head if g == g_head else 0,
                             out_init=proj_head if g == g_head else None)
                o, l = _attn_group(proj, bias[g], seq // dil, attn_heads)
                outs.append(o)
                lses.append(l)
            h, yn = _merge_outproj(outs, lses, attn_w_out[i // 2].astype(BF16), h, gains[1], gains[2], attn_heads,
                                   batch, seq, dilations)
        else:
            if yn is None:
                yn = _prenorm(h, gains[0])
            hproj = functools.partial(_proj, yn, hgrn_w_in, i // 2, bn=PROJ_BN)
            q = hproj([0], hg_width, _hgrn_query, F32, name="hgrn_in_q", col0=0 if q_head is None else hg_head,
                      out_init=q_head)
            f = hproj([hg_width], hg_width, functools.partial(_hgrn_forget, layer=i), F32, aux=hgrn_lb_logits,
                      name="hgrn_in_f")
            v = hproj([2 * hg_width], hg_width, _identity, BF16, name="hgrn_in_i")
            g = hproj([3 * hg_width], hg_width, _silu, F32, name="hgrn_in_g")
            o = _hgrn_recurrence(q, f, v, g, hgrn_out_gain[i // 2], batch, seq)
            y = _matmul(o, hgrn_w_out_bf16, i // 2, F32, name="hgrn_out")
            if ffn_head:
                w_heads = [_cast_slice(ffn_w_in[i, :, c : c + ffn_head]) for c in (0, d_ff)]
                h, yn, hidden_head = _postnorm_head(h, y, gains[1], gains[2], w_heads, d_ff, _swiglu, BF16,
                                                    name="postnorm_ffn_head")
            else:
                h, yn = _postnorm(h, y, gains[1], gains[2])
        hidden = _proj(yn, ffn_w_in, i, [0, d_ff], d_ff, _swiglu, BF16, bm=2048, bn=FFN_BN, name="ffn_in",
                       col0=0 if hidden_head is None else ffn_head, out_init=hidden_head)
        y = _matmul(hidden, ffn_w_out_bf16, i, F32, bm=512, bn=512, name="ffn_out")
        yn = q_head = hidden_head = None
        if i + 1 < depth and (i + 1) % 2 == 1:
            if hg_head:
                w_heads = [_cast_slice(hgrn_w_in[(i + 1) // 2, :, :hg_head])]
                h, yn, q_head = _postnorm_head(h, y, gains[3], norm_gains[i + 1, 0], w_heads, hg_width, _hgrn_query,
                                               F32, name="postnorm_hgrn_head")
            else:
                h, yn = _postnorm(h, y, gains[3], norm_gains[i + 1, 0])
        else:
            h = _postnorm(h, y, gains[3], None)
    return h.reshape(batch, seq, d)
```

```python
import functools
import math

import numpy as np
import jax
import jax.numpy as jnp
from jax import lax
from jax.experimental import pallas as pl
from jax.experimental.pallas import tpu as pltpu

RMS_EPS = 1e-6
NEG_INF = -1e30

DILATED_GROUPS = ((128, 1), (512, 4), (2048, 16))
HEAD_DIM = 128
ATTN_BLOCK = 128
NUM_BUCKETS = 32
MAX_EXACT = 16
REL_MAX_DISTANCE = 2048
HG_CHUNK = 64
PROJ_ROW_CHUNKS = 4
PROJ_BN = 1024
FFN_BN = 256
FFN_HEAD_COLS = 768
HGRN_HEAD_COLS = 1024
ATTN_HEAD_COLS = 1024

V7X_VMEM_BYTES = 64 * 1024 * 1024
VMEM_LIMIT = V7X_VMEM_BYTES - 8 * 1024 * 1024

BF16 = jnp.bfloat16
F32 = jnp.float32


def _params(semantics, vmem=VMEM_LIMIT):
    return pltpu.CompilerParams(dimension_semantics=semantics, vmem_limit_bytes=vmem)


def _tile(n, pref):
    if n <= pref:
        return n
    t = pref - pref % 128
    while t >= 128:
        if n % t == 0:
            return t
        t -= 128
    raise ValueError(f"no 128-multiple tile of {n} below {pref}")


def _dot(a, b):
    return jnp.dot(a, b, preferred_element_type=F32)


def _dot_nt(a, b):
    return lax.dot_general(a, b, (((1,), (1,)), ((), ())), preferred_element_type=F32)


def _dot_tn(a, b):
    return lax.dot_general(a, b, (((0,), (0,)), ((), ())), preferred_element_type=F32)


def _rms(x, gain):
    return x * lax.rsqrt(jnp.mean(x * x, axis=-1, keepdims=True) + RMS_EPS) * gain


def _silu(x):
    return x * jax.nn.sigmoid(x)


def _strided_rows(r, count, stride):
    return pl.ds(r, count) if stride == 1 else pl.ds(r, count, stride=stride)


def _prenorm_kernel(x_ref, g_ref, o_ref):
    o_ref[...] = _rms(x_ref[...], g_ref[...]).astype(o_ref.dtype)


def _prenorm(x, gain, tr=256):
    m, d = x.shape
    tr = _tile(m, tr)
    return pl.pallas_call(
        _prenorm_kernel,
        grid=(m // tr,),
        in_specs=[pl.BlockSpec((tr, d), lambda i: (i, 0)), pl.BlockSpec((1, d), lambda i: (0, 0))],
        out_specs=pl.BlockSpec((tr, d), lambda i: (i, 0)),
        out_shape=jax.ShapeDtypeStruct((m, d), BF16),
        compiler_params=_params(("parallel",)),
        name="prenorm",
    )(x, gain.reshape(1, d))


def _prenorm_dilated_kernel(x_ref, g_ref, *refs, dilations, has_head):
    w_ref = refs[0] if has_head else None
    out_refs = refs[has_head : has_head + len(dilations)]
    head_ref = refs[has_head + len(dilations)] if has_head else None
    y_ref = refs[-1]
    y = _rms(x_ref[...], g_ref[...])
    tr = y.shape[0]
    if has_head:
        head_ref[...] = _dot(y.astype(BF16), w_ref[...]).astype(head_ref.dtype)
    for c in range(y_ref.shape[0]):
        cols = slice(c * 128, (c + 1) * 128)
        y_ref[c] = y[:, cols]
        for o_ref, d in zip(out_refs, dilations):
            for r in range(d):
                o_ref[r, :, cols] = y_ref[c, _strided_rows(r, tr // d, d), :].astype(o_ref.dtype)


def _prenorm_dilated(x, gain, batch, seq, dilations, w_head=None, head_width=None, tr=256):
    m, d = x.shape
    tr = _tile(seq, tr)
    nt = seq // tr
    has_head = w_head is not None
    head_in = [pl.BlockSpec(w_head.shape, lambda i: (0, 0), pipeline_mode=pl.Buffered(1))] if has_head else []
    head_out = [pl.BlockSpec((tr, w_head.shape[1]), lambda i: (i, 0))] if has_head else []
    head_shape = [jax.ShapeDtypeStruct((m, head_width), BF16)] if has_head else []
    outs = pl.pallas_call(
        functools.partial(_prenorm_dilated_kernel, dilations=dilations, has_head=has_head),
        grid=(m // tr,),
        in_specs=[pl.BlockSpec((tr, d), lambda i: (i, 0)), pl.BlockSpec((1, d), lambda i: (0, 0))] + head_in,
        out_specs=[pl.BlockSpec((None, dil, tr // dil, d), lambda i: (i // nt, 0, i % nt, 0)) for dil in dilations]
        + head_out,
        out_shape=[jax.ShapeDtypeStruct((batch, dil, seq // dil, d), BF16) for dil in dilations] + head_shape,
        scratch_shapes=[pltpu.VMEM((d // 128, tr, 128), F32)],
        compiler_params=_params(("parallel",)),
        name="prenorm_dilated",
    )(x, gain.reshape(1, d), *([w_head] if has_head else []))
    return [o.reshape(m, d) for o in outs[: len(dilations)]], (outs[-1] if has_head else None)


def _postnorm_kernel(h_ref, y_ref, ga_ref, gb_ref, h_out_ref, yn_ref):
    h = h_ref[...] + _rms(y_ref[...], ga_ref[...])
    h_out_ref[...] = h
    yn_ref[...] = _rms(h, gb_ref[...]).astype(yn_ref.dtype)


def _postnorm_last_kernel(h_ref, y_ref, ga_ref, h_out_ref):
    h_out_ref[...] = h_ref[...] + _rms(y_ref[...], ga_ref[...])


def _postnorm(h, y, gain_post, gain_next, tr=256):
    m, d = h.shape
    tr = _tile(m, tr)
    row = pl.BlockSpec((tr, d), lambda i: (i, 0))
    vec = pl.BlockSpec((1, d), lambda i: (0, 0))
    if gain_next is None:
        return pl.pallas_call(
            _postnorm_last_kernel,
            grid=(m // tr,),
            in_specs=[row, row, vec],
            out_specs=row,
            out_shape=jax.ShapeDtypeStruct((m, d), F32),
            compiler_params=_params(("parallel",)),
            name="postnorm_last",
        )(h, y, gain_post.reshape(1, d))
    return pl.pallas_call(
        _postnorm_kernel,
        grid=(m // tr,),
        in_specs=[row, row, vec, vec],
        out_specs=[row, row],
        out_shape=[jax.ShapeDtypeStruct((m, d), F32), jax.ShapeDtypeStruct((m, d), BF16)],
        compiler_params=_params(("parallel",)),
        name="postnorm",
    )(h, y, gain_post.reshape(1, d), gain_next.reshape(1, d))


def _postnorm_head_kernel(h_ref, y_ref, ga_ref, gb_ref, *refs, n_w, epilogue):
    w_refs = refs[:n_w]
    h_out_ref, yn_ref, head_ref = refs[n_w:]
    h = h_ref[...] + _rms(y_ref[...], ga_ref[...])
    h_out_ref[...] = h
    yn = _rms(h, gb_ref[...]).astype(yn_ref.dtype)
    yn_ref[...] = yn
    head_ref[...] = epilogue(*[_dot(yn, w_ref[...]) for w_ref in w_refs]).astype(head_ref.dtype)


def _postnorm_head(h, y, gain_post, gain_next, w_heads, width, epilogue, out_dtype, tr=256, name="postnorm_head"):
    m, d = h.shape
    tr = _tile(m, tr)
    head_cols = w_heads[0].shape[1]
    row = pl.BlockSpec((tr, d), lambda i: (i, 0))
    vec = pl.BlockSpec((1, d), lambda i: (0, 0))
    w_spec = pl.BlockSpec((d, head_cols), lambda i: (0, 0), pipeline_mode=pl.Buffered(1))
    return pl.pallas_call(
        functools.partial(_postnorm_head_kernel, n_w=len(w_heads), epilogue=epilogue),
        grid=(m // tr,),
        in_specs=[row, row, vec, vec] + [w_spec] * len(w_heads),
        out_specs=[row, row, pl.BlockSpec((tr, head_cols), lambda i: (i, 0))],
        out_shape=[jax.ShapeDtypeStruct((m, d), F32), jax.ShapeDtypeStruct((m, d), BF16),
                   jax.ShapeDtypeStruct((m, width), out_dtype)],
        compiler_params=_params(("parallel",)),
        name=name,
    )(h, y, gain_post.reshape(1, d), gain_next.reshape(1, d), *w_heads)


def _matmul_kernel(x_ref, w_ref, o_ref):
    o_ref[...] = _dot(x_ref[...], w_ref[...]).astype(o_ref.dtype)


def _matmul(x, w, layer, out_dtype, bm=1024, bn=1024, name="matmul"):
    m, k = x.shape
    n = w.shape[2]
    bm, bn = _tile(m, bm), _tile(n, bn)
    return pl.pallas_call(
        _matmul_kernel,
        grid=(m // bm, n // bn),
        in_specs=[pl.BlockSpec((bm, k), lambda i, j: (i, 0)), pl.BlockSpec((None, k, bn), lambda i, j: (layer, 0, j))],
        out_specs=pl.BlockSpec((bm, bn), lambda i, j: (i, j)),
        out_shape=jax.ShapeDtypeStruct((m, n), out_dtype),
        compiler_params=_params(("parallel", "arbitrary")),
        name=name,
    )(x, w)


def _proj_kernel(*refs, n_w, has_aux, has_init, has_side, epilogue):
    x_ref, w_refs = refs[0], refs[1 : 1 + n_w]
    aux_ref = refs[1 + n_w] if has_aux else None
    n_in = 1 + n_w + has_aux + has_init + has_side
    o_ref = refs[n_in]
    wb_refs = refs[n_in + 1 + has_side :]
    jj, i = pl.program_id(0), pl.program_id(1)
    slab = w_refs[0].shape[0]
    if has_side:
        refs[n_in + 1][...] = refs[n_in - 1][...].astype(refs[n_in + 1].dtype)

    @pl.when(jj < pl.num_programs(0) - 1)
    def _():
        rows = pl.ds(pl.multiple_of(i * slab, slab), slab)
        for w_ref, wb_ref in zip(w_refs, wb_refs):
            wb_ref[jj % 2, rows, :] = w_ref[...].astype(wb_ref.dtype)

    @pl.when(jj > 0)
    def _():
        chunk = x_ref.shape[0] // PROJ_ROW_CHUNKS
        for r in range(PROJ_ROW_CHUNKS):
            rows = slice(r * chunk, (r + 1) * chunk)
            x = x_ref[rows, :]
            accs = [_dot(x, wb_ref[(jj - 1) % 2]) for wb_ref in wb_refs]
            out = epilogue(*accs, aux_ref[...]) if has_aux else epilogue(*accs)
            o_ref[rows, :] = out.astype(o_ref.dtype)


def _proj(x, w, layer, col_offsets, width, epilogue, out_dtype, aux=None, bm=1024, bn=1024, name="proj", col0=0,
          out_init=None, side_cast=None):
    m, k = x.shape
    bm, bn = _tile(m, bm), _tile(width, bn)
    n_w = len(col_offsets)
    nj, ni = (width - col0) // bn, m // bm
    slab = k // ni
    assert k % ni == 0 and slab % 16 == 0, (k, ni)
    assert all(c % bn == 0 for c in col_offsets) and col0 % bn == 0, (col_offsets, col0, bn)

    def w_map(jj, i, c):
        return (layer, jnp.where(jj < nj, i, ni - 1), jnp.minimum(jj, nj - 1) + c + col0 // bn)

    def row_map(jj, i):
        return jnp.where(jj > 0, i, 0)

    def col_map(jj):
        return jnp.maximum(jj - 1, 0) + col0 // bn

    w_specs = [pl.BlockSpec((None, slab, bn), functools.partial(w_map, c=c // bn)) for c in col_offsets]
    aux_specs = [] if aux is None else [pl.BlockSpec((aux.shape[0], bn), lambda jj, i: (0, col_map(jj)))]
    init_specs = [] if out_init is None else [pl.BlockSpec(memory_space=pl.ANY)]
    n_init = 1 + n_w + len(aux_specs)
    side_in, side_out, side_shape, side_args = [], [], [], []
    if side_cast is not None:
        w2, layer2, rb = side_cast
        rows2, cols2 = w2.shape[1:]
        blocks = rows2 // rb
        assert rows2 % rb == 0 and blocks <= (nj + 1) * ni, (rows2, rb, nj, ni)

        def side_block(jj, i):
            return jnp.minimum(jj * ni + i, blocks - 1)

        side_in = [pl.BlockSpec((None, rb, cols2), lambda jj, i: (layer2, side_block(jj, i), 0))]
        side_out = [pl.BlockSpec((rb, cols2), lambda jj, i: (side_block(jj, i), 0))]
        side_shape = [jax.ShapeDtypeStruct((rows2, cols2), BF16)]
        side_args = [w2]
    result = pl.pallas_call(
        functools.partial(_proj_kernel, n_w=n_w, has_aux=aux is not None, has_init=out_init is not None,
                          has_side=side_cast is not None, epilogue=epilogue),
        grid=(nj + 1, ni),
        in_specs=[pl.BlockSpec((bm, k), lambda jj, i: (row_map(jj, i), 0))] + w_specs + aux_specs + init_specs
        + side_in,
        out_specs=[pl.BlockSpec((bm, bn), lambda jj, i: (row_map(jj, i), col_map(jj)))] + side_out,
        out_shape=[jax.ShapeDtypeStruct((m, width), out_dtype)] + side_shape,
        scratch_shapes=[pltpu.VMEM((2, k, bn), BF16) for _ in range(n_w)],
        input_output_aliases={} if out_init is None else {n_init: 0},
        compiler_params=_params(("arbitrary", "arbitrary")),
        name=name,
    )(x, *([w] * n_w), *([] if aux is None else [aux]), *([] if out_init is None else [out_init]), *side_args)
    return result if side_cast is not None else result[0]


def _identity(acc):
    return acc


def _swiglu(gate, up):
    return _silu(gate) * up


def _t5_bucket_np(dist):
    dist = np.maximum(dist, 0)
    log_ratio = np.log(np.maximum(dist, 1).astype(np.float64) / MAX_EXACT) / math.log(REL_MAX_DISTANCE / MAX_EXACT)
    large = np.minimum(MAX_EXACT + (log_ratio * (NUM_BUCKETS - MAX_EXACT)).astype(np.int64), NUM_BUCKETS - 1)
    return np.where(dist < MAX_EXACT, dist, large)


def _bucket_tables():
    qi = np.arange(ATTN_BLOCK)[:, None]
    ki = np.arange(2 * ATTN_BLOCK)[None, :]
    rel = qi - ki + ATTN_BLOCK
    tables = []
    for window, dilation in DILATED_GROUPS:
        band = (rel >= 0) & (rel <= window // dilation)
        tables.append(np.where(band, _t5_bucket_np(rel * dilation), -1))
    return np.stack(tables).astype(np.int32)


def _bias_kernel(rb_ref, bucket_ref, o_ref, *, heads):
    g, h = pl.program_id(0), pl.program_id(1)
    bucket = bucket_ref[...]
    acc = jnp.full(bucket.shape, NEG_INF, F32)
    for b in range(NUM_BUCKETS):
        acc = jnp.where(bucket == b, rb_ref[b, g * heads + h], acc)
    o_ref[...] = acc


def _bias_tables(rel_bias, heads):
    ng = len(DILATED_GROUPS)
    buckets = jnp.asarray(_bucket_tables())
    return pl.pallas_call(
        functools.partial(_bias_kernel, heads=heads),
        grid=(ng, heads),
        in_specs=[
            pl.BlockSpec(memory_space=pltpu.SMEM),
            pl.BlockSpec((None, ATTN_BLOCK, 2 * ATTN_BLOCK), lambda g, h: (g, 0, 0)),
        ],
        out_specs=pl.BlockSpec((None, None, ATTN_BLOCK, 2 * ATTN_BLOCK), lambda g, h: (g, h, 0, 0)),
        out_shape=jax.ShapeDtypeStruct((ng, heads, ATTN_BLOCK, 2 * ATTN_BLOCK), F32),
        compiler_params=_params(("arbitrary", "arbitrary")),
        name="attn_bias",
    )(rel_bias, buckets)


def _attn_kernel(q_ref, kc_ref, kp_ref, vc_ref, vp_ref, bias_ref, o_ref, lse_ref, *, heads):
    has_prev = pl.program_id(1) > 0
    scale = HEAD_DIM ** -0.5
    lane = lax.broadcasted_iota(jnp.int32, (ATTN_BLOCK, ATTN_BLOCK), 1)
    lse_all = jnp.zeros((ATTN_BLOCK, ATTN_BLOCK), F32)
    ones = jnp.ones((ATTN_BLOCK, HEAD_DIM), BF16)
    for h in range(heads):
        cols = slice(h * HEAD_DIM, (h + 1) * HEAD_DIM)
        q = q_ref[:, cols]
        s_prev = _dot_nt(q, kp_ref[:, cols]) * scale + bias_ref[h, :, :ATTN_BLOCK]
        s_prev = jnp.where(has_prev, s_prev, NEG_INF)
        s_cur = _dot_nt(q, kc_ref[:, cols]) * scale + bias_ref[h, :, ATTN_BLOCK:]
        m = jnp.maximum(jnp.max(s_prev, axis=-1, keepdims=True), jnp.max(s_cur, axis=-1, keepdims=True))
        p_prev = jnp.exp(s_prev - m).astype(BF16)
        p_cur = jnp.exp(s_cur - m).astype(BF16)
        ov = (_dot(p_prev, jnp.concatenate([vp_ref[:, cols], ones], axis=1))
              + _dot(p_cur, jnp.concatenate([vc_ref[:, cols], ones], axis=1)))
        denom = ov[:, HEAD_DIM:]
        o_ref[:, cols] = (ov[:, :HEAD_DIM] / denom).astype(o_ref.dtype)
        lse_all = jnp.where(lane == h, m + jnp.log(denom[:, 0:1]), lse_all)
    lse_ref[...] = lse_all


def _attn_group(proj, bias_g, sub_len, heads):
    rows = proj.shape[0]
    width = heads * HEAD_DIM
    nb = sub_len // ATTN_BLOCK
    blk = (ATTN_BLOCK, width)

    def cur(which):
        return lambda s, i: (s * nb + i, which)

    def prev(which):
        return lambda s, i: (s * nb + jnp.maximum(i - 1, 0), which)

    return pl.pallas_call(
        functools.partial(_attn_kernel, heads=heads),
        grid=(rows // sub_len, nb),
        in_specs=[
            pl.BlockSpec(blk, cur(0)),
            pl.BlockSpec(blk, cur(1)),
            pl.BlockSpec(blk, prev(1)),
            pl.BlockSpec(blk, cur(2)),
            pl.BlockSpec(blk, prev(2)),
            pl.BlockSpec((heads, ATTN_BLOCK, 2 * ATTN_BLOCK), lambda s, i: (0, 0, 0)),
        ],
        out_specs=[
            pl.BlockSpec(blk, lambda s, i: (s * nb + i, 0)),
            pl.BlockSpec((ATTN_BLOCK, ATTN_BLOCK), lambda s, i: (s * nb + i, 0)),
        ],
        out_shape=[jax.ShapeDtypeStruct((rows, width), BF16), jax.ShapeDtypeStruct((rows, ATTN_BLOCK), F32)],
        compiler_params=_params(("parallel", "arbitrary")),
        name="attn_group",
    )(proj, proj, proj, proj, proj, bias_g)


def _merge_outproj_kernel(*refs, heads, dilations):
    ng = len(dilations)
    o_refs, l_refs = refs[:ng], refs[ng : 2 * ng]
    w_ref, h_ref, ga_ref, gb_ref, h_out_ref, yn_ref, merged_even_ref, merged_odd_ref, alpha_ref, acc_ref = refs[2 * ng :]
    bm = acc_ref.shape[1]
    ii = pl.program_id(0)

    @pl.when(ii == 0)
    def _():
        merged_odd_ref[...] = jnp.zeros_like(merged_odd_ref)

    def step(merge_into_ref, multiply_ref):
        for g, d in enumerate(dilations):
            for r in range(d):
                alpha_ref[g, _strided_rows(r, bm // d, d), :] = l_refs[g][r]
        lses = [alpha_ref[g] for g in range(ng)]
        mx = functools.reduce(jnp.maximum, lses)
        es = [jnp.exp(l - mx) for l in lses]
        inv = 1.0 / functools.reduce(jnp.add, es)
        for g in range(ng):
            alpha_ref[g] = es[g] * inv
        for g, d in enumerate(dilations):
            for r in range(d):
                rows = _strided_rows(r, bm // d, d)
                alpha = alpha_ref[g, rows, :]
                for h in range(heads):
                    cols = slice(h * HEAD_DIM, (h + 1) * HEAD_DIM)
                    term = alpha[:, h : h + 1] * o_refs[g][r, :, cols].astype(F32)
                    if g == 0:
                        acc_ref[h, rows, :] = term
                    else:
                        acc_ref[h, rows, :] += term
        for h in range(heads):
            merge_into_ref[:, h * HEAD_DIM : (h + 1) * HEAD_DIM] = acc_ref[h].astype(merge_into_ref.dtype)

        y = _dot(multiply_ref[...], w_ref[...])
        h_new = h_ref[...] + _rms(y, ga_ref[...])
        h_out_ref[...] = h_new
        yn_ref[...] = _rms(h_new, gb_ref[...]).astype(yn_ref.dtype)

    pl.when(ii % 2 == 0)(functools.partial(step, merged_even_ref, merged_odd_ref))
    pl.when(ii % 2 == 1)(functools.partial(step, merged_odd_ref, merged_even_ref))


def _merge_outproj(outs, lses, w_out, h, gain_post, gain_next, heads, batch, seq, dilations, bm=256):
    width = heads * HEAD_DIM
    n = w_out.shape[1]
    bm = _tile(seq, bm)
    nt = seq // bm
    tiles = batch * nt

    def in_map(ii):
        t = jnp.minimum(ii, tiles - 1)
        return (t // nt, 0, t % nt, 0)

    row = pl.BlockSpec((bm, n), lambda ii: (jnp.maximum(ii - 1, 0), 0))
    vec = pl.BlockSpec((1, n), lambda ii: (0, 0))
    o_specs = [pl.BlockSpec((None, d, bm // d, width), in_map) for d in dilations]
    l_specs = [pl.BlockSpec((None, d, bm // d, ATTN_BLOCK), in_map) for d in dilations]
    outs = [o.reshape(batch, d, seq // d, width) for o, d in zip(outs, dilations)]
    lses = [l.reshape(batch, d, seq // d, ATTN_BLOCK) for l, d in zip(lses, dilations)]
    return pl.pallas_call(
        functools.partial(_merge_outproj_kernel, heads=heads, dilations=dilations),
        grid=(tiles + 1,),
        in_specs=o_specs + l_specs + [pl.BlockSpec((width, n), lambda ii: (0, 0), pipeline_mode=pl.Buffered(1)),
                                      row, vec, vec],
        out_specs=[row, row],
        out_shape=[jax.ShapeDtypeStruct((batch * seq, n), F32), jax.ShapeDtypeStruct((batch * seq, n), BF16)],
        scratch_shapes=[
            pltpu.VMEM((bm, width), BF16),
            pltpu.VMEM((bm, width), BF16),
            pltpu.VMEM((len(dilations), bm, ATTN_BLOCK), F32),
            pltpu.VMEM((heads, bm, HEAD_DIM), F32),
        ],
        compiler_params=_params(("arbitrary",)),
        name="attn_merge_outproj",
    )(*outs, *lses, w_out, h, gain_post.reshape(1, n), gain_next.reshape(1, n))


def _hgrn_query(acc):
    return _silu(acc) * (HEAD_DIM ** -0.5)


def _hgrn_forget(acc, lb_logits, *, layer):
    p = jnp.exp(lb_logits - jnp.max(lb_logits, axis=0, keepdims=True))
    p = p / jnp.sum(p, axis=0, keepdims=True)
    lb = jnp.sum(p[: layer + 1], axis=0, keepdims=True) - p[0:1]
    return lb + (1.0 - lb) * jax.nn.sigmoid(acc)


def _hgrn_rec_kernel(q_ref, f_ref, v_ref, g_ref, gain_ref, o_ref, state_ref, b_ref, *, heads_per_step, chunks):
    @pl.when(pl.program_id(2) == 0)
    def _():
        state_ref[...] = jnp.zeros_like(state_ref)

    row = lax.broadcasted_iota(jnp.int32, (HG_CHUNK, HG_CHUNK), 0)
    col = lax.broadcasted_iota(jnp.int32, (HG_CHUNK, HG_CHUNK), 1)
    causal = row >= col
    tri = causal.astype(BF16)
    tri2 = jnp.concatenate([tri, tri], axis=1)
    mid = HG_CHUNK // 2
    gain = gain_ref[...]
    heads = range(heads_per_step)
    cols = [slice(h * HEAD_DIM, (h + 1) * HEAD_DIM) for h in heads]

    for c in range(chunks):
        rows = pl.ds(c * HG_CHUNK, HG_CHUNK)
        log_f = jnp.log(f_ref[rows, :])
        hi = log_f.astype(BF16)
        lo = (log_f - hi.astype(F32)).astype(BF16)
        b_ref[rows, :] = _dot(tri2, jnp.concatenate([hi, lo], axis=0))

    def chunk(c, carry):
        rows = pl.ds(pl.multiple_of(c * HG_CHUNK, HG_CHUNK), HG_CHUNK)
        b = [b_ref[rows, cols[h]] for h in heads]
        b_mid = [b[h][mid - 1 : mid, :] for h in heads]
        b_last = [b[h][HG_CHUNK - 1 :, :] for h in heads]
        key = [1.0 - f_ref[rows, cols[h]] for h in heads]
        q = [q_ref[rows, cols[h]] for h in heads]
        v = [v_ref[rows, cols[h]] for h in heads]
        scores = [_dot_nt((q[h] * jnp.exp(b[h] - b_mid[h])).astype(BF16),
                          (key[h] * jnp.exp(b_mid[h] - b[h])).astype(BF16)) for h in heads]
        probs = [jnp.where(causal, scores[h], 0.0).astype(BF16) for h in heads]
        state = [state_ref[h] for h in heads]
        o = [_dot(jnp.concatenate([(q[h] * jnp.exp(b[h])).astype(BF16), probs[h]], axis=1),
                  jnp.concatenate([state[h].astype(BF16), v[h]], axis=0)) for h in heads]
        update = [_dot_tn((key[h] * jnp.exp(b_last[h] - b[h])).astype(BF16), v[h]) for h in heads]
        for h in heads:
            decay = jnp.transpose(jnp.broadcast_to(jnp.exp(b_last[h]), (8, HEAD_DIM)))[:, 0:1]
            state_ref[h] = state[h] * decay + update[h]
        for h in heads:
            o_ref[rows, cols[h]] = (_rms(o[h], gain) * g_ref[rows, cols[h]]).astype(o_ref.dtype)
        return carry

    lax.fori_loop(0, chunks, chunk, 0)


def _hgrn_recurrence(q, f, v, g, out_gain, batch, seq, heads_per_step=16, ts=512):
    m, width = q.shape
    heads = width // HEAD_DIM
    heads_per_step = math.gcd(heads, heads_per_step)
    ts = _tile(seq, ts)
    bw = heads_per_step * HEAD_DIM
    nt = seq // ts
    spec = pl.BlockSpec((ts, bw), lambda b, hg, t: (b * nt + t, hg))
    return pl.pallas_call(
        functools.partial(_hgrn_rec_kernel, heads_per_step=heads_per_step, chunks=ts // HG_CHUNK),
        grid=(batch, heads // heads_per_step, nt),
        in_specs=[spec, spec, spec, spec, pl.BlockSpec((1, HEAD_DIM), lambda b, hg, t: (0, 0))],
        out_specs=spec,
        out_shape=jax.ShapeDtypeStruct((m, width), BF16),
        scratch_shapes=[pltpu.VMEM((heads_per_step, HEAD_DIM, HEAD_DIM), F32), pltpu.VMEM((ts, bw), F32)],
        compiler_params=_params(("parallel", "parallel", "arbitrary")),
        name="hgrn_recurrence",
    )(q, f, v, g, out_gain.reshape(1, HEAD_DIM))


def _cast_slice(w_slice):
    return lax.optimization_barrier(w_slice).astype(BF16)


def _side_rows(rows, steps):
    rb = 16
    while rows % rb or rows // rb > steps:
        rb += 16
    return rb


def _head_cols(width, bn, most):
    return min(most // bn, width // bn // 2) * bn


def kernel(x, norm_gains, rel_bias, attn_w_in, attn_w_out, hgrn_w_in, hgrn_lb_logits, hgrn_out_gain, hgrn_w_out,
           ffn_w_in, ffn_w_out):
    batch, seq, d = x.shape
    m = batch * seq
    depth = norm_gains.shape[0]
    attn_heads = attn_w_out.shape[1] // HEAD_DIM
    attn_width = attn_heads * HEAD_DIM
    dilations = tuple(dil for _, dil in DILATED_GROUPS)
    d_ff = ffn_w_out.shape[1]
    hg_width = hgrn_w_in.shape[2] // 4
    ffn_head = _head_cols(d_ff, FFN_BN, FFN_HEAD_COLS)
    hg_bn = _tile(hg_width, PROJ_BN)
    hg_head = _head_cols(hg_width, hg_bn, HGRN_HEAD_COLS)
    attn_head = _head_cols(3 * attn_width, _tile(3 * attn_width, PROJ_BN), ATTN_HEAD_COLS)
    h = x.reshape(m, d)
    hgrn_w_out_bf16 = hgrn_w_out.astype(BF16)
    yn = q_head = hidden_head = None
    bias = None
    for i in range(depth):
        gains = norm_gains[i]
        if i % 2 == 0:
            if bias is None:
                bias = _bias_tables(rel_bias, attn_heads)
            g_head = dilations.index(1) if attn_head and 1 in dilations else None
            w_head = None if g_head is None else _cast_slice(
                attn_w_in[i // 2, :, g_head * 3 * attn_width : g_head * 3 * attn_width + attn_head])
            yns, proj_head = _prenorm_dilated(h, gains[0], batch, seq, dilations, w_head, 3 * attn_width)
            outs, lses = [], []
            for g, dil in enumerate(dilations):
                proj = _proj(yns[g], attn_w_in, i // 2, [g * 3 * attn_width], 3 * attn_width, _identity, BF16,
                             bn=PROJ_BN, name="attn_in", col0=attn_head if g == g_head else 0,
                             out_init=proj_head if g == g_head else None)
                o, l = _attn_group(proj, bias[g], seq // dil, attn_heads)
                outs.append(o)
                lses.append(l)
            h, yn = _merge_outproj(outs, lses, attn_w_out[i // 2].astype(BF16), h, gains[1], gains[2], attn_heads,
                                   batch, seq, dilations)
        else:
            if yn is None:
                yn = _prenorm(h, gains[0])
            hproj = functools.partial(_proj, yn, hgrn_w_in, i // 2, bn=PROJ_BN)
            q = hproj([0], hg_width, _hgrn_query, F32, name="hgrn_in_q", col0=0 if q_head is None else hg_head,
                      out_init=q_head)
            f = hproj([hg_width], hg_width, functools.partial(_hgrn_forget, layer=i), F32, aux=hgrn_lb_logits,
                      name="hgrn_in_f")
            v = hproj([2 * hg_width], hg_width, _identity, BF16, name="hgrn_in_i")
            g = hproj([3 * hg_width], hg_width, _silu, F32, name="hgrn_in_g")
            o = _hgrn_recurrence(q, f, v, g, hgrn_out_gain[i // 2], batch, seq)
            y = _matmul(o, hgrn_w_out_bf16, i // 2, F32, name="hgrn_out")
            if ffn_head:
                w_heads = [_cast_slice(ffn_w_in[i, :, c : c + ffn_head]) for c in (0, d_ff)]
                h, yn, hidden_head = _postnorm_head(h, y, gains[1], gains[2], w_heads, d_ff, _swiglu, BF16,
                                                    name="postnorm_ffn_head")
            else:
                h, yn = _postnorm(h, y, gains[1], gains[2])
        ffn_col0 = 0 if hidden_head is None else ffn_head
        ffn_steps = ((d_ff - ffn_col0) // FFN_BN + 1) * (m // _tile(m, 2048))
        hidden, w_out_bf16 = _proj(yn, ffn_w_in, i, [0, d_ff], d_ff, _swiglu, BF16, bm=2048, bn=FFN_BN,
                                   name="ffn_in", col0=ffn_col0, out_init=hidden_head,
                                   side_cast=(ffn_w_out, i, _side_rows(d_ff, ffn_steps)))
        y = _matmul(hidden, w_out_bf16[None], 0, F32, bm=512, bn=512, name="ffn_out")
        yn = q_head = hidden_head = None
        if i + 1 < depth and (i + 1) % 2 == 1:
            if hg_head:
                w_heads = [_cast_slice(hgrn_w_in[(i + 1) // 2, :, :hg_head])]
                h, yn, q_head = _postnorm_head(h, y, gains[3], norm_gains[i + 1, 0], w_heads, hg_width, _hgrn_query,
                                               F32, name="postnorm_hgrn_head")
            else:
                h, yn = _postnorm(h, y, gains[3], norm_gains[i + 1, 0])
        else:
            h = _postnorm(h, y, gains[3], None)
    return h.reshape(batch, seq, d)
```

```python
import functools
import math

import numpy as np
import jax
import jax.numpy as jnp
from jax import lax
from jax.experimental import pallas as pl
from jax.experimental.pallas import tpu as pltpu

RMS_EPS = 1e-6
NEG_INF = -1e30

DILATED_GROUPS = ((128, 1), (512, 4), (2048, 16))
HEAD_DIM = 128
ATTN_BLOCK = 128
NUM_BUCKETS = 32
MAX_EXACT = 16
REL_MAX_DISTANCE = 2048
HG_CHUNK = 64
PROJ_ROW_CHUNKS = 4
PROJ_BN = 1024
FFN_BN = 256
FFN_HEAD_COLS = 768
HGRN_HEAD_COLS = 1024
ATTN_HEAD_COLS = 1024

V7X_VMEM_BYTES = 64 * 1024 * 1024
VMEM_LIMIT = V7X_VMEM_BYTES - 8 * 1024 * 1024

BF16 = jnp.bfloat16
F32 = jnp.float32


def _params(semantics, vmem=VMEM_LIMIT):
    return pltpu.CompilerParams(dimension_semantics=semantics, vmem_limit_bytes=vmem)


def _tile(n, pref):
    if n <= pref:
        return n
    t = pref - pref % 128
    while t >= 128:
        if n % t == 0:
            return t
        t -= 128
    raise ValueError(f"no 128-multiple tile of {n} below {pref}")


def _dot(a, b):
    return jnp.dot(a, b, preferred_element_type=F32)


def _dot_nt(a, b):
    return lax.dot_general(a, b, (((1,), (1,)), ((), ())), preferred_element_type=F32)


def _dot_tn(a, b):
    return lax.dot_general(a, b, (((0,), (0,)), ((), ())), preferred_element_type=F32)


def _rms(x, gain):
    return x * lax.rsqrt(jnp.mean(x * x, axis=-1, keepdims=True) + RMS_EPS) * gain


def _silu(x):
    return x * jax.nn.sigmoid(x)


def _strided_rows(r, count, stride):
    return pl.ds(r, count) if stride == 1 else pl.ds(r, count, stride=stride)


def _prenorm_kernel(x_ref, g_ref, o_ref):
    o_ref[...] = _rms(x_ref[...], g_ref[...]).astype(o_ref.dtype)


def _prenorm(x, gain, tr=256):
    m, d = x.shape
    tr = _tile(m, tr)
    return pl.pallas_call(
        _prenorm_kernel,
        grid=(m // tr,),
        in_specs=[pl.BlockSpec((tr, d), lambda i: (i, 0)), pl.BlockSpec((1, d), lambda i: (0, 0))],
        out_specs=pl.BlockSpec((tr, d), lambda i: (i, 0)),
        out_shape=jax.ShapeDtypeStruct((m, d), BF16),
        compiler_params=_params(("parallel",)),
        name="prenorm",
    )(x, gain.reshape(1, d))


def _prenorm_dilated_kernel(x_ref, g_ref, *refs, dilations, has_head):
    w_ref = refs[0] if has_head else None
    out_refs = refs[has_head : has_head + len(dilations)]
    head_ref = refs[has_head + len(dilations)] if has_head else None
    y_ref = refs[-1]
    y = _rms(x_ref[...], g_ref[...])
    tr = y.shape[0]
    if has_head:
        head_ref[...] = _dot(y.astype(BF16), w_ref[...]).astype(head_ref.dtype)
    for c in range(y_ref.shape[0]):
        cols = slice(c * 128, (c + 1) * 128)
        y_ref[c] = y[:, cols]
        for o_ref, d in zip(out_refs, dilations):
            for r in range(d):
                o_ref[r, :, cols] = y_ref[c, _strided_rows(r, tr // d, d), :].astype(o_ref.dtype)


def _prenorm_dilated(x, gain, batch, seq, dilations, w_head=None, tr=256):
    m, d = x.shape
    tr = _tile(seq, tr)
    nt = seq // tr
    has_head = w_head is not None
    head_in = [pl.BlockSpec(w_head.shape, lambda i: (0, 0), pipeline_mode=pl.Buffered(1))] if has_head else []
    head_out = [pl.BlockSpec((tr, w_head.shape[1]), lambda i: (i, 0))] if has_head else []
    head_shape = [jax.ShapeDtypeStruct((m, w_head.shape[1]), BF16)] if has_head else []
    outs = pl.pallas_call(
        functools.partial(_prenorm_dilated_kernel, dilations=dilations, has_head=has_head),
        grid=(m // tr,),
        in_specs=[pl.BlockSpec((tr, d), lambda i: (i, 0)), pl.BlockSpec((1, d), lambda i: (0, 0))] + head_in,
        out_specs=[pl.BlockSpec((None, dil, tr // dil, d), lambda i: (i // nt, 0, i % nt, 0)) for dil in dilations]
        + head_out,
        out_shape=[jax.ShapeDtypeStruct((batch, dil, seq // dil, d), BF16) for dil in dilations] + head_shape,
        scratch_shapes=[pltpu.VMEM((d // 128, tr, 128), F32)],
        compiler_params=_params(("parallel",)),
        name="prenorm_dilated",
    )(x, gain.reshape(1, d), *([w_head] if has_head else []))
    return [o.reshape(m, d) for o in outs[: len(dilations)]], (outs[-1] if has_head else None)


def _postnorm_kernel(h_ref, y_ref, ga_ref, gb_ref, h_out_ref, yn_ref):
    h = h_ref[...] + _rms(y_ref[...], ga_ref[...])
    h_out_ref[...] = h
    yn_ref[...] = _rms(h, gb_ref[...]).astype(yn_ref.dtype)


def _postnorm_last_kernel(h_ref, y_ref, ga_ref, h_out_ref):
    h_out_ref[...] = h_ref[...] + _rms(y_ref[...], ga_ref[...])


def _postnorm(h, y, gain_post, gain_next, tr=256):
    m, d = h.shape
    tr = _tile(m, tr)
    row = pl.BlockSpec((tr, d), lambda i: (i, 0))
    vec = pl.BlockSpec((1, d), lambda i: (0, 0))
    if gain_next is None:
        return pl.pallas_call(
            _postnorm_last_kernel,
            grid=(m // tr,),
            in_specs=[row, row, vec],
            out_specs=row,
            out_shape=jax.ShapeDtypeStruct((m, d), F32),
            compiler_params=_params(("parallel",)),
            name="postnorm_last",
        )(h, y, gain_post.reshape(1, d))
    return pl.pallas_call(
        _postnorm_kernel,
        grid=(m // tr,),
        in_specs=[row, row, vec, vec],
        out_specs=[row, row],
        out_shape=[jax.ShapeDtypeStruct((m, d), F32), jax.ShapeDtypeStruct((m, d), BF16)],
        compiler_params=_params(("parallel",)),
        name="postnorm",
    )(h, y, gain_post.reshape(1, d), gain_next.reshape(1, d))


def _postnorm_head_kernel(h_ref, y_ref, ga_ref, gb_ref, *refs, n_w, epilogue):
    w_refs = refs[:n_w]
    h_out_ref, yn_ref, head_ref = refs[n_w:]
    h = h_ref[...] + _rms(y_ref[...], ga_ref[...])
    h_out_ref[...] = h
    yn = _rms(h, gb_ref[...]).astype(yn_ref.dtype)
    yn_ref[...] = yn
    head_ref[...] = epilogue(*[_dot(yn, w_ref[...]) for w_ref in w_refs]).astype(head_ref.dtype)


def _postnorm_head(h, y, gain_post, gain_next, w_heads, epilogue, out_dtype, tr=256, name="postnorm_head"):
    m, d = h.shape
    tr = _tile(m, tr)
    head_cols = w_heads[0].shape[1]
    row = pl.BlockSpec((tr, d), lambda i: (i, 0))
    vec = pl.BlockSpec((1, d), lambda i: (0, 0))
    w_spec = pl.BlockSpec((d, head_cols), lambda i: (0, 0), pipeline_mode=pl.Buffered(1))
    return pl.pallas_call(
        functools.partial(_postnorm_head_kernel, n_w=len(w_heads), epilogue=epilogue),
        grid=(m // tr,),
        in_specs=[row, row, vec, vec] + [w_spec] * len(w_heads),
        out_specs=[row, row, pl.BlockSpec((tr, head_cols), lambda i: (i, 0))],
        out_shape=[jax.ShapeDtypeStruct((m, d), F32), jax.ShapeDtypeStruct((m, d), BF16),
                   jax.ShapeDtypeStruct((m, head_cols), out_dtype)],
        compiler_params=_params(("parallel",)),
        name=name,
    )(h, y, gain_post.reshape(1, d), gain_next.reshape(1, d), *w_heads)


def _matmul_kernel(x_ref, w_ref, o_ref):
    o_ref[...] = _dot(x_ref[...], w_ref[...]).astype(o_ref.dtype)


def _matmul(x, w, layer, out_dtype, bm=1024, bn=1024, name="matmul"):
    m, k = x.shape
    n = w.shape[2]
    bm, bn = _tile(m, bm), _tile(n, bn)
    return pl.pallas_call(
        _matmul_kernel,
        grid=(m // bm, n // bn),
        in_specs=[pl.BlockSpec((bm, k), lambda i, j: (i, 0)), pl.BlockSpec((None, k, bn), lambda i, j: (layer, 0, j))],
        out_specs=pl.BlockSpec((bm, bn), lambda i, j: (i, j)),
        out_shape=jax.ShapeDtypeStruct((m, n), out_dtype),
        compiler_params=_params(("parallel", "arbitrary")),
        name=name,
    )(x, w)


def _proj_kernel(*refs, n_w, has_aux, has_head, has_side, copy_steps, epilogue):
    x_ref, w_refs = refs[0], refs[1 : 1 + n_w]
    aux_ref = refs[1 + n_w] if has_aux else None
    head_ref = refs[1 + n_w + has_aux] if has_head else None
    n_in = 1 + n_w + has_aux + has_head + has_side
    o_ref = refs[n_in]
    wb_refs = refs[n_in + 1 + has_side :]
    jj, i = pl.program_id(0), pl.program_id(1)
    j0 = max(copy_steps - 1, 0)
    slab = w_refs[0].shape[0]
    if has_side:
        refs[n_in + 1][...] = refs[n_in - 1][...].astype(refs[n_in + 1].dtype)

    if has_head:
        @pl.when(jj <= j0)
        def _():
            o_ref[...] = head_ref[...]

    @pl.when((jj >= j0) & (jj < pl.num_programs(0) - 1))
    def _():
        rows = pl.ds(pl.multiple_of(i * slab, slab), slab)
        for w_ref, wb_ref in zip(w_refs, wb_refs):
            wb_ref[(jj - j0) % 2, rows, :] = w_ref[...].astype(wb_ref.dtype)

    @pl.when(jj > j0)
    def _():
        chunk = x_ref.shape[0] // PROJ_ROW_CHUNKS
        for r in range(PROJ_ROW_CHUNKS):
            rows = slice(r * chunk, (r + 1) * chunk)
            x = x_ref[rows, :]
            accs = [_dot(x, wb_ref[(jj - j0 - 1) % 2]) for wb_ref in wb_refs]
            out = epilogue(*accs, aux_ref[...]) if has_aux else epilogue(*accs)
            o_ref[rows, :] = out.astype(o_ref.dtype)


def _proj(x, w, layer, col_offsets, width, epilogue, out_dtype, aux=None, bm=1024, bn=1024, name="proj", head=None,
          side_cast=None):
    m, k = x.shape
    bm, bn = _tile(m, bm), _tile(width, bn)
    n_w = len(col_offsets)
    col0 = 0 if head is None else head.shape[1]
    kh = col0 // bn
    j0 = max(kh - 1, 0)
    nj, ni = (width - col0) // bn, m // bm
    slab = k // ni
    assert k % ni == 0 and slab % 16 == 0, (k, ni)
    assert all(c % bn == 0 for c in col_offsets) and col0 % bn == 0, (col_offsets, col0, bn)

    def staging(jj):
        return (jj >= j0) & (jj < j0 + nj)

    def w_map(jj, i, c):
        row = jnp.where(jj < j0, 0, jnp.where(staging(jj), i, ni - 1))
        return (layer, row, jnp.clip(jj - j0, 0, nj - 1) + c + kh)

    def x_map(jj, i):
        return (jnp.where(jj > j0, i, 0), 0)

    def computed_col(jj):
        return kh + jnp.maximum(jj - j0 - 1, 0)

    def out_map(jj, i):
        if kh == 0:
            return (jnp.where(jj > 0, i, 0), computed_col(jj))
        return (i, jnp.where(jj <= j0, jj, computed_col(jj)))

    w_specs = [pl.BlockSpec((None, slab, bn), functools.partial(w_map, c=c // bn)) for c in col_offsets]
    aux_specs = [] if aux is None else [pl.BlockSpec((aux.shape[0], bn), lambda jj, i: (0, computed_col(jj)))]
    head_specs = [] if head is None else [
        pl.BlockSpec((bm, bn), lambda jj, i: (jnp.where(jj <= j0, i, ni - 1), jnp.minimum(jj, kh - 1)))]
    steps = (j0 + nj + 1) * ni
    side_in, side_out, side_shape, side_args = [], [], [], []
    if side_cast is not None:
        w2, layer2, rb = side_cast
        rows2, cols2 = w2.shape[1:]
        blocks = rows2 // rb
        assert rows2 % rb == 0 and blocks <= steps, (rows2, rb, steps)

        def side_block(jj, i):
            return jnp.minimum(jj * ni + i, blocks - 1)

        side_in = [pl.BlockSpec((None, rb, cols2), lambda jj, i: (layer2, side_block(jj, i), 0))]
        side_out = [pl.BlockSpec((rb, cols2), lambda jj, i: (side_block(jj, i), 0))]
        side_shape = [jax.ShapeDtypeStruct((rows2, cols2), BF16)]
        side_args = [w2]
    result = pl.pallas_call(
        functools.partial(_proj_kernel, n_w=n_w, has_aux=aux is not None, has_head=head is not None,
                          has_side=side_cast is not None, copy_steps=kh, epilogue=epilogue),
        grid=(j0 + nj + 1, ni),
        in_specs=[pl.BlockSpec((bm, k), x_map)] + w_specs + aux_specs + head_specs + side_in,
        out_specs=[pl.BlockSpec((bm, bn), out_map)] + side_out,
        out_shape=[jax.ShapeDtypeStruct((m, width), out_dtype)] + side_shape,
        scratch_shapes=[pltpu.VMEM((2, k, bn), BF16) for _ in range(n_w)],
        compiler_params=_params(("arbitrary", "arbitrary")),
        name=name,
    )(x, *([w] * n_w), *([] if aux is None else [aux]), *([] if head is None else [head]), *side_args)
    return result if side_cast is not None else result[0]


def _identity(acc):
    return acc


def _swiglu(gate, up):
    return _silu(gate) * up


def _t5_bucket_np(dist):
    dist = np.maximum(dist, 0)
    log_ratio = np.log(np.maximum(dist, 1).astype(np.float64) / MAX_EXACT) / math.log(REL_MAX_DISTANCE / MAX_EXACT)
    large = np.minimum(MAX_EXACT + (log_ratio * (NUM_BUCKETS - MAX_EXACT)).astype(np.int64), NUM_BUCKETS - 1)
    return np.where(dist < MAX_EXACT, dist, large)


def _bucket_tables():
    qi = np.arange(ATTN_BLOCK)[:, None]
    ki = np.arange(2 * ATTN_BLOCK)[None, :]
    rel = qi - ki + ATTN_BLOCK
    tables = []
    for window, dilation in DILATED_GROUPS:
        band = (rel >= 0) & (rel <= window // dilation)
        tables.append(np.where(band, _t5_bucket_np(rel * dilation), -1))
    return np.stack(tables).astype(np.int32)


def _bias_kernel(rb_ref, bucket_ref, o_ref, *, heads):
    g, h = pl.program_id(0), pl.program_id(1)
    bucket = bucket_ref[...]
    acc = jnp.full(bucket.shape, NEG_INF, F32)
    for b in range(NUM_BUCKETS):
        acc = jnp.where(bucket == b, rb_ref[b, g * heads + h], acc)
    o_ref[...] = acc


def _bias_tables(rel_bias, heads):
    ng = len(DILATED_GROUPS)
    buckets = jnp.asarray(_bucket_tables())
    return pl.pallas_call(
        functools.partial(_bias_kernel, heads=heads),
        grid=(ng, heads),
        in_specs=[
            pl.BlockSpec(memory_space=pltpu.SMEM),
            pl.BlockSpec((None, ATTN_BLOCK, 2 * ATTN_BLOCK), lambda g, h: (g, 0, 0)),
        ],
        out_specs=pl.BlockSpec((None, None, ATTN_BLOCK, 2 * ATTN_BLOCK), lambda g, h: (g, h, 0, 0)),
        out_shape=jax.ShapeDtypeStruct((ng, heads, ATTN_BLOCK, 2 * ATTN_BLOCK), F32),
        compiler_params=_params(("arbitrary", "arbitrary")),
        name="attn_bias",
    )(rel_bias, buckets)


def _attn_kernel(q_ref, kc_ref, kp_ref, vc_ref, vp_ref, bias_ref, o_ref, lse_ref, *, heads):
    has_prev = pl.program_id(1) > 0
    scale = HEAD_DIM ** -0.5
    lane = lax.broadcasted_iota(jnp.int32, (ATTN_BLOCK, ATTN_BLOCK), 1)
    lse_all = jnp.zeros((ATTN_BLOCK, ATTN_BLOCK), F32)
    ones = jnp.ones((ATTN_BLOCK, HEAD_DIM), BF16)
    for h in range(heads):
        cols = slice(h * HEAD_DIM, (h + 1) * HEAD_DIM)
        q = q_ref[:, cols]
        s_prev = _dot_nt(q, kp_ref[:, cols]) * scale + bias_ref[h, :, :ATTN_BLOCK]
        s_prev = jnp.where(has_prev, s_prev, NEG_INF)
        s_cur = _dot_nt(q, kc_ref[:, cols]) * scale + bias_ref[h, :, ATTN_BLOCK:]
        m = jnp.maximum(jnp.max(s_prev, axis=-1, keepdims=True), jnp.max(s_cur, axis=-1, keepdims=True))
        p_prev = jnp.exp(s_prev - m).astype(BF16)
        p_cur = jnp.exp(s_cur - m).astype(BF16)
        ov = (_dot(p_prev, jnp.concatenate([vp_ref[:, cols], ones], axis=1))
              + _dot(p_cur, jnp.concatenate([vc_ref[:, cols], ones], axis=1)))
        denom = ov[:, HEAD_DIM:]
        o_ref[:, cols] = (ov[:, :HEAD_DIM] / denom).astype(o_ref.dtype)
        lse_all = jnp.where(lane == h, m + jnp.log(denom[:, 0:1]), lse_all)
    lse_ref[...] = lse_all


def _attn_group(proj, bias_g, sub_len, heads):
    rows = proj.shape[0]
    width = heads * HEAD_DIM
    nb = sub_len // ATTN_BLOCK
    blk = (ATTN_BLOCK, width)

    def cur(which):
        return lambda s, i: (s * nb + i, which)

    def prev(which):
        return lambda s, i: (s * nb + jnp.maximum(i - 1, 0), which)

    return pl.pallas_call(
        functools.partial(_attn_kernel, heads=heads),
        grid=(rows // sub_len, nb),
        in_specs=[
            pl.BlockSpec(blk, cur(0)),
            pl.BlockSpec(blk, cur(1)),
            pl.BlockSpec(blk, prev(1)),
            pl.BlockSpec(blk, cur(2)),
            pl.BlockSpec(blk, prev(2)),
            pl.BlockSpec((heads, ATTN_BLOCK, 2 * ATTN_BLOCK), lambda s, i: (0, 0, 0)),
        ],
        out_specs=[
            pl.BlockSpec(blk, lambda s, i: (s * nb + i, 0)),
            pl.BlockSpec((ATTN_BLOCK, ATTN_BLOCK), lambda s, i: (s * nb + i, 0)),
        ],
        out_shape=[jax.ShapeDtypeStruct((rows, width), BF16), jax.ShapeDtypeStruct((rows, ATTN_BLOCK), F32)],
        compiler_params=_params(("parallel", "arbitrary")),
        name="attn_group",
    )(proj, proj, proj, proj, proj, bias_g)


def _merge_outproj_kernel(*refs, heads, dilations):
    ng = len(dilations)
    o_refs, l_refs = refs[:ng], refs[ng : 2 * ng]
    w_ref, h_ref, ga_ref, gb_ref, h_out_ref, yn_ref, merged_even_ref, merged_odd_ref, alpha_ref, acc_ref = refs[2 * ng :]
    bm = acc_ref.shape[1]
    ii = pl.program_id(0)

    @pl.when(ii == 0)
    def _():
        merged_odd_ref[...] = jnp.zeros_like(merged_odd_ref)

    def step(merge_into_ref, multiply_ref):
        for g, d in enumerate(dilations):
            for r in range(d):
                alpha_ref[g, _strided_rows(r, bm // d, d), :] = l_refs[g][r]
        lses = [alpha_ref[g] for g in range(ng)]
        mx = functools.reduce(jnp.maximum, lses)
        es = [jnp.exp(l - mx) for l in lses]
        inv = 1.0 / functools.reduce(jnp.add, es)
        for g in range(ng):
            alpha_ref[g] = es[g] * inv
        for g, d in enumerate(dilations):
            for r in range(d):
                rows = _strided_rows(r, bm // d, d)
                alpha = alpha_ref[g, rows, :]
                for h in range(heads):
                    cols = slice(h * HEAD_DIM, (h + 1) * HEAD_DIM)
                    term = alpha[:, h : h + 1] * o_refs[g][r, :, cols].astype(F32)
                    if g == 0:
                        acc_ref[h, rows, :] = term
                    else:
                        acc_ref[h, rows, :] += term
        for h in range(heads):
            merge_into_ref[:, h * HEAD_DIM : (h + 1) * HEAD_DIM] = acc_ref[h].astype(merge_into_ref.dtype)

        y = _dot(multiply_ref[...], w_ref[...])
        h_new = h_ref[...] + _rms(y, ga_ref[...])
        h_out_ref[...] = h_new
        yn_ref[...] = _rms(h_new, gb_ref[...]).astype(yn_ref.dtype)

    pl.when(ii % 2 == 0)(functools.partial(step, merged_even_ref, merged_odd_ref))
    pl.when(ii % 2 == 1)(functools.partial(step, merged_odd_ref, merged_even_ref))


def _merge_outproj(outs, lses, w_out, h, gain_post, gain_next, heads, batch, seq, dilations, bm=256):
    width = heads * HEAD_DIM
    n = w_out.shape[1]
    bm = _tile(seq, bm)
    nt = seq // bm
    tiles = batch * nt

    def in_map(ii):
        t = jnp.minimum(ii, tiles - 1)
        return (t // nt, 0, t % nt, 0)

    row = pl.BlockSpec((bm, n), lambda ii: (jnp.maximum(ii - 1, 0), 0))
    vec = pl.BlockSpec((1, n), lambda ii: (0, 0))
    o_specs = [pl.BlockSpec((None, d, bm // d, width), in_map) for d in dilations]
    l_specs = [pl.BlockSpec((None, d, bm // d, ATTN_BLOCK), in_map) for d in dilations]
    outs = [o.reshape(batch, d, seq // d, width) for o, d in zip(outs, dilations)]
    lses = [l.reshape(batch, d, seq // d, ATTN_BLOCK) for l, d in zip(lses, dilations)]
    return pl.pallas_call(
        functools.partial(_merge_outproj_kernel, heads=heads, dilations=dilations),
        grid=(tiles + 1,),
        in_specs=o_specs + l_specs + [pl.BlockSpec((width, n), lambda ii: (0, 0), pipeline_mode=pl.Buffered(1)),
                                      row, vec, vec],
        out_specs=[row, row],
        out_shape=[jax.ShapeDtypeStruct((batch * seq, n), F32), jax.ShapeDtypeStruct((batch * seq, n), BF16)],
        scratch_shapes=[
            pltpu.VMEM((bm, width), BF16),
            pltpu.VMEM((bm, width), BF16),
            pltpu.VMEM((len(dilations), bm, ATTN_BLOCK), F32),
            pltpu.VMEM((heads, bm, HEAD_DIM), F32),
        ],
        compiler_params=_params(("arbitrary",)),
        name="attn_merge_outproj",
    )(*outs, *lses, w_out, h, gain_post.reshape(1, n), gain_next.reshape(1, n))


def _hgrn_query(acc):
    return _silu(acc) * (HEAD_DIM ** -0.5)


def _hgrn_forget(acc, lb_logits, *, layer):
    p = jnp.exp(lb_logits - jnp.max(lb_logits, axis=0, keepdims=True))
    p = p / jnp.sum(p, axis=0, keepdims=True)
    lb = jnp.sum(p[: layer + 1], axis=0, keepdims=True) - p[0:1]
    return lb + (1.0 - lb) * jax.nn.sigmoid(acc)


def _hgrn_rec_kernel(q_ref, f_ref, v_ref, g_ref, gain_ref, o_ref, state_ref, b_ref, *, heads_per_step, chunks):
    @pl.when(pl.program_id(2) == 0)
    def _():
        state_ref[...] = jnp.zeros_like(state_ref)

    row = lax.broadcasted_iota(jnp.int32, (HG_CHUNK, HG_CHUNK), 0)
    col = lax.broadcasted_iota(jnp.int32, (HG_CHUNK, HG_CHUNK), 1)
    causal = row >= col
    tri = causal.astype(BF16)
    tri2 = jnp.concatenate([tri, tri], axis=1)
    mid = HG_CHUNK // 2
    gain = gain_ref[...]
    heads = range(heads_per_step)
    cols = [slice(h * HEAD_DIM, (h + 1) * HEAD_DIM) for h in heads]

    for c in range(chunks):
        rows = pl.ds(c * HG_CHUNK, HG_CHUNK)
        log_f = jnp.log(f_ref[rows, :])
        hi = log_f.astype(BF16)
        lo = (log_f - hi.astype(F32)).astype(BF16)
        b_ref[rows, :] = _dot(tri2, jnp.concatenate([hi, lo], axis=0))

    def chunk(c, carry):
        rows = pl.ds(pl.multiple_of(c * HG_CHUNK, HG_CHUNK), HG_CHUNK)
        b = [b_ref[rows, cols[h]] for h in heads]
        b_mid = [b[h][mid - 1 : mid, :] for h in heads]
        b_last = [b[h][HG_CHUNK - 1 :, :] for h in heads]
        key = [1.0 - f_ref[rows, cols[h]] for h in heads]
        q = [q_ref[rows, cols[h]] for h in heads]
        v = [v_ref[rows, cols[h]] for h in heads]
        scores = [_dot_nt((q[h] * jnp.exp(b[h] - b_mid[h])).astype(BF16),
                          (key[h] * jnp.exp(b_mid[h] - b[h])).astype(BF16)) for h in heads]
        probs = [jnp.where(causal, scores[h], 0.0).astype(BF16) for h in heads]
        state = [state_ref[h] for h in heads]
        o = [_dot(jnp.concatenate([(q[h] * jnp.exp(b[h])).astype(BF16), probs[h]], axis=1),
                  jnp.concatenate([state[h].astype(BF16), v[h]], axis=0)) for h in heads]
        update = [_dot_tn((key[h] * jnp.exp(b_last[h] - b[h])).astype(BF16), v[h]) for h in heads]
        for h in heads:
            decay = jnp.transpose(jnp.broadcast_to(jnp.exp(b_last[h]), (8, HEAD_DIM)))[:, 0:1]
            state_ref[h] = state[h] * decay + update[h]
        for h in heads:
            o_ref[rows, cols[h]] = (_rms(o[h], gain) * g_ref[rows, cols[h]]).astype(o_ref.dtype)
        return carry

    lax.fori_loop(0, chunks, chunk, 0)


def _hgrn_recurrence(q, f, v, g, out_gain, batch, seq, heads_per_step=16, ts=512):
    m, width = q.shape
    heads = width // HEAD_DIM
    heads_per_step = math.gcd(heads, heads_per_step)
    ts = _tile(seq, ts)
    bw = heads_per_step * HEAD_DIM
    nt = seq // ts
    spec = pl.BlockSpec((ts, bw), lambda b, hg, t: (b * nt + t, hg))
    return pl.pallas_call(
        functools.partial(_hgrn_rec_kernel, heads_per_step=heads_per_step, chunks=ts // HG_CHUNK),
        grid=(batch, heads // heads_per_step, nt),
        in_specs=[spec, spec, spec, spec, pl.BlockSpec((1, HEAD_DIM), lambda b, hg, t: (0, 0))],
        out_specs=spec,
        out_shape=jax.ShapeDtypeStruct((m, width), BF16),
        scratch_shapes=[pltpu.VMEM((heads_per_step, HEAD_DIM, HEAD_DIM), F32), pltpu.VMEM((ts, bw), F32)],
        compiler_params=_params(("parallel", "parallel", "arbitrary")),
        name="hgrn_recurrence",
    )(q, f, v, g, out_gain.reshape(1, HEAD_DIM))


def _cast_slice(w_slice):
    return lax.optimization_barrier(w_slice).astype(BF16)


def _side_rows(rows, steps):
    rb = 16
    while rows % rb or rows // rb > steps:
        rb += 16
    return rb


def _head_cols(width, bn, most):
    return min(most // bn, width // bn // 2) * bn


def kernel(x, norm_gains, rel_bias, attn_w_in, attn_w_out, hgrn_w_in, hgrn_lb_logits, hgrn_out_gain, hgrn_w_out,
           ffn_w_in, ffn_w_out):
    batch, seq, d = x.shape
    m = batch * seq
    depth = norm_gains.shape[0]
    attn_heads = attn_w_out.shape[1] // HEAD_DIM
    attn_width = attn_heads * HEAD_DIM
    dilations = tuple(dil for _, dil in DILATED_GROUPS)
    d_ff = ffn_w_out.shape[1]
    hg_width = hgrn_w_in.shape[2] // 4
    ffn_head = _head_cols(d_ff, FFN_BN, FFN_HEAD_COLS)
    hg_bn = _tile(hg_width, PROJ_BN)
    hg_head = _head_cols(hg_width, hg_bn, HGRN_HEAD_COLS)
    attn_head = _head_cols(3 * attn_width, _tile(3 * attn_width, PROJ_BN), ATTN_HEAD_COLS)
    h = x.reshape(m, d)
    hgrn_w_out_bf16 = hgrn_w_out.astype(BF16)
    yn = q_head = hidden_head = None
    bias = None
    for i in range(depth):
        gains = norm_gains[i]
        if i % 2 == 0:
            if bias is None:
                bias = _bias_tables(rel_bias, attn_heads)
            g_head = dilations.index(1) if attn_head and 1 in dilations else None
            w_head = None if g_head is None else _cast_slice(
                attn_w_in[i // 2, :, g_head * 3 * attn_width : g_head * 3 * attn_width + attn_head])
            yns, proj_head = _prenorm_dilated(h, gains[0], batch, seq, dilations, w_head)
            outs, lses = [], []
            for g, dil in enumerate(dilations):
                proj = _proj(yns[g], attn_w_in, i // 2, [g * 3 * attn_width], 3 * attn_width, _identity, BF16,
                             bn=PROJ_BN, name="attn_in", head=proj_head if g == g_head else None)
                o, l = _attn_group(proj, bias[g], seq // dil, attn_heads)
                outs.append(o)
                lses.append(l)
            h, yn = _merge_outproj(outs, lses, attn_w_out[i // 2].astype(BF16), h, gains[1], gains[2], attn_heads,
                                   batch, seq, dilations)
        else:
            if yn is None:
                yn = _prenorm(h, gains[0])
            hproj = functools.partial(_proj, yn, hgrn_w_in, i // 2, bn=PROJ_BN)
            q = hproj([0], hg_width, _hgrn_query, F32, name="hgrn_in_q", head=q_head)
            f = hproj([hg_width], hg_width, functools.partial(_hgrn_forget, layer=i), F32, aux=hgrn_lb_logits,
                      name="hgrn_in_f")
            v = hproj([2 * hg_width], hg_width, _identity, BF16, name="hgrn_in_i")
            g = hproj([3 * hg_width], hg_width, _silu, F32, name="hgrn_in_g")
            o = _hgrn_recurrence(q, f, v, g, hgrn_out_gain[i // 2], batch, seq)
            y = _matmul(o, hgrn_w_out_bf16, i // 2, F32, name="hgrn_out")
            if ffn_head:
                w_heads = [_cast_slice(ffn_w_in[i, :, c : c + ffn_head]) for c in (0, d_ff)]
                h, yn, hidden_head = _postnorm_head(h, y, gains[1], gains[2], w_heads, _swiglu, BF16,
                                                    name="postnorm_ffn_head")
            else:
                h, yn = _postnorm(h, y, gains[1], gains[2])
        ffn_steps = (d_ff // FFN_BN + 1 - (hidden_head is not None)) * (m // _tile(m, 2048))
        hidden, w_out_bf16 = _proj(yn, ffn_w_in, i, [0, d_ff], d_ff, _swiglu, BF16, bm=2048, bn=FFN_BN,
                                   name="ffn_in", head=hidden_head,
                                   side_cast=(ffn_w_out, i, _side_rows(d_ff, ffn_steps)))
        y = _matmul(hidden, w_out_bf16[None], 0, F32, bm=512, bn=512, name="ffn_out")
        yn = q_head = hidden_head = None
        if i + 1 < depth and (i + 1) % 2 == 1:
            if hg_head:
                w_heads = [_cast_slice(hgrn_w_in[(i + 1) // 2, :, :hg_head])]
                h, yn, q_head = _postnorm_head(h, y, gains[3], norm_gains[i + 1, 0], w_heads, _hgrn_query, F32,
                                               name="postnorm_hgrn_head")
            else:
                h, yn = _postnorm(h, y, gains[3], norm_gains[i + 1, 0])
        else:
            h = _postnorm(h, y, gains[3], None)
    return h.reshape(batch, seq, d)
```

```python
import functools
import math

import numpy as np
import jax
import jax.numpy as jnp
from jax import lax
from jax.experimental import pallas as pl
from jax.experimental.pallas import tpu as pltpu

RMS_EPS = 1e-6
NEG_INF = -1e30

DILATED_GROUPS = ((128, 1), (512, 4), (2048, 16))
HEAD_DIM = 128
ATTN_BLOCK = 128
NUM_BUCKETS = 32
MAX_EXACT = 16
REL_MAX_DISTANCE = 2048
HG_CHUNK = 64
PROJ_ROW_CHUNKS = 4
PROJ_BN = 1024
FFN_BN = 256
FFN_HEAD_COLS = 768
HGRN_HEAD_COLS = 1024
ATTN_HEAD_COLS = 1024

V7X_VMEM_BYTES = 64 * 1024 * 1024
VMEM_LIMIT = V7X_VMEM_BYTES - 8 * 1024 * 1024

BF16 = jnp.bfloat16
F32 = jnp.float32


def _params(semantics, vmem=VMEM_LIMIT):
    return pltpu.CompilerParams(dimension_semantics=semantics, vmem_limit_bytes=vmem)


def _tile(n, pref):
    if n <= pref:
        return n
    t = pref - pref % 128
    while t >= 128:
        if n % t == 0:
            return t
        t -= 128
    raise ValueError(f"no 128-multiple tile of {n} below {pref}")


def _dot(a, b):
    return jnp.dot(a, b, preferred_element_type=F32)


def _dot_nt(a, b):
    return lax.dot_general(a, b, (((1,), (1,)), ((), ())), preferred_element_type=F32)


def _dot_tn(a, b):
    return lax.dot_general(a, b, (((0,), (0,)), ((), ())), preferred_element_type=F32)


def _rms(x, gain):
    return x * lax.rsqrt(jnp.mean(x * x, axis=-1, keepdims=True) + RMS_EPS) * gain


def _silu(x):
    return x * jax.nn.sigmoid(x)


def _strided_rows(r, count, stride):
    return pl.ds(r, count) if stride == 1 else pl.ds(r, count, stride=stride)


def _prenorm_kernel(x_ref, g_ref, o_ref):
    o_ref[...] = _rms(x_ref[...], g_ref[...]).astype(o_ref.dtype)


def _prenorm(x, gain, tr=256):
    m, d = x.shape
    tr = _tile(m, tr)
    return pl.pallas_call(
        _prenorm_kernel,
        grid=(m // tr,),
        in_specs=[pl.BlockSpec((tr, d), lambda i: (i, 0)), pl.BlockSpec((1, d), lambda i: (0, 0))],
        out_specs=pl.BlockSpec((tr, d), lambda i: (i, 0)),
        out_shape=jax.ShapeDtypeStruct((m, d), BF16),
        compiler_params=_params(("parallel",)),
        name="prenorm",
    )(x, gain.reshape(1, d))


def _prenorm_dilated_kernel(x_ref, g_ref, *refs, dilations, has_head):
    w_ref = refs[0] if has_head else None
    out_refs = refs[has_head : has_head + len(dilations)]
    head_ref = refs[has_head + len(dilations)] if has_head else None
    y_ref = refs[-1]
    y = _rms(x_ref[...], g_ref[...])
    tr = y.shape[0]
    if has_head:
        head_ref[...] = _dot(y.astype(BF16), w_ref[...]).astype(head_ref.dtype)
    for c in range(y_ref.shape[0]):
        cols = slice(c * 128, (c + 1) * 128)
        y_ref[c] = y[:, cols]
        for o_ref, d in zip(out_refs, dilations):
            for r in range(d):
                o_ref[r, :, cols] = y_ref[c, _strided_rows(r, tr // d, d), :].astype(o_ref.dtype)


def _prenorm_dilated(x, gain, batch, seq, dilations, w_head=None, tr=256):
    m, d = x.shape
    tr = _tile(seq, tr)
    nt = seq // tr
    has_head = w_head is not None
    head_in = [pl.BlockSpec(w_head.shape, lambda i: (0, 0), pipeline_mode=pl.Buffered(1))] if has_head else []
    head_out = [pl.BlockSpec((tr, w_head.shape[1]), lambda i: (i, 0))] if has_head else []
    head_shape = [jax.ShapeDtypeStruct((m, w_head.shape[1]), BF16)] if has_head else []
    outs = pl.pallas_call(
        functools.partial(_prenorm_dilated_kernel, dilations=dilations, has_head=has_head),
        grid=(m // tr,),
        in_specs=[pl.BlockSpec((tr, d), lambda i: (i, 0)), pl.BlockSpec((1, d), lambda i: (0, 0))] + head_in,
        out_specs=[pl.BlockSpec((None, dil, tr // dil, d), lambda i: (i // nt, 0, i % nt, 0)) for dil in dilations]
        + head_out,
        out_shape=[jax.ShapeDtypeStruct((batch, dil, seq // dil, d), BF16) for dil in dilations] + head_shape,
        scratch_shapes=[pltpu.VMEM((d // 128, tr, 128), F32)],
        compiler_params=_params(("parallel",)),
        name="prenorm_dilated",
    )(x, gain.reshape(1, d), *([w_head] if has_head else []))
    return [o.reshape(m, d) for o in outs[: len(dilations)]], (outs[-1] if has_head else None)


def _postnorm_kernel(h_ref, y_ref, ga_ref, gb_ref, h_out_ref, yn_ref):
    h = h_ref[...] + _rms(y_ref[...], ga_ref[...])
    h_out_ref[...] = h
    yn_ref[...] = _rms(h, gb_ref[...]).astype(yn_ref.dtype)


def _postnorm_last_kernel(h_ref, y_ref, ga_ref, h_out_ref):
    h_out_ref[...] = h_ref[...] + _rms(y_ref[...], ga_ref[...])


def _postnorm(h, y, gain_post, gain_next, tr=256):
    m, d = h.shape
    tr = _tile(m, tr)
    row = pl.BlockSpec((tr, d), lambda i: (i, 0))
    vec = pl.BlockSpec((1, d), lambda i: (0, 0))
    if gain_next is None:
        return pl.pallas_call(
            _postnorm_last_kernel,
            grid=(m // tr,),
            in_specs=[row, row, vec],
            out_specs=row,
            out_shape=jax.ShapeDtypeStruct((m, d), F32),
            compiler_params=_params(("parallel",)),
            name="postnorm_last",
        )(h, y, gain_post.reshape(1, d))
    return pl.pallas_call(
        _postnorm_kernel,
        grid=(m // tr,),
        in_specs=[row, row, vec, vec],
        out_specs=[row, row],
        out_shape=[jax.ShapeDtypeStruct((m, d), F32), jax.ShapeDtypeStruct((m, d), BF16)],
        compiler_params=_params(("parallel",)),
        name="postnorm",
    )(h, y, gain_post.reshape(1, d), gain_next.reshape(1, d))


def _postnorm_head_kernel(h_ref, y_ref, ga_ref, gb_ref, *refs, n_w, epilogue):
    w_refs = refs[:n_w]
    h_out_ref, yn_ref, head_ref = refs[n_w:]
    h = h_ref[...] + _rms(y_ref[...], ga_ref[...])
    h_out_ref[...] = h
    yn = _rms(h, gb_ref[...]).astype(yn_ref.dtype)
    yn_ref[...] = yn
    head_ref[...] = epilogue(*[_dot(yn, w_ref[...]) for w_ref in w_refs]).astype(head_ref.dtype)


def _postnorm_head(h, y, gain_post, gain_next, w_heads, epilogue, out_dtype, tr=256, name="postnorm_head"):
    m, d = h.shape
    tr = _tile(m, tr)
    head_cols = w_heads[0].shape[1]
    row = pl.BlockSpec((tr, d), lambda i: (i, 0))
    vec = pl.BlockSpec((1, d), lambda i: (0, 0))
    w_spec = pl.BlockSpec((d, head_cols), lambda i: (0, 0), pipeline_mode=pl.Buffered(1))
    return pl.pallas_call(
        functools.partial(_postnorm_head_kernel, n_w=len(w_heads), epilogue=epilogue),
        grid=(m // tr,),
        in_specs=[row, row, vec, vec] + [w_spec] * len(w_heads),
        out_specs=[row, row, pl.BlockSpec((tr, head_cols), lambda i: (i, 0))],
        out_shape=[jax.ShapeDtypeStruct((m, d), F32), jax.ShapeDtypeStruct((m, d), BF16),
                   jax.ShapeDtypeStruct((m, head_cols), out_dtype)],
        compiler_params=_params(("parallel",)),
        name=name,
    )(h, y, gain_post.reshape(1, d), gain_next.reshape(1, d), *w_heads)


def _matmul_kernel(x_ref, w_ref, o_ref):
    o_ref[...] = _dot(x_ref[...], w_ref[...]).astype(o_ref.dtype)


def _matmul(x, w, layer, out_dtype, bm=1024, bn=1024, name="matmul"):
    m, k = x.shape
    n = w.shape[2]
    bm, bn = _tile(m, bm), _tile(n, bn)
    return pl.pallas_call(
        _matmul_kernel,
        grid=(m // bm, n // bn),
        in_specs=[pl.BlockSpec((bm, k), lambda i, j: (i, 0)), pl.BlockSpec((None, k, bn), lambda i, j: (layer, 0, j))],
        out_specs=pl.BlockSpec((bm, bn), lambda i, j: (i, j)),
        out_shape=jax.ShapeDtypeStruct((m, n), out_dtype),
        compiler_params=_params(("parallel", "arbitrary")),
        name=name,
    )(x, w)


def _proj_kernel(*refs, n_w, has_aux, has_head, has_side, copy_steps, epilogue):
    x_ref, w_refs = refs[0], refs[1 : 1 + n_w]
    aux_ref = refs[1 + n_w] if has_aux else None
    head_ref = refs[1 + n_w + has_aux] if has_head else None
    n_in = 1 + n_w + has_aux + has_head + has_side
    o_ref = refs[n_in]
    wb_refs = refs[n_in + 1 + has_side :]
    jj, i = pl.program_id(0), pl.program_id(1)
    j0 = max(copy_steps - 1, 0)
    slab = w_refs[0].shape[0]
    if has_side:
        refs[n_in + 1][...] = refs[n_in - 1][...].astype(refs[n_in + 1].dtype)

    if has_head:
        @pl.when(jj <= j0)
        def _():
            o_ref[...] = head_ref[...]

    @pl.when((jj >= j0) & (jj < pl.num_programs(0) - 1))
    def _():
        rows = pl.ds(pl.multiple_of(i * slab, slab), slab)
        for w_ref, wb_ref in zip(w_refs, wb_refs):
            wb_ref[(jj - j0) % 2, rows, :] = w_ref[...].astype(wb_ref.dtype)

    @pl.when(jj > j0)
    def _():
        chunk = x_ref.shape[0] // PROJ_ROW_CHUNKS
        for r in range(PROJ_ROW_CHUNKS):
            rows = slice(r * chunk, (r + 1) * chunk)
            x = x_ref[rows, :]
            accs = [_dot(x, wb_ref[(jj - j0 - 1) % 2]) for wb_ref in wb_refs]
            out = epilogue(*accs, aux_ref[...]) if has_aux else epilogue(*accs)
            o_ref[rows, :] = out.astype(o_ref.dtype)


def _proj(x, w, layer, col_offsets, width, epilogue, out_dtype, aux=None, bm=1024, bn=1024, name="proj", head=None,
          side_cast=None):
    m, k = x.shape
    bm, bn = _tile(m, bm), _tile(width, bn)
    n_w = len(col_offsets)
    col0 = 0 if head is None else head.shape[1]
    kh = col0 // bn
    j0 = max(kh - 1, 0)
    nj, ni = (width - col0) // bn, m // bm
    slab = k // ni
    assert k % ni == 0 and slab % 16 == 0, (k, ni)
    assert all(c % bn == 0 for c in col_offsets) and col0 % bn == 0, (col_offsets, col0, bn)

    def staging(jj):
        return (jj >= j0) & (jj < j0 + nj)

    def w_map(jj, i, c):
        row = jnp.where(jj < j0, 0, jnp.where(staging(jj), i, ni - 1))
        return (layer, row, jnp.clip(jj - j0, 0, nj - 1) + c + kh)

    def x_map(jj, i):
        return (jnp.where(jj > j0, i, 0), 0)

    def computed_col(jj):
        return kh + jnp.maximum(jj - j0 - 1, 0)

    def out_map(jj, i):
        if kh == 0:
            return (jnp.where(jj > 0, i, 0), computed_col(jj))
        return (i, jnp.where(jj <= j0, jj, computed_col(jj)))

    w_specs = [pl.BlockSpec((None, slab, bn), functools.partial(w_map, c=c // bn)) for c in col_offsets]
    aux_specs = [] if aux is None else [pl.BlockSpec((aux.shape[0], bn), lambda jj, i: (0, computed_col(jj)))]
    head_specs = [] if head is None else [
        pl.BlockSpec((bm, bn), lambda jj, i: (jnp.where(jj <= j0, i, ni - 1), jnp.minimum(jj, kh - 1)))]
    steps = (j0 + nj + 1) * ni
    side_in, side_out, side_shape, side_args = [], [], [], []
    if side_cast is not None:
        w2, layer2, rb = side_cast
        rows2, cols2 = w2.shape[1:]
        blocks = rows2 // rb
        assert rows2 % rb == 0 and blocks <= steps, (rows2, rb, steps)

        def side_block(jj, i):
            return jnp.minimum(jj * ni + i, blocks - 1)

        side_in = [pl.BlockSpec((None, rb, cols2), lambda jj, i: (layer2, side_block(jj, i), 0))]
        side_out = [pl.BlockSpec((rb, cols2), lambda jj, i: (side_block(jj, i), 0))]
        side_shape = [jax.ShapeDtypeStruct((rows2, cols2), BF16)]
        side_args = [w2]
    result = pl.pallas_call(
        functools.partial(_proj_kernel, n_w=n_w, has_aux=aux is not None, has_head=head is not None,
                          has_side=side_cast is not None, copy_steps=kh, epilogue=epilogue),
        grid=(j0 + nj + 1, ni),
        in_specs=[pl.BlockSpec((bm, k), x_map)] + w_specs + aux_specs + head_specs + side_in,
        out_specs=[pl.BlockSpec((bm, bn), out_map)] + side_out,
        out_shape=[jax.ShapeDtypeStruct((m, width), out_dtype)] + side_shape,
        scratch_shapes=[pltpu.VMEM((2, k, bn), BF16) for _ in range(n_w)],
        compiler_params=_params(("arbitrary", "arbitrary")),
        name=name,
    )(x, *([w] * n_w), *([] if aux is None else [aux]), *([] if head is None else [head]), *side_args)
    return result if side_cast is not None else result[0]


def _identity(acc):
    return acc


def _swiglu(gate, up):
    return _silu(gate) * up


def _t5_bucket_np(dist):
    dist = np.maximum(dist, 0)
    log_ratio = np.log(np.maximum(dist, 1).astype(np.float64) / MAX_EXACT) / math.log(REL_MAX_DISTANCE / MAX_EXACT)
    large = np.minimum(MAX_EXACT + (log_ratio * (NUM_BUCKETS - MAX_EXACT)).astype(np.int64), NUM_BUCKETS - 1)
    return np.where(dist < MAX_EXACT, dist, large)


def _bucket_tables():
    qi = np.arange(ATTN_BLOCK)[:, None]
    ki = np.arange(2 * ATTN_BLOCK)[None, :]
    rel = qi - ki + ATTN_BLOCK
    tables = []
    for window, dilation in DILATED_GROUPS:
        band = (rel >= 0) & (rel <= window // dilation)
        tables.append(np.where(band, _t5_bucket_np(rel * dilation), -1))
    return np.stack(tables).astype(np.int32)


def _bias_kernel(rb_ref, bucket_ref, o_ref, *, heads):
    g, h = pl.program_id(0), pl.program_id(1)
    bucket = bucket_ref[...]
    acc = jnp.full(bucket.shape, NEG_INF, F32)
    for b in range(NUM_BUCKETS):
        acc = jnp.where(bucket == b, rb_ref[b, g * heads + h], acc)
    o_ref[...] = acc


def _bias_tables(rel_bias, heads):
    ng = len(DILATED_GROUPS)
    buckets = jnp.asarray(_bucket_tables())
    return pl.pallas_call(
        functools.partial(_bias_kernel, heads=heads),
        grid=(ng, heads),
        in_specs=[
            pl.BlockSpec(memory_space=pltpu.SMEM),
            pl.BlockSpec((None, ATTN_BLOCK, 2 * ATTN_BLOCK), lambda g, h: (g, 0, 0)),
        ],
        out_specs=pl.BlockSpec((None, None, ATTN_BLOCK, 2 * ATTN_BLOCK), lambda g, h: (g, h, 0, 0)),
        out_shape=jax.ShapeDtypeStruct((ng, heads, ATTN_BLOCK, 2 * ATTN_BLOCK), F32),
        compiler_params=_params(("arbitrary", "arbitrary")),
        name="attn_bias",
    )(rel_bias, buckets)


def _attn_kernel(q_ref, kc_ref, kp_ref, vc_ref, vp_ref, bias_ref, o_ref, lse_ref, *, heads):
    has_prev = pl.program_id(1) > 0
    scale = HEAD_DIM ** -0.5
    lane = lax.broadcasted_iota(jnp.int32, (ATTN_BLOCK, ATTN_BLOCK), 1)
    lse_all = jnp.zeros((ATTN_BLOCK, ATTN_BLOCK), F32)
    ones = jnp.ones((ATTN_BLOCK, HEAD_DIM), BF16)
    for h in range(heads):
        cols = slice(h * HEAD_DIM, (h + 1) * HEAD_DIM)
        q = q_ref[:, cols]
        s_prev = _dot_nt(q, kp_ref[:, cols]) * scale + bias_ref[h, :, :ATTN_BLOCK]
        s_prev = jnp.where(has_prev, s_prev, NEG_INF)
        s_cur = _dot_nt(q, kc_ref[:, cols]) * scale + bias_ref[h, :, ATTN_BLOCK:]
        m = jnp.maximum(jnp.max(s_prev, axis=-1, keepdims=True), jnp.max(s_cur, axis=-1, keepdims=True))
        p_prev = jnp.exp(s_prev - m).astype(BF16)
        p_cur = jnp.exp(s_cur - m).astype(BF16)
        ov = (_dot(p_prev, jnp.concatenate([vp_ref[:, cols], ones], axis=1))
              + _dot(p_cur, jnp.concatenate([vc_ref[:, cols], ones], axis=1)))
        denom = ov[:, HEAD_DIM:]
        o_ref[:, cols] = (ov[:, :HEAD_DIM] / denom).astype(o_ref.dtype)
        lse_all = jnp.where(lane == h, m + jnp.log(denom[:, 0:1]), lse_all)
    lse_ref[...] = lse_all


def _attn_group(proj, bias_g, sub_len, heads):
    rows = proj.shape[0]
    width = heads * HEAD_DIM
    nb = sub_len // ATTN_BLOCK
    blk = (ATTN_BLOCK, width)

    def cur(which):
        return lambda s, i: (s * nb + i, which)

    def prev(which):
        return lambda s, i: (s * nb + jnp.maximum(i - 1, 0), which)

    return pl.pallas_call(
        functools.partial(_attn_kernel, heads=heads),
        grid=(rows // sub_len, nb),
        in_specs=[
            pl.BlockSpec(blk, cur(0)),
            pl.BlockSpec(blk, cur(1)),
            pl.BlockSpec(blk, prev(1)),
            pl.BlockSpec(blk, cur(2)),
            pl.BlockSpec(blk, prev(2)),
            pl.BlockSpec((heads, ATTN_BLOCK, 2 * ATTN_BLOCK), lambda s, i: (0, 0, 0)),
        ],
        out_specs=[
            pl.BlockSpec(blk, lambda s, i: (s * nb + i, 0)),
            pl.BlockSpec((ATTN_BLOCK, ATTN_BLOCK), lambda s, i: (s * nb + i, 0)),
        ],
        out_shape=[jax.ShapeDtypeStruct((rows, width), BF16), jax.ShapeDtypeStruct((rows, ATTN_BLOCK), F32)],
        compiler_params=_params(("parallel", "arbitrary")),
        name="attn_group",
    )(proj, proj, proj, proj, proj, bias_g)


def _merge_outproj_kernel(*refs, heads, dilations):
    ng = len(dilations)
    o_refs, l_refs = refs[:ng], refs[ng : 2 * ng]
    w_ref, h_ref, ga_ref, gb_ref, h_out_ref, yn_ref, merged_even_ref, merged_odd_ref, alpha_ref, acc_ref = refs[2 * ng :]
    bm = acc_ref.shape[1]
    ii = pl.program_id(0)

    @pl.when(ii == 0)
    def _():
        merged_odd_ref[...] = jnp.zeros_like(merged_odd_ref)

    def step(merge_into_ref, multiply_ref):
        for g, d in enumerate(dilations):
            for r in range(d):
                alpha_ref[g, _strided_rows(r, bm // d, d), :] = l_refs[g][r]
        lses = [alpha_ref[g] for g in range(ng)]
        mx = functools.reduce(jnp.maximum, lses)
        es = [jnp.exp(l - mx) for l in lses]
        inv = 1.0 / functools.reduce(jnp.add, es)
        for g in range(ng):
            alpha_ref[g] = es[g] * inv
        for g, d in enumerate(dilations):
            for r in range(d):
                rows = _strided_rows(r, bm // d, d)
                alpha = alpha_ref[g, rows, :]
                for h in range(heads):
                    cols = slice(h * HEAD_DIM, (h + 1) * HEAD_DIM)
                    term = alpha[:, h : h + 1] * o_refs[g][r, :, cols].astype(F32)
                    if g == 0:
                        acc_ref[h, rows, :] = term
                    else:
                        acc_ref[h, rows, :] += term
        for h in range(heads):
            merge_into_ref[:, h * HEAD_DIM : (h + 1) * HEAD_DIM] = acc_ref[h].astype(merge_into_ref.dtype)

        y = _dot(multiply_ref[...], w_ref[...])
        h_new = h_ref[...] + _rms(y, ga_ref[...])
        h_out_ref[...] = h_new
        yn_ref[...] = _rms(h_new, gb_ref[...]).astype(yn_ref.dtype)

    pl.when(ii % 2 == 0)(functools.partial(step, merged_even_ref, merged_odd_ref))
    pl.when(ii % 2 == 1)(functools.partial(step, merged_odd_ref, merged_even_ref))


def _merge_outproj(outs, lses, w_out, h, gain_post, gain_next, heads, batch, seq, dilations, bm=256):
    width = heads * HEAD_DIM
    n = w_out.shape[1]
    bm = _tile(seq, bm)
    nt = seq // bm
    tiles = batch * nt

    def in_map(ii):
        t = jnp.minimum(ii, tiles - 1)
        return (t // nt, 0, t % nt, 0)

    row = pl.BlockSpec((bm, n), lambda ii: (jnp.maximum(ii - 1, 0), 0))
    vec = pl.BlockSpec((1, n), lambda ii: (0, 0))
    o_specs = [pl.BlockSpec((None, d, bm // d, width), in_map) for d in dilations]
    l_specs = [pl.BlockSpec((None, d, bm // d, ATTN_BLOCK), in_map) for d in dilations]
    outs = [o.reshape(batch, d, seq // d, width) for o, d in zip(outs, dilations)]
    lses = [l.reshape(batch, d, seq // d, ATTN_BLOCK) for l, d in zip(lses, dilations)]
    return pl.pallas_call(
        functools.partial(_merge_outproj_kernel, heads=heads, dilations=dilations),
        grid=(tiles + 1,),
        in_specs=o_specs + l_specs + [pl.BlockSpec((width, n), lambda ii: (0, 0), pipeline_mode=pl.Buffered(1)),
                                      row, vec, vec],
        out_specs=[row, row],
        out_shape=[jax.ShapeDtypeStruct((batch * seq, n), F32), jax.ShapeDtypeStruct((batch * seq, n), BF16)],
        scratch_shapes=[
            pltpu.VMEM((bm, width), BF16),
            pltpu.VMEM((bm, width), BF16),
            pltpu.VMEM((len(dilations), bm, ATTN_BLOCK), F32),
            pltpu.VMEM((heads, bm, HEAD_DIM), F32),
        ],
        compiler_params=_params(("arbitrary",)),
        name="attn_merge_outproj",
    )(*outs, *lses, w_out, h, gain_post.reshape(1, n), gain_next.reshape(1, n))


def _hgrn_query(acc):
    return _silu(acc) * (HEAD_DIM ** -0.5)


def _hgrn_forget(acc, lb_logits, *, layer):
    p = jnp.exp(lb_logits - jnp.max(lb_logits, axis=0, keepdims=True))
    p = p / jnp.sum(p, axis=0, keepdims=True)
    lb = jnp.sum(p[: layer + 1], axis=0, keepdims=True) - p[0:1]
    return lb + (1.0 - lb) * jax.nn.sigmoid(acc)


def _hgrn_rec_kernel(q_ref, f_ref, v_ref, g_ref, gain_ref, o_ref, state_ref, b_ref, *, heads_per_step, chunks):
    @pl.when(pl.program_id(2) == 0)
    def _():
        state_ref[...] = jnp.zeros_like(state_ref)

    row = lax.broadcasted_iota(jnp.int32, (HG_CHUNK, HG_CHUNK), 0)
    col = lax.broadcasted_iota(jnp.int32, (HG_CHUNK, HG_CHUNK), 1)
    causal = row >= col
    tri = causal.astype(BF16)
    tri2 = jnp.concatenate([tri, tri], axis=1)
    mid = HG_CHUNK // 2
    gain = gain_ref[...]
    heads = range(heads_per_step)
    cols = [slice(h * HEAD_DIM, (h + 1) * HEAD_DIM) for h in heads]

    for c in range(chunks):
        rows = pl.ds(c * HG_CHUNK, HG_CHUNK)
        log_f = jnp.log(f_ref[rows, :])
        hi = log_f.astype(BF16)
        lo = (log_f - hi.astype(F32)).astype(BF16)
        b_ref[rows, :] = _dot(tri2, jnp.concatenate([hi, lo], axis=0))

    def chunk(c, carry):
        rows = pl.ds(pl.multiple_of(c * HG_CHUNK, HG_CHUNK), HG_CHUNK)
        b = [b_ref[rows, cols[h]] for h in heads]
        b_mid = [b[h][mid - 1 : mid, :] for h in heads]
        b_last = [b[h][HG_CHUNK - 1 :, :] for h in heads]
        key = [1.0 - f_ref[rows, cols[h]] for h in heads]
        q = [q_ref[rows, cols[h]] for h in heads]
        v = [v_ref[rows, cols[h]] for h in heads]
        scores = [_dot_nt((q[h] * jnp.exp(b[h] - b_mid[h])).astype(BF16),
                          (key[h] * jnp.exp(b_mid[h] - b[h])).astype(BF16)) for h in heads]
        probs = [jnp.where(causal, scores[h], 0.0).astype(BF16) for h in heads]
        state = [state_ref[h] for h in heads]
        o = [_dot(jnp.concatenate([(q[h] * jnp.exp(b[h])).astype(BF16), probs[h]], axis=1),
                  jnp.concatenate([state[h].astype(BF16), v[h]], axis=0)) for h in heads]
        update = [_dot_tn((key[h] * jnp.exp(b_last[h] - b[h])).astype(BF16), v[h]) for h in heads]
        for h in heads:
            decay = jnp.transpose(jnp.broadcast_to(jnp.exp(b_last[h]), (8, HEAD_DIM)))[:, 0:1]
            state_ref[h] = state[h] * decay + update[h]
        for h in heads:
            o_ref[rows, cols[h]] = (_rms(o[h], gain) * g_ref[rows, cols[h]]).astype(o_ref.dtype)
        return carry

    lax.fori_loop(0, chunks, chunk, 0)


def _hgrn_recurrence(q, f, v, g, out_gain, batch, seq, heads_per_step=16, ts=512):
    m, width = q.shape
    heads = width // HEAD_DIM
    heads_per_step = math.gcd(heads, heads_per_step)
    ts = _tile(seq, ts)
    bw = heads_per_step * HEAD_DIM
    nt = seq // ts
    spec = pl.BlockSpec((ts, bw), lambda b, hg, t: (b * nt + t, hg))
    return pl.pallas_call(
        functools.partial(_hgrn_rec_kernel, heads_per_step=heads_per_step, chunks=ts // HG_CHUNK),
        grid=(batch, heads // heads_per_step, nt),
        in_specs=[spec, spec, spec, spec, pl.BlockSpec((1, HEAD_DIM), lambda b, hg, t: (0, 0))],
        out_specs=spec,
        out_shape=jax.ShapeDtypeStruct((m, width), BF16),
        scratch_shapes=[pltpu.VMEM((heads_per_step, HEAD_DIM, HEAD_DIM), F32), pltpu.VMEM((ts, bw), F32)],
        compiler_params=_params(("parallel", "parallel", "arbitrary")),
        name="hgrn_recurrence",
    )(q, f, v, g, out_gain.reshape(1, HEAD_DIM))


def _cast_slice(w_slice):
    return lax.optimization_barrier(w_slice).astype(BF16)


def _side_rows(rows, steps):
    rb = 16
    while rows % rb or rows // rb > steps:
        rb += 16
    return rb


def _head_cols(width, bn, most):
    return min(most // bn, width // bn // 2) * bn


def kernel(x, norm_gains, rel_bias, attn_w_in, attn_w_out, hgrn_w_in, hgrn_lb_logits, hgrn_out_gain, hgrn_w_out,
           ffn_w_in, ffn_w_out):
    batch, seq, d = x.shape
    m = batch * seq
    depth = norm_gains.shape[0]
    attn_heads = attn_w_out.shape[1] // HEAD_DIM
    attn_width = attn_heads * HEAD_DIM
    dilations = tuple(dil for _, dil in DILATED_GROUPS)
    d_ff = ffn_w_out.shape[1]
    hg_width = hgrn_w_in.shape[2] // 4
    ffn_head = _head_cols(d_ff, FFN_BN, FFN_HEAD_COLS)
    hg_bn = _tile(hg_width, PROJ_BN)
    hg_head = _head_cols(hg_width, hg_bn, HGRN_HEAD_COLS)
    attn_head = _head_cols(3 * attn_width, _tile(3 * attn_width, PROJ_BN), ATTN_HEAD_COLS)
    h = x.reshape(m, d)
    proj_tiles = m // _tile(m, 1024)
    yn = q_head = hidden_head = None
    bias = None
    for i in range(depth):
        gains = norm_gains[i]
        if i % 2 == 0:
            if bias is None:
                bias = _bias_tables(rel_bias, attn_heads)
            g_head = dilations.index(1) if attn_head and 1 in dilations else None
            w_head = None if g_head is None else _cast_slice(
                attn_w_in[i // 2, :, g_head * 3 * attn_width : g_head * 3 * attn_width + attn_head])
            yns, proj_head = _prenorm_dilated(h, gains[0], batch, seq, dilations, w_head)
            outs, lses = [], []
            attn_bn = _tile(3 * attn_width, PROJ_BN)
            for g, dil in enumerate(dilations):
                head = proj_head if g == g_head else None
                steps = (3 * attn_width // attn_bn + 1 - (head is not None)) * proj_tiles
                last = g == len(dilations) - 1
                proj = _proj(yns[g], attn_w_in, i // 2, [g * 3 * attn_width], 3 * attn_width, _identity, BF16,
                             bn=PROJ_BN, name="attn_in", head=head,
                             side_cast=(attn_w_out, i // 2, _side_rows(attn_width, steps)) if last else None)
                if last:
                    proj, w_out_bf16 = proj
                o, l = _attn_group(proj, bias[g], seq // dil, attn_heads)
                outs.append(o)
                lses.append(l)
            h, yn = _merge_outproj(outs, lses, w_out_bf16, h, gains[1], gains[2], attn_heads, batch, seq, dilations)
        else:
            if yn is None:
                yn = _prenorm(h, gains[0])
            hproj = functools.partial(_proj, yn, hgrn_w_in, i // 2, bn=PROJ_BN)
            q = hproj([0], hg_width, _hgrn_query, F32, name="hgrn_in_q", head=q_head)
            f = hproj([hg_width], hg_width, functools.partial(_hgrn_forget, layer=i), F32, aux=hgrn_lb_logits,
                      name="hgrn_in_f")
            v = hproj([2 * hg_width], hg_width, _identity, BF16, name="hgrn_in_i")
            steps = (hg_width // hg_bn + 1) * proj_tiles
            g, w_out_bf16 = hproj([3 * hg_width], hg_width, _silu, F32, name="hgrn_in_g",
                                  side_cast=(hgrn_w_out, i // 2, _side_rows(hg_width, steps)))
            o = _hgrn_recurrence(q, f, v, g, hgrn_out_gain[i // 2], batch, seq)
            y = _matmul(o, w_out_bf16[None], 0, F32, name="hgrn_out")
            if ffn_head:
                w_heads = [_cast_slice(ffn_w_in[i, :, c : c + ffn_head]) for c in (0, d_ff)]
                h, yn, hidden_head = _postnorm_head(h, y, gains[1], gains[2], w_heads, _swiglu, BF16,
                                                    name="postnorm_ffn_head")
            else:
                h, yn = _postnorm(h, y, gains[1], gains[2])
        ffn_steps = (d_ff // FFN_BN + 1 - (hidden_head is not None)) * (m // _tile(m, 2048))
        hidden, w_out_bf16 = _proj(yn, ffn_w_in, i, [0, d_ff], d_ff, _swiglu, BF16, bm=2048, bn=FFN_BN,
                                   name="ffn_in", head=hidden_head,
                                   side_cast=(ffn_w_out, i, _side_rows(d_ff, ffn_steps)))
        y = _matmul(hidden, w_out_bf16[None], 0, F32, bm=512, bn=512, name="ffn_out")
        yn = q_head = hidden_head = None
        if i + 1 < depth and (i + 1) % 2 == 1:
            if hg_head:
                w_heads = [_cast_slice(hgrn_w_in[(i + 1) // 2, :, :hg_head])]
                h, yn, q_head = _postnorm_head(h, y, gains[3], norm_gains[i + 1, 0], w_heads, _hgrn_query, F32,
                                               name="postnorm_hgrn_head")
            else:
                h, yn = _postnorm(h, y, gains[3], norm_gains[i + 1, 0])
        else:
            h = _postnorm(h, y, gains[3], None)
    return h.reshape(batch, seq, d)
```

```python
import functools
import math

import numpy as np
import jax
import jax.numpy as jnp
from jax import lax
from jax.experimental import pallas as pl
from jax.experimental.pallas import tpu as pltpu

RMS_EPS = 1e-6
NEG_INF = -1e30

DILATED_GROUPS = ((128, 1), (512, 4), (2048, 16))
HEAD_DIM = 128
ATTN_BLOCK = 128
NUM_BUCKETS = 32
MAX_EXACT = 16
REL_MAX_DISTANCE = 2048
HG_CHUNK = 64
PROJ_ROW_CHUNKS = 8
PROJ_BN = 1024
FFN_BN = 256
FFN_HEAD_COLS = 1024
HGRN_HEAD_COLS = 1024
ATTN_HEAD_COLS = 1024

V7X_VMEM_BYTES = 64 * 1024 * 1024
VMEM_LIMIT = V7X_VMEM_BYTES - 8 * 1024 * 1024
LANES = 128
SUBLANES = 8
BF16_SUBLANES = 16

BF16 = jnp.bfloat16
F32 = jnp.float32


def _params(semantics, vmem=VMEM_LIMIT):
    return pltpu.CompilerParams(dimension_semantics=semantics, vmem_limit_bytes=vmem)


def _tile(n, pref):
    if n <= pref:
        return n
    t = pref - pref % LANES
    while t >= LANES:
        if n % t == 0:
            return t
        t -= LANES
    raise ValueError(f"no lane-multiple tile of {n} below {pref}")


def _dot(a, b):
    return jnp.dot(a, b, preferred_element_type=F32)


def _dot_nt(a, b):
    return lax.dot_general(a, b, (((1,), (1,)), ((), ())), preferred_element_type=F32)


def _dot_tn(a, b):
    return lax.dot_general(a, b, (((0,), (0,)), ((), ())), preferred_element_type=F32)


def _rms(x, gain):
    return x * lax.rsqrt(jnp.mean(x * x, axis=-1, keepdims=True) + RMS_EPS) * gain


def _silu(x):
    return x * jax.nn.sigmoid(x)


def _strided_rows(r, count, stride):
    return pl.ds(r, count) if stride == 1 else pl.ds(r, count, stride=stride)


def _prenorm_kernel(x_ref, g_ref, o_ref):
    o_ref[...] = _rms(x_ref[...], g_ref[...]).astype(o_ref.dtype)


def _prenorm(x, gain, tr=256):
    m, d = x.shape
    tr = _tile(m, tr)
    return pl.pallas_call(
        _prenorm_kernel,
        grid=(m // tr,),
        in_specs=[pl.BlockSpec((tr, d), lambda i: (i, 0)), pl.BlockSpec((1, d), lambda i: (0, 0))],
        out_specs=pl.BlockSpec((tr, d), lambda i: (i, 0)),
        out_shape=jax.ShapeDtypeStruct((m, d), BF16),
        compiler_params=_params(("parallel",)),
        name="prenorm",
    )(x, gain.reshape(1, d))


def _prenorm_dilated_kernel(x_ref, g_ref, *refs, dilations, has_head):
    w_ref = refs[0] if has_head else None
    out_refs = refs[has_head : has_head + len(dilations)]
    head_ref = refs[has_head + len(dilations)] if has_head else None
    y_ref = refs[-1]
    y = _rms(x_ref[...], g_ref[...])
    tr = y.shape[0]
    if has_head:
        head_ref[...] = _dot(y.astype(BF16), w_ref[...]).astype(head_ref.dtype)
    for c in range(y_ref.shape[0]):
        cols = slice(c * LANES, (c + 1) * LANES)
        y_ref[c] = y[:, cols]
        for o_ref, d in zip(out_refs, dilations):
            for r in range(d):
                o_ref[r, :, cols] = y_ref[c, _strided_rows(r, tr // d, d), :].astype(o_ref.dtype)


def _prenorm_dilated(x, gain, batch, seq, dilations, w_head=None, tr=256):
    m, d = x.shape
    tr = _tile(seq, tr)
    nt = seq // tr
    has_head = w_head is not None
    head_in = [pl.BlockSpec(w_head.shape, lambda i: (0, 0), pipeline_mode=pl.Buffered(1))] if has_head else []
    head_out = [pl.BlockSpec((tr, w_head.shape[1]), lambda i: (i, 0))] if has_head else []
    head_shape = [jax.ShapeDtypeStruct((m, w_head.shape[1]), BF16)] if has_head else []
    outs = pl.pallas_call(
        functools.partial(_prenorm_dilated_kernel, dilations=dilations, has_head=has_head),
        grid=(m // tr,),
        in_specs=[pl.BlockSpec((tr, d), lambda i: (i, 0)), pl.BlockSpec((1, d), lambda i: (0, 0))] + head_in,
        out_specs=[pl.BlockSpec((None, dil, tr // dil, d), lambda i: (i // nt, 0, i % nt, 0)) for dil in dilations]
        + head_out,
        out_shape=[jax.ShapeDtypeStruct((batch, dil, seq // dil, d), BF16) for dil in dilations] + head_shape,
        scratch_shapes=[pltpu.VMEM((d // LANES, tr, LANES), F32)],
        compiler_params=_params(("parallel",)),
        name="prenorm_dilated",
    )(x, gain.reshape(1, d), *([w_head] if has_head else []))
    return [o.reshape(m, d) for o in outs[: len(dilations)]], (outs[-1] if has_head else None)


def _postnorm_kernel(h_ref, y_ref, ga_ref, gb_ref, h_out_ref, yn_ref):
    h = h_ref[...] + _rms(y_ref[...], ga_ref[...])
    h_out_ref[...] = h
    yn_ref[...] = _rms(h, gb_ref[...]).astype(yn_ref.dtype)


def _postnorm_last_kernel(h_ref, y_ref, ga_ref, h_out_ref):
    h_out_ref[...] = h_ref[...] + _rms(y_ref[...], ga_ref[...])


def _postnorm(h, y, gain_post, gain_next, tr=256):
    m, d = h.shape
    tr = _tile(m, tr)
    row = pl.BlockSpec((tr, d), lambda i: (i, 0))
    vec = pl.BlockSpec((1, d), lambda i: (0, 0))
    if gain_next is None:
        return pl.pallas_call(
            _postnorm_last_kernel,
            grid=(m // tr,),
            in_specs=[row, row, vec],
            out_specs=row,
            out_shape=jax.ShapeDtypeStruct((m, d), F32),
            compiler_params=_params(("parallel",)),
            name="postnorm_last",
        )(h, y, gain_post.reshape(1, d))
    return pl.pallas_call(
        _postnorm_kernel,
        grid=(m // tr,),
        in_specs=[row, row, vec, vec],
        out_specs=[row, row],
        out_shape=[jax.ShapeDtypeStruct((m, d), F32), jax.ShapeDtypeStruct((m, d), BF16)],
        compiler_params=_params(("parallel",)),
        name="postnorm",
    )(h, y, gain_post.reshape(1, d), gain_next.reshape(1, d))


def _postnorm_head_kernel(h_ref, y_ref, ga_ref, gb_ref, *refs, n_w, epilogue):
    w_refs = refs[:n_w]
    h_out_ref, yn_ref, head_ref = refs[n_w:]
    h = h_ref[...] + _rms(y_ref[...], ga_ref[...])
    h_out_ref[...] = h
    yn = _rms(h, gb_ref[...]).astype(yn_ref.dtype)
    yn_ref[...] = yn
    head_ref[...] = epilogue(*[_dot(yn, w_ref[...]) for w_ref in w_refs]).astype(head_ref.dtype)


def _postnorm_head(h, y, gain_post, gain_next, w_heads, epilogue, out_dtype, tr=256, name="postnorm_head"):
    m, d = h.shape
    tr = _tile(m, tr)
    head_cols = w_heads[0].shape[1]
    row = pl.BlockSpec((tr, d), lambda i: (i, 0))
    vec = pl.BlockSpec((1, d), lambda i: (0, 0))
    w_spec = pl.BlockSpec((d, head_cols), lambda i: (0, 0), pipeline_mode=pl.Buffered(1))
    return pl.pallas_call(
        functools.partial(_postnorm_head_kernel, n_w=len(w_heads), epilogue=epilogue),
        grid=(m // tr,),
        in_specs=[row, row, vec, vec] + [w_spec] * len(w_heads),
        out_specs=[row, row, pl.BlockSpec((tr, head_cols), lambda i: (i, 0))],
        out_shape=[jax.ShapeDtypeStruct((m, d), F32), jax.ShapeDtypeStruct((m, d), BF16),
                   jax.ShapeDtypeStruct((m, head_cols), out_dtype)],
        compiler_params=_params(("parallel",)),
        name=name,
    )(h, y, gain_post.reshape(1, d), gain_next.reshape(1, d), *w_heads)


def _matmul_kernel(x_ref, w_ref, o_ref):
    o_ref[...] = _dot(x_ref[...], w_ref[...]).astype(o_ref.dtype)


def _matmul(x, w, layer, out_dtype, bm=1024, bn=1024, name="matmul", w_resident=False):
    m, k = x.shape
    n = w.shape[2]
    bm, bn = _tile(m, bm), _tile(n, bn)
    if w_resident:
        grid = (n // bn, m // bm)
        x_spec = pl.BlockSpec((bm, k), lambda j, i: (i, 0))
        w_spec = pl.BlockSpec((None, k, bn), lambda j, i: (layer, 0, j), pipeline_mode=pl.Buffered(1))
        o_spec = pl.BlockSpec((bm, bn), lambda j, i: (i, j))
    else:
        grid = (m // bm, n // bn)
        x_spec = pl.BlockSpec((bm, k), lambda i, j: (i, 0))
        w_spec = pl.BlockSpec((None, k, bn), lambda i, j: (layer, 0, j))
        o_spec = pl.BlockSpec((bm, bn), lambda i, j: (i, j))
    return pl.pallas_call(
        _matmul_kernel,
        grid=grid,
        in_specs=[x_spec, w_spec],
        out_specs=o_spec,
        out_shape=jax.ShapeDtypeStruct((m, n), out_dtype),
        compiler_params=_params(("parallel", "arbitrary")),
        name=name,
    )(x, w)


def _proj_kernel(*refs, n_w, has_aux, has_head, has_side, copy_steps, epilogue):
    x_ref, w_refs = refs[0], refs[1 : 1 + n_w]
    aux_ref = refs[1 + n_w] if has_aux else None
    head_ref = refs[1 + n_w + has_aux] if has_head else None
    n_in = 1 + n_w + has_aux + has_head + has_side
    o_ref = refs[n_in]
    wb_refs = refs[n_in + 1 + has_side :]
    jj, i = pl.program_id(0), pl.program_id(1)
    j0 = max(copy_steps - 1, 0)
    slab = w_refs[0].shape[0]
    if has_side:
        refs[n_in + 1][...] = refs[n_in - 1][...].astype(refs[n_in + 1].dtype)

    if has_head:
        @pl.when(jj <= j0)
        def _():
            o_ref[...] = head_ref[...]

    @pl.when((jj >= j0) & (jj < pl.num_programs(0) - 1))
    def _():
        rows = pl.ds(pl.multiple_of(i * slab, slab), slab)
        for w_ref, wb_ref in zip(w_refs, wb_refs):
            wb_ref[(jj - j0) % 2, rows, :] = w_ref[...].astype(wb_ref.dtype)

    @pl.when(jj > j0)
    def _():
        chunk = x_ref.shape[0] // PROJ_ROW_CHUNKS
        for r in range(PROJ_ROW_CHUNKS):
            rows = slice(r * chunk, (r + 1) * chunk)
            x = x_ref[rows, :]
            accs = [_dot(x, wb_ref[(jj - j0 - 1) % 2]) for wb_ref in wb_refs]
            out = epilogue(*accs, aux_ref[...]) if has_aux else epilogue(*accs)
            o_ref[rows, :] = out.astype(o_ref.dtype)


def _proj(x, w, layer, col_offsets, width, epilogue, out_dtype, aux=None, bm=1024, bn=1024, name="proj", head=None,
          side_cast=None):
    m, k = x.shape
    bm, bn = _tile(m, bm), _tile(width, bn)
    n_w = len(col_offsets)
    col0 = 0 if head is None else head.shape[1]
    kh = col0 // bn
    j0 = max(kh - 1, 0)
    nj, ni = (width - col0) // bn, m // bm
    slab = k // ni
    assert k % ni == 0 and slab % BF16_SUBLANES == 0, (k, ni)
    assert all(c % bn == 0 for c in col_offsets) and col0 % bn == 0, (col_offsets, col0, bn)

    def staging(jj):
        return (jj >= j0) & (jj < j0 + nj)

    def w_map(jj, i, c):
        row = jnp.where(jj < j0, 0, jnp.where(staging(jj), i, ni - 1))
        return (layer, row, jnp.clip(jj - j0, 0, nj - 1) + c + kh)

    def x_map(jj, i):
        return (jnp.where(jj > j0, i, 0), 0)

    def computed_col(jj):
        return kh + jnp.maximum(jj - j0 - 1, 0)

    def out_map(jj, i):
        if kh == 0:
            return (jnp.where(jj > 0, i, 0), computed_col(jj))
        return (i, jnp.where(jj <= j0, jj, computed_col(jj)))

    w_specs = [pl.BlockSpec((None, slab, bn), functools.partial(w_map, c=c // bn)) for c in col_offsets]
    aux_specs = [] if aux is None else [pl.BlockSpec((aux.shape[0], bn), lambda jj, i: (0, computed_col(jj)))]
    head_specs = [] if head is None else [
        pl.BlockSpec((bm, bn), lambda jj, i: (jnp.where(jj <= j0, i, ni - 1), jnp.minimum(jj, kh - 1)))]
    steps = (j0 + nj + 1) * ni
    side_in, side_out, side_shape, side_args = [], [], [], []
    if side_cast is not None:
        w2, layer2, rb = side_cast
        rows2, cols2 = w2.shape[1:]
        blocks = rows2 // rb
        assert rows2 % rb == 0 and blocks <= steps, (rows2, rb, steps)

        def side_block(jj, i):
            return jnp.minimum(jj * ni + i, blocks - 1)

        side_in = [pl.BlockSpec((None, rb, cols2), lambda jj, i: (layer2, side_block(jj, i), 0))]
        side_out = [pl.BlockSpec((rb, cols2), lambda jj, i: (side_block(jj, i), 0))]
        side_shape = [jax.ShapeDtypeStruct((rows2, cols2), BF16)]
        side_args = [w2]
    result = pl.pallas_call(
        functools.partial(_proj_kernel, n_w=n_w, has_aux=aux is not None, has_head=head is not None,
                          has_side=side_cast is not None, copy_steps=kh, epilogue=epilogue),
        grid=(j0 + nj + 1, ni),
        in_specs=[pl.BlockSpec((bm, k), x_map)] + w_specs + aux_specs + head_specs + side_in,
        out_specs=[pl.BlockSpec((bm, bn), out_map)] + side_out,
        out_shape=[jax.ShapeDtypeStruct((m, width), out_dtype)] + side_shape,
        scratch_shapes=[pltpu.VMEM((2, k, bn), BF16) for _ in range(n_w)],
        compiler_params=_params(("arbitrary", "arbitrary")),
        name=name,
    )(x, *([w] * n_w), *([] if aux is None else [aux]), *([] if head is None else [head]), *side_args)
    return result if side_cast is not None else result[0]


def _identity(acc):
    return acc


def _swiglu(gate, up):
    return _silu(gate) * up


def _t5_bucket_np(dist):
    dist = np.maximum(dist, 0)
    log_ratio = np.log(np.maximum(dist, 1).astype(np.float64) / MAX_EXACT) / math.log(REL_MAX_DISTANCE / MAX_EXACT)
    large = np.minimum(MAX_EXACT + (log_ratio * (NUM_BUCKETS - MAX_EXACT)).astype(np.int64), NUM_BUCKETS - 1)
    return np.where(dist < MAX_EXACT, dist, large)


def _bucket_tables():
    qi = np.arange(ATTN_BLOCK)[:, None]
    ki = np.arange(2 * ATTN_BLOCK)[None, :]
    rel = qi - ki + ATTN_BLOCK
    tables = []
    for window, dilation in DILATED_GROUPS:
        band = (rel >= 0) & (rel <= window // dilation)
        tables.append(np.where(band, _t5_bucket_np(rel * dilation), -1))
    return np.stack(tables).astype(np.int32)


def _bias_kernel(rb_ref, bucket_ref, o_ref, *, heads):
    g, h = pl.program_id(0), pl.program_id(1)
    bucket = bucket_ref[...]
    acc = jnp.full(bucket.shape, NEG_INF, F32)
    for b in range(NUM_BUCKETS):
        acc = jnp.where(bucket == b, rb_ref[b, g * heads + h], acc)
    o_ref[...] = acc


def _bias_tables(rel_bias, heads):
    ng = len(DILATED_GROUPS)
    buckets = jnp.asarray(_bucket_tables())
    return pl.pallas_call(
        functools.partial(_bias_kernel, heads=heads),
        grid=(ng, heads),
        in_specs=[
            pl.BlockSpec(memory_space=pltpu.SMEM),
            pl.BlockSpec((None, ATTN_BLOCK, 2 * ATTN_BLOCK), lambda g, h: (g, 0, 0)),
        ],
        out_specs=pl.BlockSpec((None, None, ATTN_BLOCK, 2 * ATTN_BLOCK), lambda g, h: (g, h, 0, 0)),
        out_shape=jax.ShapeDtypeStruct((ng, heads, ATTN_BLOCK, 2 * ATTN_BLOCK), F32),
        compiler_params=_params(("arbitrary", "arbitrary")),
        name="attn_bias",
    )(rel_bias, buckets)


def _attn_kernel(q_ref, kc_ref, kp_ref, vc_ref, vp_ref, bias_ref, o_ref, lse_ref, *, heads):
    has_prev = pl.program_id(1) > 0
    scale = HEAD_DIM ** -0.5
    lane = lax.broadcasted_iota(jnp.int32, (ATTN_BLOCK, ATTN_BLOCK), 1)
    lse_all = jnp.zeros((ATTN_BLOCK, ATTN_BLOCK), F32)
    ones = jnp.ones((ATTN_BLOCK, HEAD_DIM), BF16)
    for h in range(heads):
        cols = slice(h * HEAD_DIM, (h + 1) * HEAD_DIM)
        q = q_ref[:, cols]
        s_prev = _dot_nt(q, kp_ref[:, cols]) * scale + bias_ref[h, :, :ATTN_BLOCK]
        s_prev = jnp.where(has_prev, s_prev, NEG_INF)
        s_cur = _dot_nt(q, kc_ref[:, cols]) * scale + bias_ref[h, :, ATTN_BLOCK:]
        m = jnp.maximum(jnp.max(s_prev, axis=-1, keepdims=True), jnp.max(s_cur, axis=-1, keepdims=True))
        p_prev = jnp.exp(s_prev - m).astype(BF16)
        p_cur = jnp.exp(s_cur - m).astype(BF16)
        ov = (_dot(p_prev, jnp.concatenate([vp_ref[:, cols], ones], axis=1))
              + _dot(p_cur, jnp.concatenate([vc_ref[:, cols], ones], axis=1)))
        denom = ov[:, HEAD_DIM:]
        o_ref[:, cols] = (ov[:, :HEAD_DIM] / denom).astype(o_ref.dtype)
        lse_all = jnp.where(lane == h, m + jnp.log(denom[:, 0:1]), lse_all)
    lse_ref[...] = lse_all


def _attn_group(proj, bias, group, sub_len, heads):
    rows = proj.shape[0]
    width = heads * HEAD_DIM
    nb = sub_len // ATTN_BLOCK
    blk = (ATTN_BLOCK, width)

    def cur(which):
        return lambda s, i: (s * nb + i, which)

    def prev(which):
        return lambda s, i: (s * nb + jnp.maximum(i - 1, 0), which)

    return pl.pallas_call(
        functools.partial(_attn_kernel, heads=heads),
        grid=(rows // sub_len, nb),
        in_specs=[
            pl.BlockSpec(blk, cur(0)),
            pl.BlockSpec(blk, cur(1)),
            pl.BlockSpec(blk, prev(1)),
            pl.BlockSpec(blk, cur(2)),
            pl.BlockSpec(blk, prev(2)),
            pl.BlockSpec((None, heads, ATTN_BLOCK, 2 * ATTN_BLOCK), lambda s, i: (group, 0, 0, 0)),
        ],
        out_specs=[
            pl.BlockSpec(blk, lambda s, i: (s * nb + i, 0)),
            pl.BlockSpec((ATTN_BLOCK, ATTN_BLOCK), lambda s, i: (s * nb + i, 0)),
        ],
        out_shape=[jax.ShapeDtypeStruct((rows, width), BF16), jax.ShapeDtypeStruct((rows, ATTN_BLOCK), F32)],
        compiler_params=_params(("parallel", "arbitrary")),
        name="attn_group",
    )(proj, proj, proj, proj, proj, bias)


def _merge_outproj_kernel(*refs, heads, dilations):
    ng = len(dilations)
    o_refs, l_refs = refs[:ng], refs[ng : 2 * ng]
    w_ref, h_ref, ga_ref, gb_ref, h_out_ref, yn_ref, merged_even_ref, merged_odd_ref, alpha_ref, acc_ref = refs[2 * ng :]
    bm = acc_ref.shape[1]
    ii = pl.program_id(0)

    @pl.when(ii == 0)
    def _():
        merged_odd_ref[...] = jnp.zeros_like(merged_odd_ref)

    def step(merge_into_ref, multiply_ref):
        for g, d in enumerate(dilations):
            for r in range(d):
                alpha_ref[g, _strided_rows(r, bm // d, d), :] = l_refs[g][r]
        lses = [alpha_ref[g] for g in range(ng)]
        mx = functools.reduce(jnp.maximum, lses)
        es = [jnp.exp(l - mx) for l in lses]
        inv = 1.0 / functools.reduce(jnp.add, es)
        for g in range(ng):
            alpha_ref[g] = es[g] * inv
        for g, d in enumerate(dilations):
            for r in range(d):
                rows = _strided_rows(r, bm // d, d)
                alpha = alpha_ref[g, rows, :]
                for h in range(heads):
                    cols = slice(h * HEAD_DIM, (h + 1) * HEAD_DIM)
                    term = alpha[:, h : h + 1] * o_refs[g][r, :, cols].astype(F32)
                    if g == 0:
                        acc_ref[h, rows, :] = term
                    else:
                        acc_ref[h, rows, :] += term
        for h in range(heads):
            merge_into_ref[:, h * HEAD_DIM : (h + 1) * HEAD_DIM] = acc_ref[h].astype(merge_into_ref.dtype)

        y = _dot(multiply_ref[...], w_ref[...])
        h_new = h_ref[...] + _rms(y, ga_ref[...])
        h_out_ref[...] = h_new
        yn_ref[...] = _rms(h_new, gb_ref[...]).astype(yn_ref.dtype)

    pl.when(ii % 2 == 0)(functools.partial(step, merged_even_ref, merged_odd_ref))
    pl.when(ii % 2 == 1)(functools.partial(step, merged_odd_ref, merged_even_ref))


def _merge_outproj(outs, lses, w_out, h, gain_post, gain_next, heads, batch, seq, dilations, bm=256):
    width = heads * HEAD_DIM
    n = w_out.shape[1]
    bm = _tile(seq, bm)
    nt = seq // bm
    tiles = batch * nt

    def in_map(ii):
        t = jnp.minimum(ii, tiles - 1)
        return (t // nt, 0, t % nt, 0)

    row = pl.BlockSpec((bm, n), lambda ii: (jnp.maximum(ii - 1, 0), 0))
    vec = pl.BlockSpec((1, n), lambda ii: (0, 0))
    o_specs = [pl.BlockSpec((None, d, bm // d, width), in_map) for d in dilations]
    l_specs = [pl.BlockSpec((None, d, bm // d, ATTN_BLOCK), in_map) for d in dilations]
    outs = [o.reshape(batch, d, seq // d, width) for o, d in zip(outs, dilations)]
    lses = [l.reshape(batch, d, seq // d, ATTN_BLOCK) for l, d in zip(lses, dilations)]
    return pl.pallas_call(
        functools.partial(_merge_outproj_kernel, heads=heads, dilations=dilations),
        grid=(tiles + 1,),
        in_specs=o_specs + l_specs + [pl.BlockSpec((width, n), lambda ii: (0, 0), pipeline_mode=pl.Buffered(1)),
                                      row, vec, vec],
        out_specs=[row, row],
        out_shape=[jax.ShapeDtypeStruct((batch * seq, n), F32), jax.ShapeDtypeStruct((batch * seq, n), BF16)],
        scratch_shapes=[
            pltpu.VMEM((bm, width), BF16),
            pltpu.VMEM((bm, width), BF16),
            pltpu.VMEM((len(dilations), bm, ATTN_BLOCK), F32),
            pltpu.VMEM((heads, bm, HEAD_DIM), F32),
        ],
        compiler_params=_params(("arbitrary",)),
        name="attn_merge_outproj",
    )(*outs, *lses, w_out, h, gain_post.reshape(1, n), gain_next.reshape(1, n))


def _hgrn_query(acc):
    return _silu(acc) * (HEAD_DIM ** -0.5)


def _hgrn_forget(acc, lb_logits, *, layer):
    p = jnp.exp(lb_logits - jnp.max(lb_logits, axis=0, keepdims=True))
    p = p / jnp.sum(p, axis=0, keepdims=True)
    lb = jnp.sum(p[: layer + 1], axis=0, keepdims=True) - p[0:1]
    return lb + (1.0 - lb) * jax.nn.sigmoid(acc)


def _hgrn_rec_kernel(q_ref, f_ref, v_ref, g_ref, gain_ref, o_ref, state_ref, b_ref, *, heads_per_step, chunks):
    @pl.when(pl.program_id(2) == 0)
    def _():
        state_ref[...] = jnp.zeros_like(state_ref)

    row = lax.broadcasted_iota(jnp.int32, (HG_CHUNK, HG_CHUNK), 0)
    col = lax.broadcasted_iota(jnp.int32, (HG_CHUNK, HG_CHUNK), 1)
    causal = row >= col
    tri = causal.astype(BF16)
    tri2 = jnp.concatenate([tri, tri], axis=1)
    mid = HG_CHUNK // 2
    gain = gain_ref[...]
    heads = range(heads_per_step)
    cols = [slice(h * HEAD_DIM, (h + 1) * HEAD_DIM) for h in heads]

    for c in range(chunks):
        rows = pl.ds(c * HG_CHUNK, HG_CHUNK)
        log_f = jnp.log(f_ref[rows, :])
        hi = log_f.astype(BF16)
        lo = (log_f - hi.astype(F32)).astype(BF16)
        b_ref[rows, :] = _dot(tri2, jnp.concatenate([hi, lo], axis=0))

    def chunk(c, carry):
        rows = pl.ds(pl.multiple_of(c * HG_CHUNK, HG_CHUNK), HG_CHUNK)
        b = [b_ref[rows, cols[h]] for h in heads]
        b_mid = [b[h][mid - 1 : mid, :] for h in heads]
        b_last = [b[h][HG_CHUNK - 1 :, :] for h in heads]
        key = [1.0 - f_ref[rows, cols[h]] for h in heads]
        q = [q_ref[rows, cols[h]] for h in heads]
        v = [v_ref[rows, cols[h]] for h in heads]
        scores = [_dot_nt((q[h] * jnp.exp(b[h] - b_mid[h])).astype(BF16),
                          (key[h] * jnp.exp(b_mid[h] - b[h])).astype(BF16)) for h in heads]
        probs = [jnp.where(causal, scores[h], 0.0).astype(BF16) for h in heads]
        state = [state_ref[h] for h in heads]
        o = [_dot(jnp.concatenate([(q[h] * jnp.exp(b[h])).astype(BF16), probs[h]], axis=1),
                  jnp.concatenate([state[h].astype(BF16), v[h]], axis=0)) for h in heads]
        update = [_dot_tn((key[h] * jnp.exp(b_last[h] - b[h])).astype(BF16), v[h]) for h in heads]
        for h in heads:
            decay = jnp.transpose(jnp.broadcast_to(jnp.exp(b_last[h]), (SUBLANES, HEAD_DIM)))[:, 0:1]
            state_ref[h] = state[h] * decay + update[h]
        for h in heads:
            o_ref[rows, cols[h]] = (_rms(o[h], gain) * g_ref[rows, cols[h]]).astype(o_ref.dtype)
        return carry

    lax.fori_loop(0, chunks, chunk, 0)


def _hgrn_recurrence(q, f, v, g, out_gain, batch, seq, heads_per_step=16, ts=512):
    m, width = q.shape
    heads = width // HEAD_DIM
    heads_per_step = math.gcd(heads, heads_per_step)
    ts = _tile(seq, ts)
    bw = heads_per_step * HEAD_DIM
    nt = seq // ts
    spec = pl.BlockSpec((ts, bw), lambda b, hg, t: (b * nt + t, hg))
    return pl.pallas_call(
        functools.partial(_hgrn_rec_kernel, heads_per_step=heads_per_step, chunks=ts // HG_CHUNK),
        grid=(batch, heads // heads_per_step, nt),
        in_specs=[spec, spec, spec, spec, pl.BlockSpec((1, HEAD_DIM), lambda b, hg, t: (0, 0))],
        out_specs=spec,
        out_shape=jax.ShapeDtypeStruct((m, width), BF16),
        scratch_shapes=[pltpu.VMEM((heads_per_step, HEAD_DIM, HEAD_DIM), F32), pltpu.VMEM((ts, bw), F32)],
        compiler_params=_params(("parallel", "parallel", "arbitrary")),
        name="hgrn_recurrence",
    )(q, f, v, g, out_gain.reshape(1, HEAD_DIM))


def _cast_slice(w_slice):
    return lax.optimization_barrier(w_slice).astype(BF16)


def _side_rows(rows, steps):
    rb = BF16_SUBLANES
    while rows % rb or rows // rb > steps:
        rb += BF16_SUBLANES
    return rb


def _head_cols(width, bn, most):
    return min(most // bn, width // bn // 2) * bn


def kernel(x, norm_gains, rel_bias, attn_w_in, attn_w_out, hgrn_w_in, hgrn_lb_logits, hgrn_out_gain, hgrn_w_out,
           ffn_w_in, ffn_w_out):
    batch, seq, d = x.shape
    m = batch * seq
    depth = norm_gains.shape[0]
    attn_heads = attn_w_out.shape[1] // HEAD_DIM
    attn_width = attn_heads * HEAD_DIM
    dilations = tuple(dil for _, dil in DILATED_GROUPS)
    d_ff = ffn_w_out.shape[1]
    hg_width = hgrn_w_in.shape[2] // 4
    ffn_head = _head_cols(d_ff, FFN_BN, FFN_HEAD_COLS)
    hg_bn = _tile(hg_width, PROJ_BN)
    hg_head = _head_cols(hg_width, hg_bn, HGRN_HEAD_COLS)
    attn_head = _head_cols(3 * attn_width, _tile(3 * attn_width, PROJ_BN), ATTN_HEAD_COLS)
    h = x.reshape(m, d)
    proj_tiles = m // _tile(m, 1024)
    yn = q_head = hidden_head = None
    bias = None
    for i in range(depth):
        gains = norm_gains[i]
        if i % 2 == 0:
            if bias is None:
                bias = _bias_tables(rel_bias, attn_heads)
            g_head = dilations.index(1) if attn_head and 1 in dilations else None
            w_head = None if g_head is None else _cast_slice(
                attn_w_in[i // 2, :, g_head * 3 * attn_width : g_head * 3 * attn_width + attn_head])
            yns, proj_head = _prenorm_dilated(h, gains[0], batch, seq, dilations, w_head)
            outs, lses = [], []
            attn_bn = _tile(3 * attn_width, PROJ_BN)
            for g, dil in enumerate(dilations):
                head = proj_head if g == g_head else None
                steps = (3 * attn_width // attn_bn + 1 - (head is not None)) * proj_tiles
                last = g == len(dilations) - 1
                proj = _proj(yns[g], attn_w_in, i // 2, [g * 3 * attn_width], 3 * attn_width, _identity, BF16,
                             bn=PROJ_BN, name="attn_in", head=head,
                             side_cast=(attn_w_out, i // 2, _side_rows(attn_width, steps)) if last else None)
                if last:
                    proj, w_out_bf16 = proj
                o, l = _attn_group(proj, bias, g, seq // dil, attn_heads)
                outs.append(o)
                lses.append(l)
            h, yn = _merge_outproj(outs, lses, w_out_bf16, h, gains[1], gains[2], attn_heads, batch, seq, dilations)
        else:
            if yn is None:
                yn = _prenorm(h, gains[0])
            hproj = functools.partial(_proj, yn, hgrn_w_in, i // 2, bn=PROJ_BN)
            q = hproj([0], hg_width, _hgrn_query, F32, name="hgrn_in_q", head=q_head)
            f = hproj([hg_width], hg_width, functools.partial(_hgrn_forget, layer=i), F32, aux=hgrn_lb_logits,
                      name="hgrn_in_f")
            v = hproj([2 * hg_width], hg_width, _identity, BF16, name="hgrn_in_i")
            steps = (hg_width // hg_bn + 1) * proj_tiles
            g, w_out_bf16 = hproj([3 * hg_width], hg_width, _silu, F32, name="hgrn_in_g",
                                  side_cast=(hgrn_w_out, i // 2, _side_rows(hg_width, steps)))
            o = _hgrn_recurrence(q, f, v, g, hgrn_out_gain[i // 2], batch, seq)
            y = _matmul(o, w_out_bf16[None], 0, F32, bn=2048, name="hgrn_out", w_resident=True)
            if ffn_head:
                w_heads = [_cast_slice(ffn_w_in[i, :, c : c + ffn_head]) for c in (0, d_ff)]
                h, yn, hidden_head = _postnorm_head(h, y, gains[1], gains[2], w_heads, _swiglu, BF16,
                                                    name="postnorm_ffn_head")
            else:
                h, yn = _postnorm(h, y, gains[1], gains[2])
        ffn_steps = (d_ff // FFN_BN + 1 - (hidden_head is not None)) * (m // _tile(m, 2048))
        hidden, w_out_bf16 = _proj(yn, ffn_w_in, i, [0, d_ff], d_ff, _swiglu, BF16, bm=2048, bn=FFN_BN,
                                   name="ffn_in", head=hidden_head,
                                   side_cast=(ffn_w_out, i, _side_rows(d_ff, ffn_steps)))
        y = _matmul(hidden, w_out_bf16[None], 0, F32, bm=512, bn=1024, name="ffn_out", w_resident=True)
        yn = q_head = hidden_head = None
        if i + 1 < depth and (i + 1) % 2 == 1:
            if hg_head:
                w_heads = [_cast_slice(hgrn_w_in[(i + 1) // 2, :, :hg_head])]
                h, yn, q_head = _postnorm_head(h, y, gains[3], norm_gains[i + 1, 0], w_heads, _hgrn_query, F32,
                                               name="postnorm_hgrn_head")
            else:
                h, yn = _postnorm(h, y, gains[3], norm_gains[i + 1, 0])
        else:
            h = _postnorm(h, y, gains[3], None)
    return h.reshape(batch, seq, d)
```
